```python
import math
import jax
import jax.numpy as jnp
from jax import lax
import numpy as np

D_MODEL = 1024
BATCH = 16
SEQ = 2048
DEPTH = 2

GRID_W = 64
CTX_LEN = 256
HEAD_DIM = 64
N_GROUPS = 4
D_MIX = D_MODEL
GROUP_W = D_MIX // N_GROUPS
ATT_HEADS = GROUP_W // HEAD_DIM
ATT_KV_HEADS = 2
NA_HEADS = GROUP_W // HEAD_DIM
WIN_H = 8
WIN_W = 16
FNO_HEADS = 4
FNO_HEAD_W = GROUP_W // FNO_HEADS
SSM_GROUP = 16
SSM_GROUPS = GROUP_W // SSM_GROUP
SSM_STATE = 64
Q_BLOCK = 128
ROPE_THETA = 10000.0
D_FF = ((8 * D_MODEL // 3 + 255) // 256) * 256
N_MOD = 6
EPS = 1e-6
ATT_Q_W = ATT_HEADS * HEAD_DIM
ATT_KV_W = ATT_KV_HEADS * HEAD_DIM
NA_W = NA_HEADS * HEAD_DIM
PROJ_WIDTHS = (ATT_Q_W, ATT_KV_W, ATT_KV_W, NA_W, NA_W, NA_W, GROUP_W, GROUP_W)
IN_W = sum(PROJ_WIDTHS)

kernel_name = 'hybrid_parallel_heads_diffusion_block'


def _rmsnorm(t, g):
    tf = t.astype(jnp.float32)
    y = tf * lax.rsqrt(jnp.mean(tf * tf, axis=-1, keepdims=True) + EPS)
    return (y * g.astype(jnp.float32)).astype(t.dtype)


def _split_proj(p):
    outs, start = [], 0
    for w in PROJ_WIDTHS:
        outs.append(p[..., start:start + w])
        start += w
    return outs


def _heads(t, n_heads):
    b, n, _ = t.shape
    return t.reshape(b, n, n_heads, HEAD_DIM).transpose(0, 2, 1, 3)


def _merge_heads(t):
    b, h, n, d = t.shape
    return t.transpose(0, 2, 1, 3).reshape(b, n, h * d)


def _rope_tables(n_tokens):
    t = jnp.arange(n_tokens, dtype=jnp.int32)
    rows = (t // GRID_W).astype(jnp.float32)
    cols = (t % GRID_W).astype(jnp.float32)
    axis_dim = HEAD_DIM // 2
    inv_freq = ROPE_THETA ** (-jnp.arange(0, axis_dim, 2, dtype=jnp.float32) / axis_dim)
    ang_r = rows[:, None] * inv_freq[None, :]
    ang_c = cols[:, None] * inv_freq[None, :]
    return (jnp.cos(ang_r), jnp.sin(ang_r), jnp.cos(ang_c), jnp.sin(ang_c))


def _rotate(t, cos, sin):
    t1, t2 = jnp.split(t, 2, axis=-1)
    return jnp.concatenate([t1 * cos - t2 * sin, t2 * cos + t1 * sin], axis=-1)


def _apply_rope_2d(t, rope):
    cos_r, sin_r, cos_c, sin_c = rope
    tr, tc = jnp.split(t, 2, axis=-1)
    out = jnp.concatenate([_rotate(tr, cos_r, sin_r), _rotate(tc, cos_c, sin_c)], axis=-1)
    return out.astype(t.dtype)


def _gqa_mixer(q, k, v, qc, kc, vc, gq, gk, rope, need_ctx):
    b, n, _ = q.shape
    nc = qc.shape[1]
    rep = ATT_HEADS // ATT_KV_HEADS
    scale = HEAD_DIM ** -0.5
    qh = _apply_rope_2d(_rmsnorm(_heads(q, ATT_HEADS), gq), rope)
    kh = _apply_rope_2d(_rmsnorm(_heads(k, ATT_KV_HEADS), gk), rope)
    vh = _heads(v, ATT_KV_HEADS)
    kch = _rmsnorm(_heads(kc, ATT_KV_HEADS), gk)
    vch = _heads(vc, ATT_KV_HEADS)
    keys = jnp.concatenate([kh, kch], axis=2)
    vals = jnp.concatenate([vh, vch], axis=2)
    n_blk = n // Q_BLOCK
    q_blocks = jnp.moveaxis(qh.reshape(b, ATT_KV_HEADS, rep, n_blk, Q_BLOCK, HEAD_DIM), 3, 0)

    def block(qb):
        s = jnp.einsum('bgrqd,bgkd->bgrqk', qb, keys).astype(jnp.float32) * scale
        p = jax.nn.softmax(s, axis=-1).astype(vals.dtype)
        return jnp.einsum('bgrqk,bgkd->bgrqd', p, vals)

    o = lax.map(block, q_blocks)
    o = jnp.moveaxis(o, 0, 3).reshape(b, ATT_HEADS, n, HEAD_DIM)
    out = _merge_heads(o)
    out_c = None
    if need_ctx:
        qch = _rmsnorm(_heads(qc, ATT_HEADS), gq).reshape(b, ATT_KV_HEADS, rep, nc, HEAD_DIM)
        s = jnp.einsum('bgrqd,bgkd->bgrqk', qch, kch).astype(jnp.float32) * scale
        p = jax.nn.softmax(s, axis=-1).astype(vch.dtype)
        oc = jnp.einsum('bgrqk,bgkd->bgrqd', p, vch).reshape(b, ATT_HEADS, nc, HEAD_DIM)
        out_c = _merge_heads(oc)
    return out, out_c


def _na_mixer(q, k, v, qc, kc, vc, gq, gk, rel_bias, need_ctx):
    b, n, _ = q.shape
    rows = n // GRID_W
    kh_win = min(WIN_H, rows)
    n_loc = kh_win * WIN_W
    scale = HEAD_DIM ** -0.5
    qh = _rmsnorm(_heads(q, NA_HEADS), gq)
    kh = _rmsnorm(_heads(k, NA_HEADS), gk)
    vh = _heads(v, NA_HEADS)
    qch = _rmsnorm(_heads(qc, NA_HEADS), gq)
    kch = _rmsnorm(_heads(kc, NA_HEADS), gk)
    vch = _heads(vc, NA_HEADS)
    q_rows = jnp.moveaxis(qh.reshape(b, NA_HEADS, rows, GRID_W, HEAD_DIM), 2, 0)
    k_grid = kh.reshape(b, NA_HEADS, rows, GRID_W, HEAD_DIM)
    v_grid = vh.reshape(b, NA_HEADS, rows, GRID_W, HEAD_DIM)
    cols = jnp.arange(GRID_W)
    col_start = jnp.clip(cols - WIN_W // 2, 0, GRID_W - WIN_W)
    col_idx = col_start[:, None] + jnp.arange(WIN_W)[None, :]
    dc_idx = col_idx - cols[:, None] + (WIN_W - 1)

    def row_block(args):
        r, qr = args
        rs = jnp.clip(r - WIN_H // 2, 0, rows - kh_win)
        k_win = lax.dynamic_slice_in_dim(k_grid, rs, kh_win, axis=2)[:, :, :, col_idx]
        v_win = lax.dynamic_slice_in_dim(v_grid, rs, kh_win, axis=2)[:, :, :, col_idx]
        dr_idx = rs + jnp.arange(kh_win) - r + (WIN_H - 1)
        bias = rel_bias[:, dr_idx[None, :, None], dc_idx[:, None, :]]
        s_loc = jnp.einsum('bhqd,bhiqjd->bhqij', qr, k_win).astype(jnp.float32) * scale + bias.astype(jnp.float32)
        s_ctx = jnp.einsum('bhqd,bhkd->bhqk', qr, kch).astype(jnp.float32) * scale
        s = jnp.concatenate([s_loc.reshape(b, NA_HEADS, GRID_W, n_loc), s_ctx], axis=-1)
        p = jax.nn.softmax(s, axis=-1).astype(vh.dtype)
        p_loc = p[..., :n_loc].reshape(b, NA_HEADS, GRID_W, kh_win, WIN_W)
        return (jnp.einsum('bhqij,bhiqjd->bhqd', p_loc, v_win)
                + jnp.einsum('bhqk,bhkd->bhqd', p[..., n_loc:], vch))

    o = lax.map(row_block, (jnp.arange(rows), q_rows))
    o = jnp.moveaxis(o, 0, 2).reshape(b, NA_HEADS, n, HEAD_DIM)
    out = _merge_heads(o)
    out_c = None
    if need_ctx:
        s = jnp.einsum('bhqd,bhkd->bhqk', qch, kch).astype(jnp.float32) * scale
        p = jax.nn.softmax(s, axis=-1).astype(vch.dtype)
        out_c = _merge_heads(jnp.einsum('bhqk,bhkd->bhqd', p, vch))
    return out, out_c


def _fourier_mixer(f, w):
    b, n, _ = f.shape
    fh = f.astype(jnp.float32).reshape(b, n, FNO_HEADS, FNO_HEAD_W)
    y = jnp.fft.fftn(fh, axes=(1, 3), norm='ortho').real
    return (y.reshape(b, n, GROUP_W) @ w.astype(jnp.float32)).astype(f.dtype)


def _zoh(lam_re, lam_im, log_dt, b_re, b_im):
    lam = lax.complex(lam_re.astype(jnp.float32), lam_im.astype(jnp.float32))
    dt = jnp.exp(log_dt.astype(jnp.float32))[:, None]
    a_bar = jnp.exp(lam * dt)
    b_mat = lax.complex(b_re.astype(jnp.float32), b_im.astype(jnp.float32))
    b_bar = ((a_bar - 1.0) / lam)[..., None] * b_mat
    return a_bar, b_bar


def _diag_scan(u, a_bar, b_bar, h0, reverse):
    n = u.shape[1]
    bu = jnp.einsum('gpc,bngc->bngp', b_bar, u)
    if h0 is not None:
        edge = n - 1 if reverse else 0
        bu = bu.at[:, edge].add(a_bar * h0)
    a = jnp.broadcast_to(a_bar, (1, n) + a_bar.shape)

    def combine(e1, e2):
        a1, b1 = e1
        a2, b2 = e2
        return a1 * a2, a2 * b1 + b2

    _, h = lax.associative_scan(combine, (a, bu), axis=1, reverse=reverse)
    return h


def _glu(y, w_glu, b_glu):
    g = jax.nn.gelu(y)
    return g * jax.nn.sigmoid(g @ w_glu.astype(jnp.float32) + b_glu.astype(jnp.float32))


def _s5_mixer(u, uc, lam_re, lam_im, log_dt, b_re, b_im, c_re, c_im, d_skip, w_glu, b_glu, need_ctx):
    b, n, _ = u.shape
    nc = uc.shape[1]
    uf = u.astype(jnp.float32).reshape(b, n, SSM_GROUPS, SSM_GROUP)
    ucf = uc.astype(jnp.float32).reshape(b, nc, SSM_GROUPS, SSM_GROUP)
    d_g = d_skip.astype(jnp.float32).reshape(SSM_GROUPS, SSM_GROUP)
    y = uf * d_g
    yc = ucf * d_g
    for direction in range(2):
        reverse = direction == 1
        a_bar, b_bar = _zoh(lam_re[direction], lam_im[direction], log_dt[direction], b_re[direction], b_im[direction])
        c_mat = lax.complex(c_re[direction].astype(jnp.float32), c_im[direction].astype(jnp.float32))
        h_ctx = _diag_scan(ucf, a_bar, b_bar, None, reverse)
        h0 = h_ctx[:, 0] if reverse else h_ctx[:, -1]
        h_lat = _diag_scan(uf, a_bar, b_bar, h0, reverse)
        y = y + jnp.einsum('gcp,bngp->bngc', c_mat, h_lat).real
        if need_ctx:
            yc = yc + jnp.einsum('gcp,bngp->bngc', c_mat, h_ctx).real
    out = _glu(y.reshape(b, n, GROUP_W), w_glu, b_glu).astype(u.dtype)
    out_c = None
    if need_ctx:
        out_c = _glu(yc.reshape(b, nc, GROUP_W), w_glu, b_glu).astype(uc.dtype)
    return out, out_c


def _group_norm(o, g):
    b, n, _ = o.shape
    og = _rmsnorm(o.reshape(b, n, N_GROUPS, GROUP_W), g.reshape(N_GROUPS, GROUP_W))
    return og.reshape(b, n, D_MIX)


def _swiglu(h, w1, w3, w2):
    return (jax.nn.silu(h @ w1) * (h @ w3)) @ w2


def _layer(x, xc, c, c_ctx, rope, need_ctx, w_mod, b_mod, g_norm1, w_in, att_q_gain, att_k_gain,
           na_q_gain, na_k_gain, na_rel_bias, w_fourier, ssm_lam_re, ssm_lam_im, ssm_log_dt,
           ssm_b_re, ssm_b_im, ssm_c_re, ssm_c_im, ssm_d, w_glu, b_glu, g_group, w_out,
           g_norm2, w_ff1, w_ff3, w_ff2):
    sh1, sc1, ga1, sh2, sc2, ga2 = [m[:, None, :] for m in jnp.split(jax.nn.silu(c) @ w_mod + b_mod, N_MOD, axis=-1)]
    csh1, csc1, cga1, csh2, csc2, cga2 = jnp.split(jax.nn.silu(c_ctx) @ w_mod + b_mod, N_MOD, axis=-1)
    h = _rmsnorm(x, g_norm1) * (1.0 + sc1) + sh1
    hc = _rmsnorm(xc, g_norm1) * (1.0 + csc1) + csh1
    aq, ak, av, dq, dk, dv, fin, sin_ = _split_proj(h @ w_in)
    caq, cak, cav, cdq, cdk, cdv, cfin, csin = _split_proj(hc @ w_in)
    o_a, oc_a = _gqa_mixer(aq, ak, av, caq, cak, cav, att_q_gain, att_k_gain, rope, need_ctx)
    o_d, oc_d = _na_mixer(dq, dk, dv, cdq, cdk, cdv, na_q_gain, na_k_gain, na_rel_bias, need_ctx)
    o_f = _fourier_mixer(fin, w_fourier)
    o_s, oc_s = _s5_mixer(sin_, csin, ssm_lam_re, ssm_lam_im, ssm_log_dt, ssm_b_re, ssm_b_im,
                          ssm_c_re, ssm_c_im, ssm_d, w_glu, b_glu, need_ctx)
    o = _group_norm(jnp.concatenate([o_a, o_d, o_f, o_s.astype(o_a.dtype)], axis=-1), g_group)
    x = x + ga1 * (o @ w_out)
    h2 = _rmsnorm(x, g_norm2) * (1.0 + sc2) + sh2
    x = x + ga2 * _swiglu(h2, w_ff1, w_ff3, w_ff2)
    if need_ctx:
        oc_f = _fourier_mixer(cfin, w_fourier)
        oc = _group_norm(jnp.concatenate([oc_a, oc_d, oc_f, oc_s.astype(oc_a.dtype)], axis=-1), g_group)
        xc = xc + cga1 * (oc @ w_out)
        hc2 = _rmsnorm(xc, g_norm2) * (1.0 + csc2) + csh2
        xc = xc + cga2 * _swiglu(hc2, w_ff1, w_ff3, w_ff2)
    return x, xc


def setup_inputs(seed: int = 0) -> dict:
    key = jax.random.key(seed)
    ks = jax.random.split(key, 32)
    f32 = jnp.float32
    nrm = lambda k, s, sc: jax.random.normal(k, s, f32) * sc
    gain = lambda k, s: 1.0 + 0.02 * jax.random.normal(k, s, f32)
    lam_im0 = math.pi * jnp.arange(SSM_STATE, dtype=f32)
    return {
        'x': nrm(ks[0], (BATCH, SEQ, D_MODEL), 1.0),
        'c': nrm(ks[1], (BATCH, D_MODEL), 1.0),
        'ctx': nrm(ks[2], (BATCH, CTX_LEN, D_MODEL), 1.0),
        'c_ctx': nrm(ks[3], (D_MODEL,), 1.0),
        'w_mod': nrm(ks[4], (DEPTH, D_MODEL, N_MOD * D_MODEL), 0.5 * D_MODEL ** -0.5),
        'b_mod': nrm(ks[5], (DEPTH, N_MOD * D_MODEL), 0.02),
        'g_norm1': gain(ks[6], (DEPTH, D_MODEL)),
        'w_in': nrm(ks[7], (DEPTH, D_MODEL, IN_W), D_MODEL ** -0.5),
        'att_q_gain': gain(ks[8], (DEPTH, HEAD_DIM)),
        'att_k_gain': gain(ks[9], (DEPTH, HEAD_DIM)),
        'na_q_gain': gain(ks[10], (DEPTH, HEAD_DIM)),
        'na_k_gain': gain(ks[11], (DEPTH, HEAD_DIM)),
        'na_rel_bias': nrm(ks[12], (DEPTH, NA_HEADS, 2 * WIN_H - 1, 2 * WIN_W - 1), 0.02),
        'w_fourier': nrm(ks[13], (DEPTH, GROUP_W, GROUP_W), GROUP_W ** -0.5),
        'ssm_lam_re': -0.5 + 0.01 * jax.random.normal(ks[14], (DEPTH, 2, SSM_GROUPS, SSM_STATE), f32),
        'ssm_lam_im': lam_im0 + 0.01 * jax.random.normal(ks[15], (DEPTH, 2, SSM_GROUPS, SSM_STATE), f32),
        'ssm_log_dt': jax.random.uniform(ks[16], (DEPTH, 2, SSM_GROUPS), f32, minval=math.log(1e-3), maxval=math.log(1e-1)),
        'ssm_b_re': nrm(ks[17], (DEPTH, 2, SSM_GROUPS, SSM_STATE, SSM_GROUP), (2 * SSM_GROUP) ** -0.5),
        'ssm_b_im': nrm(ks[18], (DEPTH, 2, SSM_GROUPS, SSM_STATE, SSM_GROUP), (2 * SSM_GROUP) ** -0.5),
        'ssm_c_re': nrm(ks[19], (DEPTH, 2, SSM_GROUPS, SSM_GROUP, SSM_STATE), (2 * SSM_STATE) ** -0.5),
        'ssm_c_im': nrm(ks[20], (DEPTH, 2, SSM_GROUPS, SSM_GROUP, SSM_STATE), (2 * SSM_STATE) ** -0.5),
        'ssm_d': nrm(ks[21], (DEPTH, GROUP_W), 1.0),
        'w_glu': nrm(ks[22], (DEPTH, GROUP_W, GROUP_W), GROUP_W ** -0.5),
        'b_glu': nrm(ks[23], (DEPTH, GROUP_W), 0.02),
        'g_group': gain(ks[24], (DEPTH, D_MIX)),
        'w_out': nrm(ks[25], (DEPTH, D_MIX, D_MODEL), D_MIX ** -0.5),
        'g_norm2': gain(ks[26], (DEPTH, D_MODEL)),
        'w_ff1': nrm(ks[27], (DEPTH, D_MODEL, D_FF), D_MODEL ** -0.5),
        'w_ff3': nrm(ks[28], (DEPTH, D_MODEL, D_FF), D_MODEL ** -0.5),
        'w_ff2': nrm(ks[29], (DEPTH, D_FF, D_MODEL), D_FF ** -0.5),
    }


def reference(x, c, ctx, c_ctx, w_mod, b_mod, g_norm1, w_in, att_q_gain, att_k_gain, na_q_gain,
              na_k_gain, na_rel_bias, w_fourier, ssm_lam_re, ssm_lam_im, ssm_log_dt, ssm_b_re,
              ssm_b_im, ssm_c_re, ssm_c_im, ssm_d, w_glu, b_glu, g_group, w_out, g_norm2,
              w_ff1, w_ff3, w_ff2):
    rope = _rope_tables(x.shape[1])
    xc = ctx
    for i in range(DEPTH):
        x, xc = _layer(x, xc, c, c_ctx, rope, i < DEPTH - 1, w_mod[i], b_mod[i], g_norm1[i], w_in[i],
                       att_q_gain[i], att_k_gain[i], na_q_gain[i], na_k_gain[i], na_rel_bias[i],
                       w_fourier[i], ssm_lam_re[i], ssm_lam_im[i], ssm_log_dt[i], ssm_b_re[i],
                       ssm_b_im[i], ssm_c_re[i], ssm_c_im[i], ssm_d[i], w_glu[i], b_glu[i],
                       g_group[i], w_out[i], g_norm2[i], w_ff1[i], w_ff3[i], w_ff2[i])
    return x
```

```python
import functools
import math

import numpy as np
import jax
import jax.numpy as jnp
from jax import lax
from jax.experimental import pallas as pl
from jax.experimental.pallas import tpu as pltpu

F32 = jnp.float32
BF16 = jnp.bfloat16

D_MODEL = 1024
DEPTH = 2
GRID_W = 64
HEAD_DIM = 64
N_GROUPS = 4
GROUP_W = D_MODEL // N_GROUPS
WIN_H = 8
WIN_W = 16
NA_HEADS = GROUP_W // HEAD_DIM
FNO_HEAD_W = 64
SSM_GROUP = 16
SSM_GROUPS = GROUP_W // SSM_GROUP
SSM_STATE = 64
SSM_W = SSM_GROUPS * SSM_STATE
ROPE_THETA = 10000.0
D_FF = ((8 * D_MODEL // 3 + 255) // 256) * 256
N_MOD = 6
EPS = 1e-6
IN_W = 7 * GROUP_W
QKV_W = 6 * GROUP_W

LANES = 128
TILE = 256
BAND_ROWS = TILE // GRID_W
S5_STEPS = 128
FF_CHUNK = D_FF // 2
VMEM_LIMIT_BYTES = 56 * 2**20
NEG_BIG = -1e30

_Q_PERM = np.concatenate([np.arange(0, 64), np.arange(128, 192), np.arange(64, 128), np.arange(192, 256)])


def _params(sem):
    return pltpu.CompilerParams(dimension_semantics=sem, vmem_limit_bytes=VMEM_LIMIT_BYTES)


def _sds(shape, dtype):
    return jax.ShapeDtypeStruct(shape, dtype)


def _mod_kernel(c_ref, w_ref, b_ref, o_ref):
    c = c_ref[...]
    s = c * jax.nn.sigmoid(c)
    o_ref[0] = jnp.dot(s, w_ref[0], preferred_element_type=F32, precision=lax.Precision.HIGHEST) + b_ref[0]


def _modulation(cc, w_mod, b_mod):
    rows, d = cc.shape
    n_blk = 4
    bw = N_MOD * d // n_blk
    return pl.pallas_call(
        _mod_kernel,
        grid=(DEPTH, n_blk),
        in_specs=[
            pl.BlockSpec((rows, d), lambda l, j: (0, 0)),
            pl.BlockSpec((1, d, bw), lambda l, j: (l, 0, j)),
            pl.BlockSpec((1, 1, bw), lambda l, j: (l, 0, j)),
        ],
        out_specs=pl.BlockSpec((1, rows, bw), lambda l, j: (l, 0, j)),
        out_shape=_sds((DEPTH, rows, N_MOD * d), F32),
        compiler_params=_params(("arbitrary", "arbitrary")),
        name="modulation",
    )(cc, w_mod, b_mod)


def _rms_rows(x):
    return x * lax.rsqrt(jnp.mean(x * x, axis=-1, keepdims=True) + EPS)


def _mod_slices(mod_ref, first):
    m = mod_ref[0]
    return [m[:, (first + k) * D_MODEL:(first + k + 1) * D_MODEL] for k in range(3)]


def _head_norm(t, gain):
    lo = lax.broadcasted_iota(jnp.int32, t.shape, 1) < HEAD_DIM
    sq = t * t
    s_lo = jnp.sum(jnp.where(lo, sq, 0.0), axis=-1, keepdims=True)
    s_hi = jnp.sum(jnp.where(lo, 0.0, sq), axis=-1, keepdims=True)
    ms = jnp.where(lo, s_lo, s_hi) * (1.0 / HEAD_DIM)
    return t * lax.rsqrt(ms + EPS) * gain


def _rope(t, cos, sin):
    first = (lax.broadcasted_iota(jnp.int32, t.shape, 1) & 16) == 0
    partner = jnp.where(first, pltpu.roll(t, LANES - 16, axis=1), pltpu.roll(t, 16, axis=1))
    return t * cos + partner * sin


def _softmax_pv(s, v):
    p = jnp.exp(s - jnp.max(s, axis=-1, keepdims=True))
    l = jnp.sum(p, axis=-1, keepdims=True)
    return jnp.dot(p.astype(BF16), v, preferred_element_type=F32) / l


def _split_heads(q):
    lo = lax.broadcasted_iota(jnp.int32, q.shape, 1) < HEAD_DIM
    zero = jnp.zeros_like(q)
    return jnp.concatenate([jnp.where(lo, q, zero), jnp.where(lo, zero, q)], axis=0)


def _merge_heads(o, rows):
    lo = lax.broadcasted_iota(jnp.int32, (rows, LANES), 1) < HEAD_DIM
    return jnp.where(lo, o[:rows], o[rows:])


def _nt_dot(a, b):
    return lax.dot_general(a, b, (((1,), (1,)), ((), ())), preferred_element_type=F32)


def _inproj_kernel(*refs, split_input):
    if split_input:
        x_ref, ctx_ref, *refs = refs
    else:
        x_ref, *refs = refs
    (mod_ref, g1_ref, w_ref, cos_ref, sin_ref, gq_ref, gk_ref, nq_ref, nk_ref, qkv_ref, u_ref) = refs
    x = x_ref[0]
    if split_input:
        x = jnp.where(pl.program_id(1) == 0, ctx_ref[0], x)
    sh1, sc1, _ = _mod_slices(mod_ref, 0)
    h = (_rms_rows(x) * g1_ref[...]) * (1.0 + sc1) + sh1
    p = jnp.dot(h.astype(BF16), w_ref[...], preferred_element_type=F32)
    cos = cos_ref[...]
    sin = sin_ref[...]

    def tile(j):
        return p[:, j * LANES:(j + 1) * LANES]

    out = [
        _rope(_head_norm(tile(0), gq_ref[...]), cos, sin),
        _rope(_head_norm(tile(1), gq_ref[...]), cos, sin),
        _rope(_head_norm(tile(2), gk_ref[...]), cos, sin),
        tile(3),
        _head_norm(tile(4), nq_ref[...]),
        _head_norm(tile(5), nq_ref[...]),
        _head_norm(tile(6), nk_ref[...]),
        _head_norm(tile(7), nk_ref[...]),
        tile(8), tile(9), tile(10), tile(11),
    ]
    for j, t in enumerate(out):
        qkv_ref[0, :, j * LANES:(j + 1) * LANES] = t.astype(BF16)
    u_ref[0] = p[:, QKV_W:]


def _inproj(xs, mod, g1, w_in, cos_t, sin_t, gq, gk, nq, nk, *, n_batch, n_tok):
    split_input = len(xs) == 2
    n_tiles = n_tok // TILE
    tok = (1, TILE, D_MODEL)
    if split_input:
        x_specs = [pl.BlockSpec(tok, lambda b, i: (b, jnp.maximum(i - 1, 0), 0)),
                   pl.BlockSpec(tok, lambda b, i: (b, 0, 0))]
    else:
        x_specs = [pl.BlockSpec(tok, lambda b, i: (b, i, 0))]
    vec = lambda w: pl.BlockSpec((1, w), lambda b, i: (0, 0))
    return pl.pallas_call(
        functools.partial(_inproj_kernel, split_input=split_input),
        grid=(n_batch, n_tiles),
        in_specs=x_specs + [
            pl.BlockSpec((1, 1, N_MOD * D_MODEL), lambda b, i: (jnp.where(i == 0, n_batch, b), 0, 0)),
            vec(D_MODEL),
            pl.BlockSpec((D_MODEL, IN_W), lambda b, i: (0, 0)),
            pl.BlockSpec((TILE, LANES), lambda b, i: (i, 0)),
            pl.BlockSpec((TILE, LANES), lambda b, i: (i, 0)),
            vec(LANES), vec(LANES), vec(LANES), vec(LANES),
        ],
        out_specs=[pl.BlockSpec((1, TILE, QKV_W), lambda b, i: (b, i, 0)),
                   pl.BlockSpec((1, TILE, GROUP_W), lambda b, i: (b, i, 0))],
        out_shape=[_sds((n_batch, n_tok, QKV_W), BF16), _sds((n_batch, n_tok, GROUP_W), F32)],
        compiler_params=_params(("arbitrary", "arbitrary")),
        name="inproj",
    )(*xs, mod, g1, w_in, cos_t, sin_t, gq, gk, nq, nk)


def _gqa_kernel(qa_ref, qb_ref, k_ref, v_ref, o_ref, *, n_ctx, tile_off):
    def attend(n_keys):
        k = k_ref[0, :n_keys, :]
        v = v_ref[0, :n_keys, :]
        outs = []
        for q_ref in (qa_ref, qb_ref):
            q2 = _split_heads(q_ref[0])
            outs.append(_merge_heads(_softmax_pv(_nt_dot(q2, k), v), TILE))
        o_ref[0] = jnp.concatenate(outs, axis=-1)

    if tile_off == 0:
        is_ctx = pl.program_id(1) == 0
        pl.when(is_ctx)(lambda: attend(n_ctx))
        pl.when(jnp.logical_not(is_ctx))(lambda: attend(k_ref.shape[1]))
    else:
        attend(k_ref.shape[1])


def _gqa(qkv, *, need_ctx, n_ctx):
    n_batch, n_tok, _ = qkv.shape
    off = 0 if need_ctx else n_ctx // TILE
    q_spec = lambda col: pl.BlockSpec((1, TILE, LANES), lambda b, i: (b, i + off, col))
    kv_spec = lambda col: pl.BlockSpec((1, n_tok, LANES), lambda b, i: (b, 0, col))
    return pl.pallas_call(
        functools.partial(_gqa_kernel, n_ctx=n_ctx, tile_off=off),
        grid=(n_batch, n_tok // TILE - off),
        in_specs=[q_spec(0), q_spec(1), kv_spec(2), kv_spec(3)],
        out_specs=pl.BlockSpec((1, TILE, GROUP_W), lambda b, i: (b, i + off, 0)),
        out_shape=_sds((n_batch, n_tok, GROUP_W), F32),
        compiler_params=_params(("arbitrary", "arbitrary")),
        name="gqa",
    )(qkv, qkv, qkv, qkv)


def _na_kernel(q_ref, kp_ref, kc_ref, kn_ref, kx_ref, vp_ref, vc_ref, vn_ref, vx_ref, tbl_ref, o_ref, *, tile_off):
    def attend(local):
        outs = []
        for pair in range(2):
            cols = slice(pair * LANES, (pair + 1) * LANES)
            q2 = _split_heads(q_ref[0, :, cols])
            if local:
                k = jnp.concatenate([r[0, :, cols] for r in (kp_ref, kc_ref, kn_ref, kx_ref)], axis=0)
                v = jnp.concatenate([r[0, :, cols] for r in (vp_ref, vc_ref, vn_ref, vx_ref)], axis=0)
                bias = jnp.concatenate([tbl_ref[0, 2 * pair], tbl_ref[0, 2 * pair + 1]], axis=0)
                s = _nt_dot(q2, k)
                s = jnp.concatenate([s[:, :3 * TILE] + bias, s[:, 3 * TILE:]], axis=-1)
            else:
                k = kx_ref[0, :, cols]
                v = vx_ref[0, :, cols]
                s = _nt_dot(q2, k)
            outs.append(_merge_heads(_softmax_pv(s, v), TILE))
        o_ref[0] = jnp.concatenate(outs, axis=-1)

    if tile_off == 0:
        is_ctx = pl.program_id(1) == 0
        pl.when(is_ctx)(lambda: attend(False))
        pl.when(jnp.logical_not(is_ctx))(lambda: attend(True))
    else:
        attend(True)


def _na(qkv, table, *, need_ctx, n_ctx):
    n_batch, n_tok, _ = qkv.shape
    assert n_ctx == TILE
    n_tiles = n_tok // TILE
    off = 0 if need_ctx else 1
    blk = (1, TILE, GROUP_W)
    cur = lambda col: pl.BlockSpec(blk, lambda b, i: (b, i + off, col))
    prev = lambda col: pl.BlockSpec(blk, lambda b, i: (b, jnp.maximum(i + off - 1, 1), col))
    nxt = lambda col: pl.BlockSpec(blk, lambda b, i: (b, jnp.minimum(i + off + 1, n_tiles - 1), col))
    ctx = lambda col: pl.BlockSpec(blk, lambda b, i: (b, 0, col))

    def cls(b, i):
        t = i + off
        return (jnp.where(t == n_tiles - 1, 2, jnp.where(t <= 1, 0, 1)), 0, 0, 0)

    return pl.pallas_call(
        functools.partial(_na_kernel, tile_off=off),
        grid=(n_batch, n_tiles - off),
        in_specs=[cur(2), prev(3), cur(3), nxt(3), ctx(3), prev(4), cur(4), nxt(4), ctx(4),
                  pl.BlockSpec((1, NA_HEADS, TILE, 3 * TILE), cls)],
        out_specs=pl.BlockSpec(blk, lambda b, i: (b, i + off, 0)),
        out_shape=_sds((n_batch, n_tok, GROUP_W), F32),
        compiler_params=_params(("arbitrary", "arbitrary")),
        name="na",
    )(*([qkv] * 9), table)


def _na_bias_table(rel_bias, rows):
    n_bands = rows // BAND_ROWS
    assert rows >= WIN_H and rows % BAND_ROWS == 0
    q = np.arange(TILE)
    j = np.arange(3 * TILE)
    drs, dcs, valids = [], [], []
    for band in (0, min(1, n_bands - 1), n_bands - 1):
        r = BAND_ROWS * band + q // GRID_W
        qc = q % GRID_W
        band_k = band - 1 + j // TILE
        kr = BAND_ROWS * band_k + (j % TILE) // GRID_W
        kc = j % GRID_W
        rs = np.clip(r - WIN_H // 2, 0, rows - WIN_H)
        cs = np.clip(qc - WIN_W // 2, 0, GRID_W - WIN_W)
        valid = ((band_k >= 0) & (band_k < n_bands))[None, :]
        valid = valid & (kr[None, :] >= rs[:, None]) & (kr[None, :] < rs[:, None] + WIN_H)
        valid = valid & (kc[None, :] >= cs[:, None]) & (kc[None, :] < cs[:, None] + WIN_W)
        drs.append(np.clip(kr[None, :] - r[:, None] + WIN_H - 1, 0, 2 * WIN_H - 2))
        dcs.append(np.clip(kc[None, :] - qc[:, None] + WIN_W - 1, 0, 2 * WIN_W - 2))
        valids.append(valid)
    dr, dc, valid = np.stack(drs), np.stack(dcs), np.stack(valids)
    bias = rel_bias.astype(F32)[:, dr, dc]
    return jnp.transpose(jnp.where(valid[None], bias, NEG_BIG), (1, 0, 2, 3))


def _fourier_kernel(x_ref, cs_ref, dft_ref, dftc_ref, w_ref, o_ref, x12_ref, *, n_ctx, n_lat, tile_off):
    t = pl.program_id(1) + tile_off

    def head_dft(x):
        x12 = jnp.dot(x, cs_ref[...], preferred_element_type=F32).astype(BF16)
        return jnp.concatenate([x12[:, :GROUP_W], x12[:, GROUP_W:]], axis=0)

    def finish(y, n):
        y = y * (1.0 / math.sqrt(n * FNO_HEAD_W))
        o_ref[0] = jnp.dot(y.astype(BF16), w_ref[...], preferred_element_type=F32)

    if tile_off == 0:
        @pl.when(t == 0)
        def _():
            x12 = head_dft(x_ref[0, :n_ctx, :])
            finish(jnp.dot(dftc_ref[...], x12, preferred_element_type=F32), n_ctx)

    @pl.when(t == n_ctx // TILE)
    def _():
        x12_ref[...] = head_dft(x_ref[0, n_ctx:, :])

    @pl.when(t >= n_ctx // TILE)
    def _():
        row0 = pl.multiple_of((t - n_ctx // TILE) * TILE, TILE)
        finish(jnp.dot(dft_ref[pl.ds(row0, TILE), :], x12_ref[...], preferred_element_type=F32), n_lat)


def _fourier(qkv, cs_bd, dft, dft_ctx, w_f, *, need_ctx, n_ctx):
    n_batch, n_tok, _ = qkv.shape
    n_lat = n_tok - n_ctx
    off = 0 if need_ctx else n_ctx // TILE
    const = lambda shape: pl.BlockSpec(shape, lambda b, i: (0,) * len(shape))
    return pl.pallas_call(
        functools.partial(_fourier_kernel, n_ctx=n_ctx, n_lat=n_lat, tile_off=off),
        grid=(n_batch, n_tok // TILE - off),
        in_specs=[pl.BlockSpec((1, n_tok, GROUP_W), lambda b, i: (b, 0, 5)),
                  const(cs_bd.shape), const(dft.shape), const(dft_ctx.shape), const(w_f.shape)],
        out_specs=pl.BlockSpec((1, TILE, GROUP_W), lambda b, i: (b, i + off, 0)),
        out_shape=_sds((n_batch, n_tok, GROUP_W), F32),
        scratch_shapes=[pltpu.VMEM((2 * n_lat, GROUP_W), BF16)],
        compiler_params=_params(("arbitrary", "arbitrary")),
        name="fourier",
    )(qkv, cs_bd, dft, dft_ctx, w_f)


def _dft_matrix(n):
    k = lax.broadcasted_iota(jnp.int32, (n, n), 0)
    m = lax.broadcasted_iota(jnp.int32, (n, n), 1)
    ang = ((k * m) % n).astype(F32) * (2.0 * math.pi / n)
    return jnp.concatenate([jnp.cos(ang), -jnp.sin(ang)], axis=1).astype(BF16)


def _head_dft_matrix():
    a = np.arange(GROUP_W)
    same = (a[:, None] // FNO_HEAD_W) == (a[None, :] // FNO_HEAD_W)
    ang = 2.0 * np.pi * (((a[:, None] % FNO_HEAD_W) * (a[None, :] % FNO_HEAD_W)) % FNO_HEAD_W) / FNO_HEAD_W
    m = np.concatenate([np.where(same, np.cos(ang), 0.0), np.where(same, np.sin(ang), 0.0)], axis=1)
    return jnp.asarray(m, F32).astype(BF16)


def _s5_kernel(u_ref, a_ref, bd_ref, cd_ref, y_ref, bu_ref, h_ref, *, n_batch):
    rev = pl.program_id(0) == 1
    rows = S5_STEPS * n_batch

    @pl.when(pl.program_id(1) == 0)
    def _():
        h_ref[...] = jnp.zeros_like(h_ref)

    u = pltpu.einshape("btc->tbc", u_ref[...]).reshape(rows, GROUP_W)
    bu_ref[...] = jnp.dot(u.astype(BF16), bd_ref[0], preferred_element_type=F32)

    chunk = 2 * LANES
    for sc in range(SSM_W // chunk):
        re = pl.ds(sc * chunk, chunk)
        im = pl.ds(SSM_W + sc * chunk, chunk)
        a_re = jnp.broadcast_to(a_ref[0, :, re], (n_batch, chunk))
        a_im = jnp.broadcast_to(a_ref[0, :, im], (n_batch, chunk))

        def step(s, carry, re=re, im=im, a_re=a_re, a_im=a_im):
            h_re, h_im = carry
            s = jnp.where(rev, S5_STEPS - 1 - s, s)
            r = pl.ds(pl.multiple_of(s * n_batch, n_batch), n_batch)
            n_re = a_re * h_re - a_im * h_im + bu_ref[r, re]
            n_im = a_re * h_im + a_im * h_re + bu_ref[r, im]
            bu_ref[r, re] = n_re
            bu_ref[r, im] = n_im
            return n_re, n_im

        h_re, h_im = lax.fori_loop(0, S5_STEPS, step, (h_ref[:, re], h_ref[:, im]), unroll=4)
        h_ref[:, re] = h_re
        h_ref[:, im] = h_im

    y = jnp.dot(bu_ref[...].astype(BF16), cd_ref[0], preferred_element_type=F32)
    y_ref[0] = pltpu.einshape("tbc->btc", y.reshape(S5_STEPS, n_batch, GROUP_W))


def _s5(u, a_tab, b_dense, c_dense, *, n_ctx):
    n_batch, n_tok, _ = u.shape
    n_blk = n_tok // S5_STEPS
    n_cblk = n_ctx // S5_STEPS

    def blk(d, s):
        back = jnp.where(s < n_cblk, n_cblk - 1 - s, n_blk - 1 - (s - n_cblk))
        return jnp.where(d == 0, s, back)

    return pl.pallas_call(
        functools.partial(_s5_kernel, n_batch=n_batch),
        grid=(2, n_blk),
        in_specs=[pl.BlockSpec((n_batch, S5_STEPS, GROUP_W), lambda d, s: (0, blk(d, s), 0)),
                  pl.BlockSpec((1, 1, 2 * SSM_W), lambda d, s: (d, 0, 0)),
                  pl.BlockSpec((1, GROUP_W, 2 * SSM_W), lambda d, s: (d, 0, 0)),
                  pl.BlockSpec((1, 2 * SSM_W, GROUP_W), lambda d, s: (d, 0, 0))],
        out_specs=pl.BlockSpec((1, n_batch, S5_STEPS, GROUP_W), lambda d, s: (d, 0, blk(d, s), 0)),
        out_shape=_sds((2, n_batch, n_tok, GROUP_W), F32),
        scratch_shapes=[pltpu.VMEM((S5_STEPS * n_batch, 2 * SSM_W), F32),
                        pltpu.VMEM((n_batch, 2 * SSM_W), F32)],
        compiler_params=_params(("arbitrary", "arbitrary")),
        name="s5",
    )(u, a_tab, b_dense, c_dense)


def _s5_tables(lam_re, lam_im, log_dt, b_re, b_im, c_re, c_im):
    lam = lax.complex(lam_re.astype(F32), lam_im.astype(F32))
    dt = jnp.exp(log_dt.astype(F32))[..., None]
    a_bar = jnp.exp(lam * dt)
    b_bar = ((a_bar - 1.0) / lam)[..., None] * lax.complex(b_re.astype(F32), b_im.astype(F32))
    eye = jnp.eye(SSM_GROUPS, dtype=F32)

    def embed_b(m):
        m = jnp.transpose(m, (0, 1, 3, 2))[:, :, :, None, :] * eye[None, :, None, :, None]
        return m.reshape(2, GROUP_W, SSM_W)

    def embed_c(m):
        m = jnp.transpose(m, (0, 1, 3, 2))[:, :, :, None, :] * eye[None, :, None, :, None]
        return m.reshape(2, SSM_W, GROUP_W)

    a_tab = jnp.concatenate([jnp.real(a_bar).reshape(2, 1, SSM_W), jnp.imag(a_bar).reshape(2, 1, SSM_W)], axis=-1)
    b_dense = jnp.concatenate([embed_b(jnp.real(b_bar)), embed_b(jnp.imag(b_bar))], axis=-1).astype(BF16)
    c_dense = jnp.concatenate([embed_c(c_re.astype(F32)), -embed_c(c_im.astype(F32))], axis=1).astype(BF16)
    return a_tab, b_dense, c_dense


def _gelu_tanh(x):
    return x * (0.5 * (1.0 + jnp.tanh(math.sqrt(2.0 / math.pi) * (x + 0.044715 * (x * x * x)))))


def _mixout_kernel(*refs, split_input):
    if split_input:
        x_ref, ctx_ref, *refs = refs
    else:
        x_ref, *refs = refs
    (mod_ref, oa_ref, od_ref, of_ref, u_ref, ys_ref, d_ref, wg_ref, bg_ref, gg_ref, wo_ref, o_ref) = refs
    x = x_ref[0]
    if split_input:
        x = jnp.where(pl.program_id(1) == 0, ctx_ref[0], x)
    _, _, ga1 = _mod_slices(mod_ref, 0)
    y = u_ref[0] * d_ref[...] + ys_ref[0, 0] + ys_ref[1, 0]
    g = _gelu_tanh(y)
    z = jnp.dot(g.astype(BF16), wg_ref[...], preferred_element_type=F32) + bg_ref[...]
    o_s = g * jax.nn.sigmoid(z)
    parts = [oa_ref[0], od_ref[0], of_ref[0], o_s]
    normed = [(_rms_rows(p) * gg_ref[:, j * GROUP_W:(j + 1) * GROUP_W]).astype(BF16) for j, p in enumerate(parts)]
    r = jnp.dot(jnp.concatenate(normed, axis=-1), wo_ref[...], preferred_element_type=F32)
    o_ref[0] = x + ga1 * r


def _mixout(xs, mod, o_a, o_d, o_f, u, ys, d_skip, w_glu, b_glu, g_group, w_out, *, need_ctx, n_ctx):
    split_input = len(xs) == 2
    n_batch, n_tok, _ = u.shape
    off = 0 if need_ctx else n_ctx // TILE
    n_out = n_tok - off * TILE
    tok = (1, TILE, D_MODEL)
    if split_input:
        assert need_ctx
        x_specs = [pl.BlockSpec(tok, lambda b, i: (b, jnp.maximum(i - 1, 0), 0)),
                   pl.BlockSpec(tok, lambda b, i: (b, 0, 0))]
    else:
        x_specs = [pl.BlockSpec(tok, lambda b, i: (b, i + off, 0))]
    grp = pl.BlockSpec((1, TILE, GROUP_W), lambda b, i: (b, i + off, 0))
    const = lambda shape: pl.BlockSpec(shape, lambda b, i: (0,) * len(shape))
    return pl.pallas_call(
        functools.partial(_mixout_kernel, split_input=split_input),
        grid=(n_batch, n_out // TILE),
        in_specs=x_specs + [
            pl.BlockSpec((1, 1, N_MOD * D_MODEL), lambda b, i: (jnp.where(i + off == 0, n_batch, b), 0, 0)),
            grp, grp, grp, grp,
            pl.BlockSpec((2, 1, TILE, GROUP_W), lambda b, i: (0, b, i + off, 0)),
            const((1, GROUP_W)), const((GROUP_W, GROUP_W)), const((1, GROUP_W)),
            const((1, D_MODEL)), const((D_MODEL, D_MODEL)),
        ],
        out_specs=pl.BlockSpec(tok, lambda b, i: (b, i, 0)),
        out_shape=_sds((n_batch, n_out, D_MODEL), F32),
        compiler_params=_params(("arbitrary", "arbitrary")),
        name="mixout",
    )(*xs, mod, o_a, o_d, o_f, u, ys, d_skip, w_glu, b_glu, g_group, w_out)


def _ffn_kernel(x_ref, mod_ref, g2_ref, w1_ref, w3_ref, w2_ref, o_ref):
    x = x_ref[0]
    sh2, sc2, ga2 = _mod_slices(mod_ref, 3)
    h = ((_rms_rows(x) * g2_ref[...]) * (1.0 + sc2) + sh2).astype(BF16)
    acc = jnp.zeros((TILE, D_MODEL), F32)
    for c in range(D_FF // FF_CHUNK):
        cols = slice(c * FF_CHUNK, (c + 1) * FF_CHUNK)
        a = jnp.dot(h, w1_ref[:, cols], preferred_element_type=F32)
        b = jnp.dot(h, w3_ref[:, cols], preferred_element_type=F32)
        t = (a * jax.nn.sigmoid(a)) * b
        acc = acc + jnp.dot(t.astype(BF16), w2_ref[cols, :], preferred_element_type=F32)
    o_ref[0] = x + ga2 * acc


def _ffn(x, mod, g2, w1, w3, w2, *, has_ctx):
    n_batch, n_tok, _ = x.shape
    tok = pl.BlockSpec((1, TILE, D_MODEL), lambda b, i: (b, i, 0))
    const = lambda shape: pl.BlockSpec(shape, lambda b, i: (0,) * len(shape))
    if has_ctx:
        mod_map = lambda b, i: (jnp.where(i == 0, n_batch, b), 0, 0)
    else:
        mod_map = lambda b, i: (b, 0, 0)
    return pl.pallas_call(
        _ffn_kernel,
        grid=(n_batch, n_tok // TILE),
        in_specs=[tok, pl.BlockSpec((1, 1, N_MOD * D_MODEL), mod_map), const((1, D_MODEL)),
                  const(w1.shape), const(w3.shape), const(w2.shape)],
        out_specs=tok,
        out_shape=_sds(x.shape, F32),
        compiler_params=_params(("arbitrary", "arbitrary")),
        name="ffn",
    )(x, mod, g2, w1, w3, w2)


def _rope_tables(n_lat, n_ctx):
    t = jnp.arange(n_lat, dtype=jnp.int32)
    rows = (t // GRID_W).astype(F32)
    cols = (t % GRID_W).astype(F32)
    axis_dim = HEAD_DIM // 2
    inv_freq = ROPE_THETA ** (-jnp.arange(0, axis_dim, 2, dtype=F32) / axis_dim)
    ang_r = rows[:, None] * inv_freq[None, :]
    ang_c = cols[:, None] * inv_freq[None, :]
    cos = jnp.concatenate([jnp.cos(ang_r)] * 2 + [jnp.cos(ang_c)] * 2, axis=-1)
    sin = jnp.concatenate([-jnp.sin(ang_r), jnp.sin(ang_r), -jnp.sin(ang_c), jnp.sin(ang_c)], axis=-1)
    cos = jnp.concatenate([jnp.ones((n_ctx, HEAD_DIM), F32), cos], axis=0)
    sin = jnp.concatenate([jnp.zeros((n_ctx, HEAD_DIM), F32), sin], axis=0)
    return jnp.tile(cos, (1, 2)), jnp.tile(sin, (1, 2))


def kernel(x, c, ctx, c_ctx, w_mod, b_mod, g_norm1, w_in, att_q_gain, att_k_gain, na_q_gain, na_k_gain, na_rel_bias, w_fourier, ssm_lam_re, ssm_lam_im, ssm_log_dt, ssm_b_re, ssm_b_im, ssm_c_re, ssm_c_im, ssm_d, w_glu, b_glu, g_group, w_out, g_norm2, w_ff1, w_ff3, w_ff2):
    n_batch, n_lat, d = x.shape
    n_ctx = ctx.shape[1]
    n_tok = n_lat + n_ctx
    assert d == D_MODEL and n_ctx == TILE and n_lat % TILE == 0 and n_lat % GRID_W == 0
    assert n_batch % 8 == 0 and n_tok % S5_STEPS == 0

    mod_rows = -(-(n_batch + 1) // 8) * 8
    cc = jnp.concatenate([c, c_ctx[None, :], jnp.zeros((mod_rows - n_batch - 1, d), F32)], axis=0)
    mod = _modulation(cc, w_mod, b_mod.reshape(DEPTH, 1, N_MOD * d)).reshape(DEPTH, mod_rows, 1, N_MOD * d)

    cos_t, sin_t = _rope_tables(n_lat, n_ctx)
    cs_bd = _head_dft_matrix()
    dft = _dft_matrix(n_lat)
    dft_ctx = _dft_matrix(n_ctx)
    in_cols = np.concatenate([_Q_PERM, np.arange(GROUP_W, IN_W)])
    out_rows = np.concatenate([_Q_PERM, np.arange(GROUP_W, D_MODEL)])
    scale = HEAD_DIM ** -0.5
    tile2 = lambda g, s=1.0: jnp.tile(g.astype(F32) * s, 2)[None, :]
    row = lambda v: v.astype(F32)[None, :]

    xs = (x, ctx)
    for l in range(DEPTH):
        need_ctx = l < DEPTH - 1
        qkv, u = _inproj(xs, mod[l], row(g_norm1[l]), w_in[l][:, in_cols].astype(BF16), cos_t, sin_t,
                         tile2(att_q_gain[l], scale), tile2(att_k_gain[l]),
                         tile2(na_q_gain[l], scale), tile2(na_k_gain[l]),
                         n_batch=n_batch, n_tok=n_tok)
        o_a = _gqa(qkv, need_ctx=need_ctx, n_ctx=n_ctx)
        o_d = _na(qkv, _na_bias_table(na_rel_bias[l], n_lat // GRID_W), need_ctx=need_ctx, n_ctx=n_ctx)
        o_f = _fourier(qkv, cs_bd, dft, dft_ctx, w_fourier[l].astype(BF16), need_ctx=need_ctx, n_ctx=n_ctx)
        a_tab, b_dense, c_dense = _s5_tables(ssm_lam_re[l], ssm_lam_im[l], ssm_log_dt[l], ssm_b_re[l],
                                             ssm_b_im[l], ssm_c_re[l], ssm_c_im[l])
        ys = _s5(u, a_tab, b_dense, c_dense, n_ctx=n_ctx)
        x1 = _mixout(xs, mod[l], o_a, o_d, o_f, u, ys, row(ssm_d[l]), w_glu[l].astype(BF16), row(b_glu[l]),
                     row(g_group[l][out_rows]), w_out[l][out_rows, :].astype(BF16),
                     need_ctx=need_ctx, n_ctx=n_ctx)
        x2 = _ffn(x1, mod[l], row(g_norm2[l]), w_ff1[l].astype(BF16), w_ff3[l].astype(BF16),
                  w_ff2[l].astype(BF16), has_ctx=need_ctx)
        xs = (x2,)
    return xs[0]
```

```python
import functools
import math

import numpy as np
import jax
import jax.numpy as jnp
from jax import lax
from jax.experimental import pallas as pl
from jax.experimental.pallas import tpu as pltpu

F32 = jnp.float32
BF16 = jnp.bfloat16

D_MODEL = 1024
DEPTH = 2
GRID_W = 64
HEAD_DIM = 64
N_GROUPS = 4
GROUP_W = D_MODEL // N_GROUPS
WIN_H = 8
WIN_W = 16
NA_HEADS = GROUP_W // HEAD_DIM
FNO_HEAD_W = 64
SSM_GROUP = 16
SSM_GROUPS = GROUP_W // SSM_GROUP
SSM_STATE = 64
SSM_W = SSM_GROUPS * SSM_STATE
ROPE_THETA = 10000.0
D_FF = ((8 * D_MODEL // 3 + 255) // 256) * 256
N_MOD = 6
EPS = 1e-6
IN_W = 7 * GROUP_W
QKV_W = 6 * GROUP_W

LANES = 128
TILE = 256
BAND_ROWS = TILE // GRID_W
S5_STEPS = 128
FF_CHUNK = D_FF // 2
VMEM_LIMIT_BYTES = 56 * 2**20
NEG_BIG = -1e30

_Q_PERM = np.concatenate([np.arange(0, 64), np.arange(128, 192), np.arange(64, 128), np.arange(192, 256)])


def _params(sem):
    return pltpu.CompilerParams(dimension_semantics=sem, vmem_limit_bytes=VMEM_LIMIT_BYTES)


def _sds(shape, dtype):
    return jax.ShapeDtypeStruct(shape, dtype)


def _mod_kernel(c_ref, w_ref, b_ref, o_ref):
    c = c_ref[...]
    s = c * jax.nn.sigmoid(c)
    o_ref[0] = jnp.dot(s, w_ref[0], preferred_element_type=F32, precision=lax.Precision.HIGHEST) + b_ref[0]


def _modulation(cc, w_mod, b_mod):
    rows, d = cc.shape
    n_blk = 4
    bw = N_MOD * d // n_blk
    return pl.pallas_call(
        _mod_kernel,
        grid=(DEPTH, n_blk),
        in_specs=[
            pl.BlockSpec((rows, d), lambda l, j: (0, 0)),
            pl.BlockSpec((1, d, bw), lambda l, j: (l, 0, j)),
            pl.BlockSpec((1, 1, bw), lambda l, j: (l, 0, j)),
        ],
        out_specs=pl.BlockSpec((1, rows, bw), lambda l, j: (l, 0, j)),
        out_shape=_sds((DEPTH, rows, N_MOD * d), F32),
        compiler_params=_params(("arbitrary", "arbitrary")),
        name="modulation",
    )(cc, w_mod, b_mod)


def _rms_rows(x):
    return x * lax.rsqrt(jnp.mean(x * x, axis=-1, keepdims=True) + EPS)


def _mod_slices(mod_ref, first):
    m = mod_ref[0]
    return [m[:, (first + k) * D_MODEL:(first + k + 1) * D_MODEL] for k in range(3)]


def _head_norm(t, gain):
    lo = lax.broadcasted_iota(jnp.int32, t.shape, 1) < HEAD_DIM
    sq = t * t
    s_lo = jnp.sum(jnp.where(lo, sq, 0.0), axis=-1, keepdims=True)
    s_hi = jnp.sum(jnp.where(lo, 0.0, sq), axis=-1, keepdims=True)
    ms = jnp.where(lo, s_lo, s_hi) * (1.0 / HEAD_DIM)
    return t * lax.rsqrt(ms + EPS) * gain


def _rope(t, cos, sin):
    first = (lax.broadcasted_iota(jnp.int32, t.shape, 1) & 16) == 0
    partner = jnp.where(first, pltpu.roll(t, LANES - 16, axis=1), pltpu.roll(t, 16, axis=1))
    return t * cos + partner * sin


def _softmax_pv(s, v):
    p = jnp.exp(s - jnp.max(s, axis=-1, keepdims=True))
    l = jnp.sum(p, axis=-1, keepdims=True)
    return jnp.dot(p.astype(BF16), v, preferred_element_type=F32) / l


def _split_heads(q):
    lo = lax.broadcasted_iota(jnp.int32, q.shape, 1) < HEAD_DIM
    zero = jnp.zeros_like(q)
    return jnp.concatenate([jnp.where(lo, q, zero), jnp.where(lo, zero, q)], axis=0)


def _merge_heads(o, rows):
    lo = lax.broadcasted_iota(jnp.int32, (rows, LANES), 1) < HEAD_DIM
    return jnp.where(lo, o[:rows], o[rows:])


def _nt_dot(a, b):
    return lax.dot_general(a, b, (((1,), (1,)), ((), ())), preferred_element_type=F32)


def _inproj_kernel(*refs, split_input):
    if split_input:
        x_ref, ctx_ref, *refs = refs
    else:
        x_ref, *refs = refs
    (mod_ref, g1_ref, w_ref, cos_ref, sin_ref, gq_ref, gk_ref, nq_ref, nk_ref, qkv_ref, u_ref) = refs
    x = x_ref[0]
    if split_input:
        x = jnp.where(pl.program_id(1) == 0, ctx_ref[0], x)
    sh1, sc1, _ = _mod_slices(mod_ref, 0)
    h = (_rms_rows(x) * g1_ref[...]) * (1.0 + sc1) + sh1
    p = jnp.dot(h.astype(BF16), w_ref[...], preferred_element_type=F32)
    cos = cos_ref[...]
    sin = sin_ref[...]

    def tile(j):
        return p[:, j * LANES:(j + 1) * LANES]

    out = [
        _rope(_head_norm(tile(0), gq_ref[...]), cos, sin),
        _rope(_head_norm(tile(1), gq_ref[...]), cos, sin),
        _rope(_head_norm(tile(2), gk_ref[...]), cos, sin),
        tile(3),
        _head_norm(tile(4), nq_ref[...]),
        _head_norm(tile(5), nq_ref[...]),
        _head_norm(tile(6), nk_ref[...]),
        _head_norm(tile(7), nk_ref[...]),
        tile(8), tile(9), tile(10), tile(11),
    ]
    for j, t in enumerate(out):
        qkv_ref[0, :, j * LANES:(j + 1) * LANES] = t.astype(BF16)
    u_ref[0] = p[:, QKV_W:]


def _inproj(xs, mod, g1, w_in, cos_t, sin_t, gq, gk, nq, nk, *, n_batch, n_tok):
    split_input = len(xs) == 2
    n_tiles = n_tok // TILE
    tok = (1, TILE, D_MODEL)
    if split_input:
        x_specs = [pl.BlockSpec(tok, lambda b, i: (b, jnp.maximum(i - 1, 0), 0)),
                   pl.BlockSpec(tok, lambda b, i: (b, 0, 0))]
    else:
        x_specs = [pl.BlockSpec(tok, lambda b, i: (b, i, 0))]
    vec = lambda w: pl.BlockSpec((1, w), lambda b, i: (0, 0))
    return pl.pallas_call(
        functools.partial(_inproj_kernel, split_input=split_input),
        grid=(n_batch, n_tiles),
        in_specs=x_specs + [
            pl.BlockSpec((1, 1, N_MOD * D_MODEL), lambda b, i: (jnp.where(i == 0, n_batch, b), 0, 0)),
            vec(D_MODEL),
            pl.BlockSpec((D_MODEL, IN_W), lambda b, i: (0, 0)),
            pl.BlockSpec((TILE, LANES), lambda b, i: (i, 0)),
            pl.BlockSpec((TILE, LANES), lambda b, i: (i, 0)),
            vec(LANES), vec(LANES), vec(LANES), vec(LANES),
        ],
        out_specs=[pl.BlockSpec((1, TILE, QKV_W), lambda b, i: (b, i, 0)),
                   pl.BlockSpec((1, TILE, GROUP_W), lambda b, i: (b, i, 0))],
        out_shape=[_sds((n_batch, n_tok, QKV_W), BF16), _sds((n_batch, n_tok, GROUP_W), F32)],
        compiler_params=_params(("arbitrary", "arbitrary")),
        name="inproj",
    )(*xs, mod, g1, w_in, cos_t, sin_t, gq, gk, nq, nk)


def _gqa_kernel(qa_ref, qb_ref, k_ref, v_ref, o_ref, *, n_ctx, tile_off):
    def attend(n_keys):
        k = k_ref[0, :n_keys, :]
        v = v_ref[0, :n_keys, :]
        outs = []
        for q_ref in (qa_ref, qb_ref):
            q2 = _split_heads(q_ref[0])
            outs.append(_merge_heads(_softmax_pv(_nt_dot(q2, k), v), TILE))
        o_ref[0] = jnp.concatenate(outs, axis=-1)

    if tile_off == 0:
        is_ctx = pl.program_id(1) == 0
        pl.when(is_ctx)(lambda: attend(n_ctx))
        pl.when(jnp.logical_not(is_ctx))(lambda: attend(k_ref.shape[1]))
    else:
        attend(k_ref.shape[1])


def _gqa(qkv, *, need_ctx, n_ctx):
    n_batch, n_tok, _ = qkv.shape
    off = 0 if need_ctx else n_ctx // TILE
    q_spec = lambda col: pl.BlockSpec((1, TILE, LANES), lambda b, i: (b, i + off, col))
    kv_spec = lambda col: pl.BlockSpec((1, n_tok, LANES), lambda b, i: (b, 0, col))
    return pl.pallas_call(
        functools.partial(_gqa_kernel, n_ctx=n_ctx, tile_off=off),
        grid=(n_batch, n_tok // TILE - off),
        in_specs=[q_spec(0), q_spec(1), kv_spec(2), kv_spec(3)],
        out_specs=pl.BlockSpec((1, TILE, GROUP_W), lambda b, i: (b, i + off, 0)),
        out_shape=_sds((n_batch, n_tok, GROUP_W), F32),
        compiler_params=_params(("arbitrary", "arbitrary")),
        name="gqa",
    )(qkv, qkv, qkv, qkv)


def _na_kernel(q_ref, kp_ref, kc_ref, kn_ref, kx_ref, vp_ref, vc_ref, vn_ref, vx_ref, tbl_ref, o_ref, *, tile_off):
    def attend(local):
        outs = []
        for pair in range(2):
            cols = slice(pair * LANES, (pair + 1) * LANES)
            q2 = _split_heads(q_ref[0, :, cols])
            if local:
                k = jnp.concatenate([r[0, :, cols] for r in (kp_ref, kc_ref, kn_ref, kx_ref)], axis=0)
                v = jnp.concatenate([r[0, :, cols] for r in (vp_ref, vc_ref, vn_ref, vx_ref)], axis=0)
                bias = jnp.concatenate([tbl_ref[0, 2 * pair], tbl_ref[0, 2 * pair + 1]], axis=0)
                s = _nt_dot(q2, k)
                s = jnp.concatenate([s[:, :3 * TILE] + bias, s[:, 3 * TILE:]], axis=-1)
            else:
                k = kx_ref[0, :, cols]
                v = vx_ref[0, :, cols]
                s = _nt_dot(q2, k)
            outs.append(_merge_heads(_softmax_pv(s, v), TILE))
        o_ref[0] = jnp.concatenate(outs, axis=-1)

    if tile_off == 0:
        is_ctx = pl.program_id(1) == 0
        pl.when(is_ctx)(lambda: attend(False))
        pl.when(jnp.logical_not(is_ctx))(lambda: attend(True))
    else:
        attend(True)


def _na(qkv, table, *, need_ctx, n_ctx):
    n_batch, n_tok, _ = qkv.shape
    assert n_ctx == TILE
    n_tiles = n_tok // TILE
    off = 0 if need_ctx else 1
    blk = (1, TILE, GROUP_W)
    cur = lambda col: pl.BlockSpec(blk, lambda b, i: (b, i + off, col))
    prev = lambda col: pl.BlockSpec(blk, lambda b, i: (b, jnp.maximum(i + off - 1, 1), col))
    nxt = lambda col: pl.BlockSpec(blk, lambda b, i: (b, jnp.minimum(i + off + 1, n_tiles - 1), col))
    ctx = lambda col: pl.BlockSpec(blk, lambda b, i: (b, 0, col))

    def cls(b, i):
        t = i + off
        return (jnp.where(t == n_tiles - 1, 2, jnp.where(t <= 1, 0, 1)), 0, 0, 0)

    return pl.pallas_call(
        functools.partial(_na_kernel, tile_off=off),
        grid=(n_batch, n_tiles - off),
        in_specs=[cur(2), prev(3), cur(3), nxt(3), ctx(3), prev(4), cur(4), nxt(4), ctx(4),
                  pl.BlockSpec((1, NA_HEADS, TILE, 3 * TILE), cls)],
        out_specs=pl.BlockSpec(blk, lambda b, i: (b, i + off, 0)),
        out_shape=_sds((n_batch, n_tok, GROUP_W), F32),
        compiler_params=_params(("arbitrary", "arbitrary")),
        name="na",
    )(*([qkv] * 9), table)


def _na_bias_table(rel_bias, rows):
    n_bands = rows // BAND_ROWS
    assert rows >= WIN_H and rows % BAND_ROWS == 0
    n_dr, n_dc = 2 * WIN_H - 1, 2 * WIN_W - 1
    w = GRID_W
    shift = w - WIN_W
    vp = jnp.pad(rel_bias.astype(F32), ((0, 0), (0, 0), (shift, 2 * w - shift - n_dc)))
    skew = jnp.broadcast_to(vp[:, :, None, :], (NA_HEADS, n_dr, w, 2 * w)).reshape(NA_HEADS, n_dr, 2 * w * w)
    col = skew[:, :, :w * (2 * w - 1)].reshape(NA_HEADS, n_dr, w, 2 * w - 1)[:, :, :, w - 1:]
    qc = np.arange(w)
    cs = np.clip(qc - WIN_W // 2, 0, w - WIN_W)
    col_ok = (qc[None, :] >= cs[:, None]) & (qc[None, :] < cs[:, None] + WIN_W)
    col = jnp.where(col_ok[None, None], col, NEG_BIG)
    masked = jnp.full((NA_HEADS, w, w), NEG_BIG, F32)
    classes = []
    for band in (0, min(1, n_bands - 1), n_bands - 1):
        q_rows = []
        for i in range(BAND_ROWS):
            r = BAND_ROWS * band + i
            rs = min(max(r - WIN_H // 2, 0), rows - WIN_H)
            blocks = []
            for kk in range(3 * BAND_ROWS):
                kr = BAND_ROWS * (band - 1) + kk
                ok = 0 <= kr < rows and rs <= kr < rs + WIN_H
                blocks.append(col[:, kr - r + WIN_H - 1] if ok else masked)
            q_rows.append(jnp.concatenate(blocks, axis=-1))
        classes.append(jnp.concatenate(q_rows, axis=-2))
    return jnp.stack(classes)


def _fourier_kernel(x_ref, cs_ref, dft_ref, dftc_ref, w_ref, o_ref, x12_ref, *, n_ctx, n_lat, tile_off):
    t = pl.program_id(1) + tile_off

    def head_dft(x):
        x12 = jnp.dot(x, cs_ref[...], preferred_element_type=F32).astype(BF16)
        return jnp.concatenate([x12[:, :GROUP_W], x12[:, GROUP_W:]], axis=0)

    def finish(y, n):
        y = y * (1.0 / math.sqrt(n * FNO_HEAD_W))
        o_ref[0] = jnp.dot(y.astype(BF16), w_ref[...], preferred_element_type=F32)

    if tile_off == 0:
        @pl.when(t == 0)
        def _():
            x12 = head_dft(x_ref[0, :n_ctx, :])
            finish(jnp.dot(dftc_ref[...], x12, preferred_element_type=F32), n_ctx)

    @pl.when(t == n_ctx // TILE)
    def _():
        x12_ref[...] = head_dft(x_ref[0, n_ctx:, :])

    @pl.when(t >= n_ctx // TILE)
    def _():
        row0 = pl.multiple_of((t - n_ctx // TILE) * TILE, TILE)
        finish(jnp.dot(dft_ref[pl.ds(row0, TILE), :], x12_ref[...], preferred_element_type=F32), n_lat)


def _fourier(qkv, cs_bd, dft, dft_ctx, w_f, *, need_ctx, n_ctx):
    n_batch, n_tok, _ = qkv.shape
    n_lat = n_tok - n_ctx
    off = 0 if need_ctx else n_ctx // TILE
    const = lambda shape: pl.BlockSpec(shape, lambda b, i: (0,) * len(shape))
    return pl.pallas_call(
        functools.partial(_fourier_kernel, n_ctx=n_ctx, n_lat=n_lat, tile_off=off),
        grid=(n_batch, n_tok // TILE - off),
        in_specs=[pl.BlockSpec((1, n_tok, GROUP_W), lambda b, i: (b, 0, 5)),
                  const(cs_bd.shape), const(dft.shape), const(dft_ctx.shape), const(w_f.shape)],
        out_specs=pl.BlockSpec((1, TILE, GROUP_W), lambda b, i: (b, i + off, 0)),
        out_shape=_sds((n_batch, n_tok, GROUP_W), F32),
        scratch_shapes=[pltpu.VMEM((2 * n_lat, GROUP_W), BF16)],
        compiler_params=_params(("arbitrary", "arbitrary")),
        name="fourier",
    )(qkv, cs_bd, dft, dft_ctx, w_f)


def _dft_matrix(n):
    k = lax.broadcasted_iota(jnp.int32, (n, n), 0)
    m = lax.broadcasted_iota(jnp.int32, (n, n), 1)
    ang = ((k * m) % n).astype(F32) * (2.0 * math.pi / n)
    return jnp.concatenate([jnp.cos(ang), -jnp.sin(ang)], axis=1).astype(BF16)


def _head_dft_matrix():
    a = np.arange(GROUP_W)
    same = (a[:, None] // FNO_HEAD_W) == (a[None, :] // FNO_HEAD_W)
    ang = 2.0 * np.pi * (((a[:, None] % FNO_HEAD_W) * (a[None, :] % FNO_HEAD_W)) % FNO_HEAD_W) / FNO_HEAD_W
    m = np.concatenate([np.where(same, np.cos(ang), 0.0), np.where(same, np.sin(ang), 0.0)], axis=1)
    return jnp.asarray(m, F32).astype(BF16)


def _s5_kernel(u_ref, a_ref, bd_ref, cd_ref, y_ref, bu_ref, h_ref, *, n_batch):
    rev = pl.program_id(0) == 1
    rows = S5_STEPS * n_batch

    @pl.when(pl.program_id(1) == 0)
    def _():
        h_ref[...] = jnp.zeros_like(h_ref)

    u = pltpu.einshape("btc->tbc", u_ref[...]).reshape(rows, GROUP_W)
    bu_ref[...] = jnp.dot(u.astype(BF16), bd_ref[0], preferred_element_type=F32)

    chunk = 2 * LANES
    for sc in range(SSM_W // chunk):
        re = pl.ds(sc * chunk, chunk)
        im = pl.ds(SSM_W + sc * chunk, chunk)
        a_re = jnp.broadcast_to(a_ref[0, :, re], (n_batch, chunk))
        a_im = jnp.broadcast_to(a_ref[0, :, im], (n_batch, chunk))

        def step(s, carry, re=re, im=im, a_re=a_re, a_im=a_im):
            h_re, h_im = carry
            s = jnp.where(rev, S5_STEPS - 1 - s, s)
            r = pl.ds(pl.multiple_of(s * n_batch, n_batch), n_batch)
            n_re = a_re * h_re - a_im * h_im + bu_ref[r, re]
            n_im = a_re * h_im + a_im * h_re + bu_ref[r, im]
            bu_ref[r, re] = n_re
            bu_ref[r, im] = n_im
            return n_re, n_im

        h_re, h_im = lax.fori_loop(0, S5_STEPS, step, (h_ref[:, re], h_ref[:, im]), unroll=4)
        h_ref[:, re] = h_re
        h_ref[:, im] = h_im

    y = jnp.dot(bu_ref[...].astype(BF16), cd_ref[0], preferred_element_type=F32)
    y_ref[0] = pltpu.einshape("tbc->btc", y.reshape(S5_STEPS, n_batch, GROUP_W))


def _s5(u, a_tab, b_dense, c_dense, *, n_ctx):
    n_batch, n_tok, _ = u.shape
    n_blk = n_tok // S5_STEPS
    n_cblk = n_ctx // S5_STEPS

    def blk(d, s):
        back = jnp.where(s < n_cblk, n_cblk - 1 - s, n_blk - 1 - (s - n_cblk))
        return jnp.where(d == 0, s, back)

    return pl.pallas_call(
        functools.partial(_s5_kernel, n_batch=n_batch),
        grid=(2, n_blk),
        in_specs=[pl.BlockSpec((n_batch, S5_STEPS, GROUP_W), lambda d, s: (0, blk(d, s), 0)),
                  pl.BlockSpec((1, 1, 2 * SSM_W), lambda d, s: (d, 0, 0)),
                  pl.BlockSpec((1, GROUP_W, 2 * SSM_W), lambda d, s: (d, 0, 0)),
                  pl.BlockSpec((1, 2 * SSM_W, GROUP_W), lambda d, s: (d, 0, 0))],
        out_specs=pl.BlockSpec((1, n_batch, S5_STEPS, GROUP_W), lambda d, s: (d, 0, blk(d, s), 0)),
        out_shape=_sds((2, n_batch, n_tok, GROUP_W), F32),
        scratch_shapes=[pltpu.VMEM((S5_STEPS * n_batch, 2 * SSM_W), F32),
                        pltpu.VMEM((n_batch, 2 * SSM_W), F32)],
        compiler_params=_params(("arbitrary", "arbitrary")),
        name="s5",
    )(u, a_tab, b_dense, c_dense)


def _s5_tables(lam_re, lam_im, log_dt, b_re, b_im, c_re, c_im):
    lam = lax.complex(lam_re.astype(F32), lam_im.astype(F32))
    dt = jnp.exp(log_dt.astype(F32))[..., None]
    a_bar = jnp.exp(lam * dt)
    b_bar = ((a_bar - 1.0) / lam)[..., None] * lax.complex(b_re.astype(F32), b_im.astype(F32))
    eye = jnp.eye(SSM_GROUPS, dtype=F32)

    def embed_b(m):
        m = jnp.transpose(m, (0, 1, 3, 2))[:, :, :, None, :] * eye[None, :, None, :, None]
        return m.reshape(2, GROUP_W, SSM_W)

    def embed_c(m):
        m = jnp.transpose(m, (0, 1, 3, 2))[:, :, :, None, :] * eye[None, :, None, :, None]
        return m.reshape(2, SSM_W, GROUP_W)

    a_tab = jnp.concatenate([jnp.real(a_bar).reshape(2, 1, SSM_W), jnp.imag(a_bar).reshape(2, 1, SSM_W)], axis=-1)
    b_dense = jnp.concatenate([embed_b(jnp.real(b_bar)), embed_b(jnp.imag(b_bar))], axis=-1).astype(BF16)
    c_dense = jnp.concatenate([embed_c(c_re.astype(F32)), -embed_c(c_im.astype(F32))], axis=1).astype(BF16)
    return a_tab, b_dense, c_dense


def _gelu_tanh(x):
    return x * (0.5 * (1.0 + jnp.tanh(math.sqrt(2.0 / math.pi) * (x + 0.044715 * (x * x * x)))))


def _mixout_kernel(*refs, split_input):
    if split_input:
        x_ref, ctx_ref, *refs = refs
    else:
        x_ref, *refs = refs
    (mod_ref, oa_ref, od_ref, of_ref, u_ref, ys_ref, d_ref, wg_ref, bg_ref, gg_ref, wo_ref, o_ref) = refs
    x = x_ref[0]
    if split_input:
        x = jnp.where(pl.program_id(1) == 0, ctx_ref[0], x)
    _, _, ga1 = _mod_slices(mod_ref, 0)
    y = u_ref[0] * d_ref[...] + ys_ref[0, 0] + ys_ref[1, 0]
    g = _gelu_tanh(y)
    z = jnp.dot(g.astype(BF16), wg_ref[...], preferred_element_type=F32) + bg_ref[...]
    o_s = g * jax.nn.sigmoid(z)
    parts = [oa_ref[0], od_ref[0], of_ref[0], o_s]
    normed = [(_rms_rows(p) * gg_ref[:, j * GROUP_W:(j + 1) * GROUP_W]).astype(BF16) for j, p in enumerate(parts)]
    r = jnp.dot(jnp.concatenate(normed, axis=-1), wo_ref[...], preferred_element_type=F32)
    o_ref[0] = x + ga1 * r


def _mixout(xs, mod, o_a, o_d, o_f, u, ys, d_skip, w_glu, b_glu, g_group, w_out, *, need_ctx, n_ctx):
    split_input = len(xs) == 2
    n_batch, n_tok, _ = u.shape
    off = 0 if need_ctx else n_ctx // TILE
    n_out = n_tok - off * TILE
    tok = (1, TILE, D_MODEL)
    if split_input:
        assert need_ctx
        x_specs = [pl.BlockSpec(tok, lambda b, i: (b, jnp.maximum(i - 1, 0), 0)),
                   pl.BlockSpec(tok, lambda b, i: (b, 0, 0))]
    else:
        x_specs = [pl.BlockSpec(tok, lambda b, i: (b, i + off, 0))]
    grp = pl.BlockSpec((1, TILE, GROUP_W), lambda b, i: (b, i + off, 0))
    const = lambda shape: pl.BlockSpec(shape, lambda b, i: (0,) * len(shape))
    return pl.pallas_call(
        functools.partial(_mixout_kernel, split_input=split_input),
        grid=(n_batch, n_out // TILE),
        in_specs=x_specs + [
            pl.BlockSpec((1, 1, N_MOD * D_MODEL), lambda b, i: (jnp.where(i + off == 0, n_batch, b), 0, 0)),
            grp, grp, grp, grp,
            pl.BlockSpec((2, 1, TILE, GROUP_W), lambda b, i: (0, b, i + off, 0)),
            const((1, GROUP_W)), const((GROUP_W, GROUP_W)), const((1, GROUP_W)),
            const((1, D_MODEL)), const((D_MODEL, D_MODEL)),
        ],
        out_specs=pl.BlockSpec(tok, lambda b, i: (b, i, 0)),
        out_shape=_sds((n_batch, n_out, D_MODEL), F32),
        compiler_params=_params(("arbitrary", "arbitrary")),
        name="mixout",
    )(*xs, mod, o_a, o_d, o_f, u, ys, d_skip, w_glu, b_glu, g_group, w_out)


def _ffn_kernel(x_ref, mod_ref, g2_ref, w1_ref, w3_ref, w2_ref, o_ref):
    x = x_ref[0]
    sh2, sc2, ga2 = _mod_slices(mod_ref, 3)
    h = ((_rms_rows(x) * g2_ref[...]) * (1.0 + sc2) + sh2).astype(BF16)
    acc = jnp.zeros((TILE, D_MODEL), F32)
    for c in range(D_FF // FF_CHUNK):
        cols = slice(c * FF_CHUNK, (c + 1) * FF_CHUNK)
        a = jnp.dot(h, w1_ref[:, cols], preferred_element_type=F32)
        b = jnp.dot(h, w3_ref[:, cols], preferred_element_type=F32)
        t = (a * jax.nn.sigmoid(a)) * b
        acc = acc + jnp.dot(t.astype(BF16), w2_ref[cols, :], preferred_element_type=F32)
    o_ref[0] = x + ga2 * acc


def _ffn(x, mod, g2, w1, w3, w2, *, has_ctx):
    n_batch, n_tok, _ = x.shape
    tok = pl.BlockSpec((1, TILE, D_MODEL), lambda b, i: (b, i, 0))
    const = lambda shape: pl.BlockSpec(shape, lambda b, i: (0,) * len(shape))
    if has_ctx:
        mod_map = lambda b, i: (jnp.where(i == 0, n_batch, b), 0, 0)
    else:
        mod_map = lambda b, i: (b, 0, 0)
    return pl.pallas_call(
        _ffn_kernel,
        grid=(n_batch, n_tok // TILE),
        in_specs=[tok, pl.BlockSpec((1, 1, N_MOD * D_MODEL), mod_map), const((1, D_MODEL)),
                  const(w1.shape), const(w3.shape), const(w2.shape)],
        out_specs=tok,
        out_shape=_sds(x.shape, F32),
        compiler_params=_params(("arbitrary", "arbitrary")),
        name="ffn",
    )(x, mod, g2, w1, w3, w2)


def _rope_tables(n_lat, n_ctx):
    t = jnp.arange(n_lat, dtype=jnp.int32)
    rows = (t // GRID_W).astype(F32)
    cols = (t % GRID_W).astype(F32)
    axis_dim = HEAD_DIM // 2
    inv_freq = ROPE_THETA ** (-jnp.arange(0, axis_dim, 2, dtype=F32) / axis_dim)
    ang_r = rows[:, None] * inv_freq[None, :]
    ang_c = cols[:, None] * inv_freq[None, :]
    cos = jnp.concatenate([jnp.cos(ang_r)] * 2 + [jnp.cos(ang_c)] * 2, axis=-1)
    sin = jnp.concatenate([-jnp.sin(ang_r), jnp.sin(ang_r), -jnp.sin(ang_c), jnp.sin(ang_c)], axis=-1)
    cos = jnp.concatenate([jnp.ones((n_ctx, HEAD_DIM), F32), cos], axis=0)
    sin = jnp.concatenate([jnp.zeros((n_ctx, HEAD_DIM), F32), sin], axis=0)
    return jnp.tile(cos, (1, 2)), jnp.tile(sin, (1, 2))


def kernel(x, c, ctx, c_ctx, w_mod, b_mod, g_norm1, w_in, att_q_gain, att_k_gain, na_q_gain, na_k_gain, na_rel_bias, w_fourier, ssm_lam_re, ssm_lam_im, ssm_log_dt, ssm_b_re, ssm_b_im, ssm_c_re, ssm_c_im, ssm_d, w_glu, b_glu, g_group, w_out, g_norm2, w_ff1, w_ff3, w_ff2):
    n_batch, n_lat, d = x.shape
    n_ctx = ctx.shape[1]
    n_tok = n_lat + n_ctx
    assert d == D_MODEL and n_ctx == TILE and n_lat % TILE == 0 and n_lat % GRID_W == 0
    assert n_batch % 8 == 0 and n_tok % S5_STEPS == 0

    mod_rows = -(-(n_batch + 1) // 8) * 8
    cc = jnp.concatenate([c, c_ctx[None, :], jnp.zeros((mod_rows - n_batch - 1, d), F32)], axis=0)
    mod = _modulation(cc, w_mod, b_mod.reshape(DEPTH, 1, N_MOD * d)).reshape(DEPTH, mod_rows, 1, N_MOD * d)

    cos_t, sin_t = _rope_tables(n_lat, n_ctx)
    cs_bd = _head_dft_matrix()
    dft = _dft_matrix(n_lat)
    dft_ctx = _dft_matrix(n_ctx)
    in_cols = np.concatenate([_Q_PERM, np.arange(GROUP_W, IN_W)])
    out_rows = np.concatenate([_Q_PERM, np.arange(GROUP_W, D_MODEL)])
    scale = HEAD_DIM ** -0.5
    tile2 = lambda g, s=1.0: jnp.tile(g.astype(F32) * s, 2)[None, :]
    row = lambda v: v.astype(F32)[None, :]

    xs = (x, ctx)
    for l in range(DEPTH):
        need_ctx = l < DEPTH - 1
        qkv, u = _inproj(xs, mod[l], row(g_norm1[l]), w_in[l][:, in_cols].astype(BF16), cos_t, sin_t,
                         tile2(att_q_gain[l], scale), tile2(att_k_gain[l]),
                         tile2(na_q_gain[l], scale), tile2(na_k_gain[l]),
                         n_batch=n_batch, n_tok=n_tok)
        o_a = _gqa(qkv, need_ctx=need_ctx, n_ctx=n_ctx)
        o_d = _na(qkv, _na_bias_table(na_rel_bias[l], n_lat // GRID_W), need_ctx=need_ctx, n_ctx=n_ctx)
        o_f = _fourier(qkv, cs_bd, dft, dft_ctx, w_fourier[l].astype(BF16), need_ctx=need_ctx, n_ctx=n_ctx)
        a_tab, b_dense, c_dense = _s5_tables(ssm_lam_re[l], ssm_lam_im[l], ssm_log_dt[l], ssm_b_re[l],
                                             ssm_b_im[l], ssm_c_re[l], ssm_c_im[l])
        ys = _s5(u, a_tab, b_dense, c_dense, n_ctx=n_ctx)
        x1 = _mixout(xs, mod[l], o_a, o_d, o_f, u, ys, row(ssm_d[l]), w_glu[l].astype(BF16), row(b_glu[l]),
                     row(g_group[l][out_rows]), w_out[l][out_rows, :].astype(BF16),
                     need_ctx=need_ctx, n_ctx=n_ctx)
        x2 = _ffn(x1, mod[l], row(g_norm2[l]), w_ff1[l].astype(BF16), w_ff3[l].astype(BF16),
                  w_ff2[l].astype(BF16), has_ctx=need_ctx)
        xs = (x2,)
    return xs[0]
```

```python
import functools
import math

import numpy as np
import jax
import jax.numpy as jnp
from jax import lax
from jax.experimental import pallas as pl
from jax.experimental.pallas import tpu as pltpu

F32 = jnp.float32
BF16 = jnp.bfloat16

D_MODEL = 1024
DEPTH = 2
GRID_W = 64
HEAD_DIM = 64
N_GROUPS = 4
GROUP_W = D_MODEL // N_GROUPS
WIN_H = 8
WIN_W = 16
NA_HEADS = GROUP_W // HEAD_DIM
FNO_HEAD_W = 64
SSM_GROUP = 16
SSM_GROUPS = GROUP_W // SSM_GROUP
SSM_STATE = 64
SSM_W = SSM_GROUPS * SSM_STATE
ROPE_THETA = 10000.0
D_FF = ((8 * D_MODEL // 3 + 255) // 256) * 256
N_MOD = 6
EPS = 1e-6
IN_W = 7 * GROUP_W
QKV_W = 6 * GROUP_W

LANES = 128
TILE = 256
BAND_ROWS = TILE // GRID_W
S5_STEPS = 128
FF_CHUNK = D_FF // 2
VMEM_LIMIT_BYTES = 56 * 2**20
NEG_BIG = -1e30
LOG2E = 1.0 / math.log(2.0)
EXP2_SAFE = 60.0

_Q_PERM = np.concatenate([np.arange(0, 64), np.arange(128, 192), np.arange(64, 128), np.arange(192, 256)])


def _params(sem):
    return pltpu.CompilerParams(dimension_semantics=sem, vmem_limit_bytes=VMEM_LIMIT_BYTES)


def _sds(shape, dtype):
    return jax.ShapeDtypeStruct(shape, dtype)


def _mod_kernel(c_ref, w_ref, b_ref, o_ref):
    c = c_ref[...]
    s = c * jax.nn.sigmoid(c)
    o_ref[0] = jnp.dot(s, w_ref[0], preferred_element_type=F32, precision=lax.Precision.HIGHEST) + b_ref[0]


def _modulation(cc, w_mod, b_mod):
    rows, d = cc.shape
    n_blk = 4
    bw = N_MOD * d // n_blk
    return pl.pallas_call(
        _mod_kernel,
        grid=(DEPTH, n_blk),
        in_specs=[
            pl.BlockSpec((rows, d), lambda l, j: (0, 0)),
            pl.BlockSpec((1, d, bw), lambda l, j: (l, 0, j)),
            pl.BlockSpec((1, 1, bw), lambda l, j: (l, 0, j)),
        ],
        out_specs=pl.BlockSpec((1, rows, bw), lambda l, j: (l, 0, j)),
        out_shape=_sds((DEPTH, rows, N_MOD * d), F32),
        compiler_params=_params(("arbitrary", "arbitrary")),
        name="modulation",
    )(cc, w_mod, b_mod)


def _rms_rows(x):
    return x * lax.rsqrt(jnp.mean(x * x, axis=-1, keepdims=True) + EPS)


def _mod_slices(mod_ref, first):
    m = mod_ref[0]
    return [m[:, (first + k) * D_MODEL:(first + k + 1) * D_MODEL] for k in range(3)]


def _head_norm(t, gain):
    lo = lax.broadcasted_iota(jnp.int32, t.shape, 1) < HEAD_DIM
    sq = t * t
    s_lo = jnp.sum(jnp.where(lo, sq, 0.0), axis=-1, keepdims=True)
    s_hi = jnp.sum(jnp.where(lo, 0.0, sq), axis=-1, keepdims=True)
    ms = jnp.where(lo, s_lo, s_hi) * (1.0 / HEAD_DIM)
    return t * lax.rsqrt(ms + EPS) * gain


def _rope(t, cos, sin):
    first = (lax.broadcasted_iota(jnp.int32, t.shape, 1) & 16) == 0
    partner = jnp.where(first, pltpu.roll(t, LANES - 16, axis=1), pltpu.roll(t, 16, axis=1))
    return t * cos + partner * sin


def _softmax_pv(s, v, bounded):
    p = jnp.exp2(s if bounded else s - jnp.max(s, axis=-1, keepdims=True))
    l = jnp.sum(p, axis=-1, keepdims=True)
    return jnp.dot(p.astype(BF16), v, preferred_element_type=F32) / l


def _when_bounded(flag_ref, body):
    bounded = flag_ref[0] == 1
    pl.when(bounded)(lambda: body(True))
    pl.when(jnp.logical_not(bounded))(lambda: body(False))


def _logits_bounded(gq, gk, extra=0.0):
    bound = math.sqrt(HEAD_DIM) * jnp.max(jnp.abs(gq)) * jnp.max(jnp.abs(gk)) + extra
    return (bound * LOG2E <= EXP2_SAFE).astype(jnp.int32).reshape(1)


def _split_heads(q):
    lo = lax.broadcasted_iota(jnp.int32, q.shape, 1) < HEAD_DIM
    zero = jnp.zeros_like(q)
    return jnp.concatenate([jnp.where(lo, q, zero), jnp.where(lo, zero, q)], axis=0)


def _merge_heads(o, rows):
    lo = lax.broadcasted_iota(jnp.int32, (rows, LANES), 1) < HEAD_DIM
    return jnp.where(lo, o[:rows], o[rows:])


def _nt_dot(a, b):
    return lax.dot_general(a, b, (((1,), (1,)), ((), ())), preferred_element_type=F32)


def _inproj_kernel(*refs, split_input):
    if split_input:
        x_ref, ctx_ref, *refs = refs
    else:
        x_ref, *refs = refs
    (mod_ref, g1_ref, w_ref, cos_ref, sin_ref, gq_ref, gk_ref, nq_ref, nk_ref, qkv_ref, u_ref) = refs
    x = x_ref[0]
    if split_input:
        x = jnp.where(pl.program_id(1) == 0, ctx_ref[0], x)
    sh1, sc1, _ = _mod_slices(mod_ref, 0)
    h = (_rms_rows(x) * g1_ref[...]) * (1.0 + sc1) + sh1
    p = jnp.dot(h.astype(BF16), w_ref[...], preferred_element_type=F32)
    cos = cos_ref[...]
    sin = sin_ref[...]

    def tile(j):
        return p[:, j * LANES:(j + 1) * LANES]

    out = [
        _rope(_head_norm(tile(0), gq_ref[...]), cos, sin),
        _rope(_head_norm(tile(1), gq_ref[...]), cos, sin),
        _rope(_head_norm(tile(2), gk_ref[...]), cos, sin),
        tile(3),
        _head_norm(tile(4), nq_ref[...]),
        _head_norm(tile(5), nq_ref[...]),
        _head_norm(tile(6), nk_ref[...]),
        _head_norm(tile(7), nk_ref[...]),
        tile(8), tile(9), tile(10), tile(11),
    ]
    for j, t in enumerate(out):
        qkv_ref[0, :, j * LANES:(j + 1) * LANES] = t.astype(BF16)
    u_ref[0] = p[:, QKV_W:]


def _inproj(xs, mod, g1, w_in, cos_t, sin_t, gq, gk, nq, nk, *, n_batch, n_tok):
    split_input = len(xs) == 2
    n_tiles = n_tok // TILE
    tok = (1, TILE, D_MODEL)
    if split_input:
        x_specs = [pl.BlockSpec(tok, lambda b, i: (b, jnp.maximum(i - 1, 0), 0)),
                   pl.BlockSpec(tok, lambda b, i: (b, 0, 0))]
    else:
        x_specs = [pl.BlockSpec(tok, lambda b, i: (b, i, 0))]
    vec = lambda w: pl.BlockSpec((1, w), lambda b, i: (0, 0))
    return pl.pallas_call(
        functools.partial(_inproj_kernel, split_input=split_input),
        grid=(n_batch, n_tiles),
        in_specs=x_specs + [
            pl.BlockSpec((1, 1, N_MOD * D_MODEL), lambda b, i: (jnp.where(i == 0, n_batch, b), 0, 0)),
            vec(D_MODEL),
            pl.BlockSpec((D_MODEL, IN_W), lambda b, i: (0, 0)),
            pl.BlockSpec((TILE, LANES), lambda b, i: (i, 0)),
            pl.BlockSpec((TILE, LANES), lambda b, i: (i, 0)),
            vec(LANES), vec(LANES), vec(LANES), vec(LANES),
        ],
        out_specs=[pl.BlockSpec((1, TILE, QKV_W), lambda b, i: (b, i, 0)),
                   pl.BlockSpec((1, TILE, GROUP_W), lambda b, i: (b, i, 0))],
        out_shape=[_sds((n_batch, n_tok, QKV_W), BF16), _sds((n_batch, n_tok, GROUP_W), F32)],
        compiler_params=_params(("arbitrary", "arbitrary")),
        name="inproj",
    )(*xs, mod, g1, w_in, cos_t, sin_t, gq, gk, nq, nk)


def _gqa_kernel(flag_ref, qa_ref, qb_ref, k_ref, v_ref, o_ref, *, n_ctx, tile_off):
    def attend(n_keys, bounded):
        k = k_ref[0, :n_keys, :]
        v = v_ref[0, :n_keys, :]
        outs = []
        for q_ref in (qa_ref, qb_ref):
            q2 = _split_heads(q_ref[0])
            outs.append(_merge_heads(_softmax_pv(_nt_dot(q2, k), v, bounded), TILE))
        o_ref[0] = jnp.concatenate(outs, axis=-1).astype(o_ref.dtype)

    if tile_off == 0:
        is_ctx = pl.program_id(1) == 0
        pl.when(is_ctx)(lambda: _when_bounded(flag_ref, functools.partial(attend, n_ctx)))
        pl.when(jnp.logical_not(is_ctx))(lambda: _when_bounded(flag_ref, functools.partial(attend, k_ref.shape[1])))
    else:
        _when_bounded(flag_ref, functools.partial(attend, k_ref.shape[1]))


_SMEM_SPEC = pl.BlockSpec(memory_space=pltpu.SMEM)


def _gqa(bounded, qkv, *, need_ctx, n_ctx):
    n_batch, n_tok, _ = qkv.shape
    off = 0 if need_ctx else n_ctx // TILE
    q_spec = lambda col: pl.BlockSpec((1, TILE, LANES), lambda b, i: (b, i + off, col))
    kv_spec = lambda col: pl.BlockSpec((1, n_tok, LANES), lambda b, i: (b, 0, col))
    return pl.pallas_call(
        functools.partial(_gqa_kernel, n_ctx=n_ctx, tile_off=off),
        grid=(n_batch, n_tok // TILE - off),
        in_specs=[_SMEM_SPEC, q_spec(0), q_spec(1), kv_spec(2), kv_spec(3)],
        out_specs=pl.BlockSpec((1, TILE, GROUP_W), lambda b, i: (b, i, 0)),
        out_shape=_sds((n_batch, n_tok - off * TILE, GROUP_W), BF16),
        compiler_params=_params(("arbitrary", "arbitrary")),
        name="gqa",
    )(bounded, qkv, qkv, qkv, qkv)


def _na_kernel(flag_ref, q_ref, kp_ref, kc_ref, kn_ref, kx_ref, vp_ref, vc_ref, vn_ref, vx_ref, tbl_ref, o_ref, *, tile_off):
    def attend(local, bounded):
        outs = []
        for pair in range(2):
            cols = slice(pair * LANES, (pair + 1) * LANES)
            q2 = _split_heads(q_ref[0, :, cols])
            if local:
                k = jnp.concatenate([r[0, :, cols] for r in (kp_ref, kc_ref, kn_ref, kx_ref)], axis=0)
                v = jnp.concatenate([r[0, :, cols] for r in (vp_ref, vc_ref, vn_ref, vx_ref)], axis=0)
                bias = jnp.concatenate([tbl_ref[0, 2 * pair], tbl_ref[0, 2 * pair + 1]], axis=0)
                s = _nt_dot(q2, k)
                s = jnp.concatenate([s[:, :3 * TILE] + bias, s[:, 3 * TILE:]], axis=-1)
            else:
                k = kx_ref[0, :, cols]
                v = vx_ref[0, :, cols]
                s = _nt_dot(q2, k)
            outs.append(_merge_heads(_softmax_pv(s, v, bounded), TILE))
        o_ref[0] = jnp.concatenate(outs, axis=-1).astype(o_ref.dtype)

    if tile_off == 0:
        is_ctx = pl.program_id(1) == 0
        pl.when(is_ctx)(lambda: _when_bounded(flag_ref, functools.partial(attend, False)))
        pl.when(jnp.logical_not(is_ctx))(lambda: _when_bounded(flag_ref, functools.partial(attend, True)))
    else:
        _when_bounded(flag_ref, functools.partial(attend, True))


def _na(bounded, qkv, table, *, need_ctx, n_ctx):
    n_batch, n_tok, _ = qkv.shape
    assert n_ctx == TILE
    n_tiles = n_tok // TILE
    off = 0 if need_ctx else 1
    blk = (1, TILE, GROUP_W)
    cur = lambda col: pl.BlockSpec(blk, lambda b, i: (b, i + off, col))
    prev = lambda col: pl.BlockSpec(blk, lambda b, i: (b, jnp.maximum(i + off - 1, 1), col))
    nxt = lambda col: pl.BlockSpec(blk, lambda b, i: (b, jnp.minimum(i + off + 1, n_tiles - 1), col))
    ctx = lambda col: pl.BlockSpec(blk, lambda b, i: (b, 0, col))

    def cls(b, i):
        t = i + off
        return (jnp.where(t == n_tiles - 1, 2, jnp.where(t <= 1, 0, 1)), 0, 0, 0)

    return pl.pallas_call(
        functools.partial(_na_kernel, tile_off=off),
        grid=(n_batch, n_tiles - off),
        in_specs=[_SMEM_SPEC, cur(2), prev(3), cur(3), nxt(3), ctx(3), prev(4), cur(4), nxt(4), ctx(4),
                  pl.BlockSpec((1, NA_HEADS, TILE, 3 * TILE), cls)],
        out_specs=pl.BlockSpec(blk, lambda b, i: (b, i, 0)),
        out_shape=_sds((n_batch, n_tok - off * TILE, GROUP_W), BF16),
        compiler_params=_params(("arbitrary", "arbitrary")),
        name="na",
    )(bounded, *([qkv] * 9), table)


def _na_bias_table(rel_bias, rows):
    n_bands = rows // BAND_ROWS
    assert rows >= WIN_H and rows % BAND_ROWS == 0
    n_dr, n_dc = 2 * WIN_H - 1, 2 * WIN_W - 1
    w = GRID_W
    shift = w - WIN_W
    vp = jnp.pad(rel_bias.astype(F32) * LOG2E, ((0, 0), (0, 0), (shift, 2 * w - shift - n_dc)))
    skew = jnp.broadcast_to(vp[:, :, None, :], (NA_HEADS, n_dr, w, 2 * w)).reshape(NA_HEADS, n_dr, 2 * w * w)
    col = skew[:, :, :w * (2 * w - 1)].reshape(NA_HEADS, n_dr, w, 2 * w - 1)[:, :, :, w - 1:]
    qc = np.arange(w)
    cs = np.clip(qc - WIN_W // 2, 0, w - WIN_W)
    col_ok = (qc[None, :] >= cs[:, None]) & (qc[None, :] < cs[:, None] + WIN_W)
    col = jnp.where(col_ok[None, None], col, NEG_BIG)
    masked = jnp.full((NA_HEADS, w, w), NEG_BIG, F32)
    classes = []
    for band in (0, min(1, n_bands - 1), n_bands - 1):
        q_rows = []
        for i in range(BAND_ROWS):
            r = BAND_ROWS * band + i
            rs = min(max(r - WIN_H // 2, 0), rows - WIN_H)
            blocks = []
            for kk in range(3 * BAND_ROWS):
                kr = BAND_ROWS * (band - 1) + kk
                ok = 0 <= kr < rows and rs <= kr < rs + WIN_H
                blocks.append(col[:, kr - r + WIN_H - 1] if ok else masked)
            q_rows.append(jnp.concatenate(blocks, axis=-1))
        classes.append(jnp.concatenate(q_rows, axis=-2))
    return jnp.stack(classes)


def _fourier_kernel(x_ref, cs_ref, dft_ref, dftc_ref, w_ref, o_ref, x12_ref, *, n_ctx, n_lat, tile_off):
    t = pl.program_id(1) + tile_off

    def head_dft(x):
        x12 = jnp.dot(x, cs_ref[...], preferred_element_type=F32).astype(BF16)
        return jnp.concatenate([x12[:, :GROUP_W], x12[:, GROUP_W:]], axis=0)

    def finish(y, n):
        y = y * (1.0 / math.sqrt(n * FNO_HEAD_W))
        o_ref[0] = jnp.dot(y.astype(BF16), w_ref[...], preferred_element_type=F32).astype(o_ref.dtype)

    if tile_off == 0:
        @pl.when(t == 0)
        def _():
            x12 = head_dft(x_ref[0, :n_ctx, :])
            finish(jnp.dot(dftc_ref[...], x12, preferred_element_type=F32), n_ctx)

    @pl.when(t == n_ctx // TILE)
    def _():
        x12_ref[...] = head_dft(x_ref[0, n_ctx:, :])

    @pl.when(t >= n_ctx // TILE)
    def _():
        row0 = pl.multiple_of((t - n_ctx // TILE) * TILE, TILE)
        finish(jnp.dot(dft_ref[pl.ds(row0, TILE), :], x12_ref[...], preferred_element_type=F32), n_lat)


def _fourier(qkv, cs_bd, dft, dft_ctx, w_f, *, need_ctx, n_ctx):
    n_batch, n_tok, _ = qkv.shape
    n_lat = n_tok - n_ctx
    off = 0 if need_ctx else n_ctx // TILE
    const = lambda shape: pl.BlockSpec(shape, lambda b, i: (0,) * len(shape))
    return pl.pallas_call(
        functools.partial(_fourier_kernel, n_ctx=n_ctx, n_lat=n_lat, tile_off=off),
        grid=(n_batch, n_tok // TILE - off),
        in_specs=[pl.BlockSpec((1, n_tok, GROUP_W), lambda b, i: (b, 0, 5)),
                  const(cs_bd.shape), const(dft.shape), const(dft_ctx.shape), const(w_f.shape)],
        out_specs=pl.BlockSpec((1, TILE, GROUP_W), lambda b, i: (b, i, 0)),
        out_shape=_sds((n_batch, n_tok - off * TILE, GROUP_W), BF16),
        scratch_shapes=[pltpu.VMEM((2 * n_lat, GROUP_W), BF16)],
        compiler_params=_params(("arbitrary", "arbitrary")),
        name="fourier",
    )(qkv, cs_bd, dft, dft_ctx, w_f)


def _dft_matrix(n):
    k = lax.broadcasted_iota(jnp.int32, (n, n), 0)
    m = lax.broadcasted_iota(jnp.int32, (n, n), 1)
    ang = ((k * m) % n).astype(F32) * (2.0 * math.pi / n)
    return jnp.concatenate([jnp.cos(ang), -jnp.sin(ang)], axis=1).astype(BF16)


def _head_dft_matrix():
    a = np.arange(GROUP_W)
    same = (a[:, None] // FNO_HEAD_W) == (a[None, :] // FNO_HEAD_W)
    ang = 2.0 * np.pi * (((a[:, None] % FNO_HEAD_W) * (a[None, :] % FNO_HEAD_W)) % FNO_HEAD_W) / FNO_HEAD_W
    m = np.concatenate([np.where(same, np.cos(ang), 0.0), np.where(same, np.sin(ang), 0.0)], axis=1)
    return jnp.asarray(m, F32).astype(BF16)


def _s5_kernel(u_ref, a_ref, bd_ref, cd_ref, y_ref, bu_ref, h_ref, *, n_batch):
    rev = pl.program_id(0) == 1
    rows = S5_STEPS * n_batch

    @pl.when(pl.program_id(1) == 0)
    def _():
        h_ref[...] = jnp.zeros_like(h_ref)

    u = pltpu.einshape("btc->tbc", u_ref[...]).reshape(rows, GROUP_W)
    bu_ref[...] = jnp.dot(u.astype(BF16), bd_ref[0], preferred_element_type=F32)

    chunk = 4 * LANES
    for sc in range(SSM_W // chunk):
        re = pl.ds(sc * chunk, chunk)
        im = pl.ds(SSM_W + sc * chunk, chunk)
        a_re = jnp.broadcast_to(a_ref[0, :, re], (n_batch, chunk))
        a_im = jnp.broadcast_to(a_ref[0, :, im], (n_batch, chunk))

        def step(s, carry, re=re, im=im, a_re=a_re, a_im=a_im):
            h_re, h_im = carry
            s = jnp.where(rev, S5_STEPS - 1 - s, s)
            r = pl.ds(pl.multiple_of(s * n_batch, n_batch), n_batch)
            n_re = a_re * h_re - a_im * h_im + bu_ref[r, re]
            n_im = a_re * h_im + a_im * h_re + bu_ref[r, im]
            bu_ref[r, re] = n_re
            bu_ref[r, im] = n_im
            return n_re, n_im

        h_re, h_im = lax.fori_loop(0, S5_STEPS, step, (h_ref[:, re], h_ref[:, im]), unroll=4)
        h_ref[:, re] = h_re
        h_ref[:, im] = h_im

    y = jnp.dot(bu_ref[...].astype(BF16), cd_ref[0], preferred_element_type=F32)
    y_ref[0] = pltpu.einshape("tbc->btc", y.reshape(S5_STEPS, n_batch, GROUP_W)).astype(y_ref.dtype)


def _s5(u, a_tab, b_dense, c_dense, *, n_ctx):
    n_batch, n_tok, _ = u.shape
    n_blk = n_tok // S5_STEPS
    n_cblk = n_ctx // S5_STEPS

    def blk(d, s):
        back = jnp.where(s < n_cblk, n_cblk - 1 - s, n_blk - 1 - (s - n_cblk))
        return jnp.where(d == 0, s, back)

    return pl.pallas_call(
        functools.partial(_s5_kernel, n_batch=n_batch),
        grid=(2, n_blk),
        in_specs=[pl.BlockSpec((n_batch, S5_STEPS, GROUP_W), lambda d, s: (0, blk(d, s), 0)),
                  pl.BlockSpec((1, 1, 2 * SSM_W), lambda d, s: (d, 0, 0)),
                  pl.BlockSpec((1, GROUP_W, 2 * SSM_W), lambda d, s: (d, 0, 0)),
                  pl.BlockSpec((1, 2 * SSM_W, GROUP_W), lambda d, s: (d, 0, 0))],
        out_specs=pl.BlockSpec((1, n_batch, S5_STEPS, GROUP_W), lambda d, s: (d, 0, blk(d, s), 0)),
        out_shape=_sds((2, n_batch, n_tok, GROUP_W), BF16),
        scratch_shapes=[pltpu.VMEM((S5_STEPS * n_batch, 2 * SSM_W), F32),
                        pltpu.VMEM((n_batch, 2 * SSM_W), F32)],
        compiler_params=_params(("arbitrary", "arbitrary")),
        name="s5",
    )(u, a_tab, b_dense, c_dense)


def _s5_tables(lam_re, lam_im, log_dt, b_re, b_im, c_re, c_im):
    lam = lax.complex(lam_re.astype(F32), lam_im.astype(F32))
    dt = jnp.exp(log_dt.astype(F32))[..., None]
    a_bar = jnp.exp(lam * dt)
    b_bar = ((a_bar - 1.0) / lam)[..., None] * lax.complex(b_re.astype(F32), b_im.astype(F32))
    eye = jnp.eye(SSM_GROUPS, dtype=F32)

    def embed_b(m):
        m = jnp.transpose(m, (0, 1, 3, 2))[:, :, :, None, :] * eye[None, :, None, :, None]
        return m.reshape(2, GROUP_W, SSM_W)

    def embed_c(m):
        m = jnp.transpose(m, (0, 1, 3, 2))[:, :, :, None, :] * eye[None, :, None, :, None]
        return m.reshape(2, SSM_W, GROUP_W)

    a_tab = jnp.concatenate([jnp.real(a_bar).reshape(2, 1, SSM_W), jnp.imag(a_bar).reshape(2, 1, SSM_W)], axis=-1)
    b_dense = jnp.concatenate([embed_b(jnp.real(b_bar)), embed_b(jnp.imag(b_bar))], axis=-1).astype(BF16)
    c_dense = jnp.concatenate([embed_c(c_re.astype(F32)), -embed_c(c_im.astype(F32))], axis=1).astype(BF16)
    return a_tab, b_dense, c_dense


def _gelu_tanh(x):
    return x * (0.5 * (1.0 + jnp.tanh(math.sqrt(2.0 / math.pi) * (x + 0.044715 * (x * x * x)))))


def _mixout_kernel(*refs, split_input):
    if split_input:
        x_ref, ctx_ref, *refs = refs
    else:
        x_ref, *refs = refs
    (mod_ref, oa_ref, od_ref, of_ref, u_ref, ys_ref, d_ref, wg_ref, bg_ref, gg_ref, wo_ref, o_ref) = refs
    x = x_ref[0]
    if split_input:
        x = jnp.where(pl.program_id(1) == 0, ctx_ref[0], x)
    _, _, ga1 = _mod_slices(mod_ref, 0)
    y = u_ref[0] * d_ref[...] + ys_ref[0, 0].astype(F32) + ys_ref[1, 0].astype(F32)
    g = _gelu_tanh(y)
    z = jnp.dot(g.astype(BF16), wg_ref[...], preferred_element_type=F32) + bg_ref[...]
    o_s = g * jax.nn.sigmoid(z)
    parts = [oa_ref[0].astype(F32), od_ref[0].astype(F32), of_ref[0].astype(F32), o_s]
    normed = [(_rms_rows(p) * gg_ref[:, j * GROUP_W:(j + 1) * GROUP_W]).astype(BF16) for j, p in enumerate(parts)]
    r = jnp.dot(jnp.concatenate(normed, axis=-1), wo_ref[...], preferred_element_type=F32)
    o_ref[0] = x + ga1 * r


def _mixout(xs, mod, o_a, o_d, o_f, u, ys, d_skip, w_glu, b_glu, g_group, w_out, *, need_ctx, n_ctx):
    split_input = len(xs) == 2
    n_batch, n_tok, _ = u.shape
    off = 0 if need_ctx else n_ctx // TILE
    n_out = n_tok - off * TILE
    tok = (1, TILE, D_MODEL)
    if split_input:
        assert need_ctx
        x_specs = [pl.BlockSpec(tok, lambda b, i: (b, jnp.maximum(i - 1, 0), 0)),
                   pl.BlockSpec(tok, lambda b, i: (b, 0, 0))]
    else:
        x_specs = [pl.BlockSpec(tok, lambda b, i: (b, i + off, 0))]
    grp = pl.BlockSpec((1, TILE, GROUP_W), lambda b, i: (b, i + off, 0))
    mix = pl.BlockSpec((1, TILE, GROUP_W), lambda b, i: (b, i, 0))
    const = lambda shape: pl.BlockSpec(shape, lambda b, i: (0,) * len(shape))
    return pl.pallas_call(
        functools.partial(_mixout_kernel, split_input=split_input),
        grid=(n_batch, n_out // TILE),
        in_specs=x_specs + [
            pl.BlockSpec((1, 1, N_MOD * D_MODEL), lambda b, i: (jnp.where(i + off == 0, n_batch, b), 0, 0)),
            mix, mix, mix, grp,
            pl.BlockSpec((2, 1, TILE, GROUP_W), lambda b, i: (0, b, i + off, 0)),
            const((1, GROUP_W)), const((GROUP_W, GROUP_W)), const((1, GROUP_W)),
            const((1, D_MODEL)), const((D_MODEL, D_MODEL)),
        ],
        out_specs=pl.BlockSpec(tok, lambda b, i: (b, i, 0)),
        out_shape=_sds((n_batch, n_out, D_MODEL), F32),
        compiler_params=_params(("arbitrary", "arbitrary")),
        name="mixout",
    )(*xs, mod, o_a, o_d, o_f, u, ys, d_skip, w_glu, b_glu, g_group, w_out)


def _ffn_kernel(x_ref, mod_ref, g2_ref, w1_ref, w3_ref, w2_ref, o_ref):
    x = x_ref[0]
    sh2, sc2, ga2 = _mod_slices(mod_ref, 3)
    h = ((_rms_rows(x) * g2_ref[...]) * (1.0 + sc2) + sh2).astype(BF16)
    acc = jnp.zeros((TILE, D_MODEL), F32)
    for c in range(D_FF // FF_CHUNK):
        cols = slice(c * FF_CHUNK, (c + 1) * FF_CHUNK)
        a = jnp.dot(h, w1_ref[:, cols], preferred_element_type=F32)
        b = jnp.dot(h, w3_ref[:, cols], preferred_element_type=F32)
        t = (a * jax.nn.sigmoid(a)) * b
        acc = acc + jnp.dot(t.astype(BF16), w2_ref[cols, :], preferred_element_type=F32)
    o_ref[0] = x + ga2 * acc


def _ffn(x, mod, g2, w1, w3, w2, *, has_ctx):
    n_batch, n_tok, _ = x.shape
    tok = pl.BlockSpec((1, TILE, D_MODEL), lambda b, i: (b, i, 0))
    const = lambda shape: pl.BlockSpec(shape, lambda b, i: (0,) * len(shape))
    if has_ctx:
        mod_map = lambda b, i: (jnp.where(i == 0, n_batch, b), 0, 0)
    else:
        mod_map = lambda b, i: (b, 0, 0)
    return pl.pallas_call(
        _ffn_kernel,
        grid=(n_batch, n_tok // TILE),
        in_specs=[tok, pl.BlockSpec((1, 1, N_MOD * D_MODEL), mod_map), const((1, D_MODEL)),
                  const(w1.shape), const(w3.shape), const(w2.shape)],
        out_specs=tok,
        out_shape=_sds(x.shape, F32),
        compiler_params=_params(("arbitrary", "arbitrary")),
        name="ffn",
    )(x, mod, g2, w1, w3, w2)


def _rope_tables(n_lat, n_ctx):
    t = jnp.arange(n_lat, dtype=jnp.int32)
    rows = (t // GRID_W).astype(F32)
    cols = (t % GRID_W).astype(F32)
    axis_dim = HEAD_DIM // 2
    inv_freq = ROPE_THETA ** (-jnp.arange(0, axis_dim, 2, dtype=F32) / axis_dim)
    ang_r = rows[:, None] * inv_freq[None, :]
    ang_c = cols[:, None] * inv_freq[None, :]
    cos = jnp.concatenate([jnp.cos(ang_r)] * 2 + [jnp.cos(ang_c)] * 2, axis=-1)
    sin = jnp.concatenate([-jnp.sin(ang_r), jnp.sin(ang_r), -jnp.sin(ang_c), jnp.sin(ang_c)], axis=-1)
    cos = jnp.concatenate([jnp.ones((n_ctx, HEAD_DIM), F32), cos], axis=0)
    sin = jnp.concatenate([jnp.zeros((n_ctx, HEAD_DIM), F32), sin], axis=0)
    return jnp.tile(cos, (1, 2)), jnp.tile(sin, (1, 2))


def kernel(x, c, ctx, c_ctx, w_mod, b_mod, g_norm1, w_in, att_q_gain, att_k_gain, na_q_gain, na_k_gain, na_rel_bias, w_fourier, ssm_lam_re, ssm_lam_im, ssm_log_dt, ssm_b_re, ssm_b_im, ssm_c_re, ssm_c_im, ssm_d, w_glu, b_glu, g_group, w_out, g_norm2, w_ff1, w_ff3, w_ff2):
    n_batch, n_lat, d = x.shape
    n_ctx = ctx.shape[1]
    n_tok = n_lat + n_ctx
    assert d == D_MODEL and n_ctx == TILE and n_lat % TILE == 0 and n_lat % GRID_W == 0
    assert n_batch % 8 == 0 and n_tok % S5_STEPS == 0

    mod_rows = -(-(n_batch + 1) // 8) * 8
    cc = jnp.concatenate([c, c_ctx[None, :], jnp.zeros((mod_rows - n_batch - 1, d), F32)], axis=0)
    mod = _modulation(cc, w_mod, b_mod.reshape(DEPTH, 1, N_MOD * d)).reshape(DEPTH, mod_rows, 1, N_MOD * d)

    cos_t, sin_t = _rope_tables(n_lat, n_ctx)
    cs_bd = _head_dft_matrix()
    dft = _dft_matrix(n_lat)
    dft_ctx = _dft_matrix(n_ctx)
    in_cols = np.concatenate([_Q_PERM, np.arange(GROUP_W, IN_W)])
    out_rows = np.concatenate([_Q_PERM, np.arange(GROUP_W, D_MODEL)])
    scale = HEAD_DIM ** -0.5 * LOG2E
    tile2 = lambda g, s=1.0: jnp.tile(g.astype(F32) * s, 2)[None, :]
    row = lambda v: v.astype(F32)[None, :]

    xs = (x, ctx)
    for l in range(DEPTH):
        need_ctx = l < DEPTH - 1
        qkv, u = _inproj(xs, mod[l], row(g_norm1[l]), w_in[l][:, in_cols].astype(BF16), cos_t, sin_t,
                         tile2(att_q_gain[l], scale), tile2(att_k_gain[l]),
                         tile2(na_q_gain[l], scale), tile2(na_k_gain[l]),
                         n_batch=n_batch, n_tok=n_tok)
        o_a = _gqa(_logits_bounded(att_q_gain[l], att_k_gain[l]), qkv, need_ctx=need_ctx, n_ctx=n_ctx)
        o_d = _na(_logits_bounded(na_q_gain[l], na_k_gain[l], jnp.max(jnp.abs(na_rel_bias[l]))), qkv,
                  _na_bias_table(na_rel_bias[l], n_lat // GRID_W), need_ctx=need_ctx, n_ctx=n_ctx)
        o_f = _fourier(qkv, cs_bd, dft, dft_ctx, w_fourier[l].astype(BF16), need_ctx=need_ctx, n_ctx=n_ctx)
        a_tab, b_dense, c_dense = _s5_tables(ssm_lam_re[l], ssm_lam_im[l], ssm_log_dt[l], ssm_b_re[l],
                                             ssm_b_im[l], ssm_c_re[l], ssm_c_im[l])
        ys = _s5(u, a_tab, b_dense, c_dense, n_ctx=n_ctx)
        x1 = _mixout(xs, mod[l], o_a, o_d, o_f, u, ys, row(ssm_d[l]), w_glu[l].astype(BF16), row(b_glu[l]),
                     row(g_group[l][out_rows]), w_out[l][out_rows, :].astype(BF16),
                     need_ctx=need_ctx, n_ctx=n_ctx)
        x2 = _ffn(x1, mod[l], row(g_norm2[l]), w_ff1[l].astype(BF16), w_ff3[l].astype(BF16),
                  w_ff2[l].astype(BF16), has_ctx=need_ctx)
        xs = (x2,)
    return xs[0]
```

```python
import functools
import math

import numpy as np
import jax
import jax.numpy as jnp
from jax import lax
from jax.experimental import pallas as pl
from jax.experimental.pallas import tpu as pltpu

F32 = jnp.float32
BF16 = jnp.bfloat16

D_MODEL = 1024
DEPTH = 2
GRID_W = 64
HEAD_DIM = 64
N_GROUPS = 4
GROUP_W = D_MODEL // N_GROUPS
WIN_H = 8
WIN_W = 16
NA_HEADS = GROUP_W // HEAD_DIM
FNO_HEAD_W = 64
SSM_GROUP = 16
SSM_GROUPS = GROUP_W // SSM_GROUP
SSM_STATE = 64
SSM_W = SSM_GROUPS * SSM_STATE
ROPE_THETA = 10000.0
D_FF = ((8 * D_MODEL // 3 + 255) // 256) * 256
N_MOD = 6
EPS = 1e-6
IN_W = 7 * GROUP_W
QKV_W = 6 * GROUP_W

LANES = 128
TILE = 256
BAND_ROWS = TILE // GRID_W
S5_STEPS = 128
S5_SUB = 32
FF_CHUNK = D_FF // 2
FFN_TILES = (768, 512, 256)
VMEM_LIMIT_BYTES = 56 * 2**20
NEG_BIG = -1e30
LOG2E = 1.0 / math.log(2.0)
EXP2_SAFE = 60.0

_Q_ORDER = (0, 2, 1, 3)


def _params(sem):
    return pltpu.CompilerParams(dimension_semantics=sem, vmem_limit_bytes=VMEM_LIMIT_BYTES)


def _sds(shape, dtype):
    return jax.ShapeDtypeStruct(shape, dtype)


def _mod_kernel(c_ref, w_ref, b_ref, o_ref):
    c = c_ref[...]
    s = c * jax.nn.sigmoid(c)
    o_ref[0] = jnp.dot(s, w_ref[0], preferred_element_type=F32, precision=lax.Precision.HIGHEST) + b_ref[0]


def _modulation(cc, w_mod, b_mod):
    rows, d = cc.shape
    n_blk = 4
    bw = N_MOD * d // n_blk
    return pl.pallas_call(
        _mod_kernel,
        grid=(DEPTH, n_blk),
        in_specs=[
            pl.BlockSpec((rows, d), lambda l, j: (0, 0)),
            pl.BlockSpec((1, d, bw), lambda l, j: (l, 0, j)),
            pl.BlockSpec((1, 1, bw), lambda l, j: (l, 0, j)),
        ],
        out_specs=pl.BlockSpec((1, rows, bw), lambda l, j: (l, 0, j)),
        out_shape=_sds((DEPTH, rows, N_MOD * d), F32),
        compiler_params=_params(("arbitrary", "arbitrary")),
        name="modulation",
    )(cc, w_mod, b_mod)


def _rms_rows(x):
    return x * lax.rsqrt(jnp.mean(x * x, axis=-1, keepdims=True) + EPS)


def _mod_slices(mod_ref, first):
    m = mod_ref[0]
    return [m[:, (first + k) * D_MODEL:(first + k + 1) * D_MODEL] for k in range(3)]


def _head_norm(t, gain):
    lo = lax.broadcasted_iota(jnp.int32, t.shape, 1) < HEAD_DIM
    sq = t * t
    s_lo = jnp.sum(jnp.where(lo, sq, 0.0), axis=-1, keepdims=True)
    s_hi = jnp.sum(jnp.where(lo, 0.0, sq), axis=-1, keepdims=True)
    ms = jnp.where(lo, s_lo, s_hi) * (1.0 / HEAD_DIM)
    return t * lax.rsqrt(ms + EPS) * gain


def _rope(t, cos, sin):
    first = (lax.broadcasted_iota(jnp.int32, t.shape, 1) & 16) == 0
    partner = jnp.where(first, pltpu.roll(t, LANES - 16, axis=1), pltpu.roll(t, 16, axis=1))
    return t * cos + partner * sin


def _softmax_pv(s, v, bounded):
    p = jnp.exp2(s if bounded else s - jnp.max(s, axis=-1, keepdims=True))
    l = jnp.sum(p, axis=-1, keepdims=True)
    return jnp.dot(p.astype(BF16), v, preferred_element_type=F32) / l


def _when_bounded(flag_ref, body):
    bounded = flag_ref[0] == 1
    pl.when(bounded)(lambda: body(True))
    pl.when(jnp.logical_not(bounded))(lambda: body(False))


def _logits_bounded(gq, gk, extra=0.0):
    bound = math.sqrt(HEAD_DIM) * jnp.max(jnp.abs(gq)) * jnp.max(jnp.abs(gk)) + extra
    return (bound * LOG2E <= EXP2_SAFE).astype(jnp.int32).reshape(1)


def _split_heads(q):
    lo = lax.broadcasted_iota(jnp.int32, q.shape, 1) < HEAD_DIM
    zero = jnp.zeros_like(q)
    return jnp.concatenate([jnp.where(lo, q, zero), jnp.where(lo, zero, q)], axis=0)


def _merge_heads(o, rows):
    lo = lax.broadcasted_iota(jnp.int32, (rows, LANES), 1) < HEAD_DIM
    return jnp.where(lo, o[:rows], o[rows:])


def _nt_dot(a, b):
    return lax.dot_general(a, b, (((1,), (1,)), ((), ())), preferred_element_type=F32)


def _inproj_kernel(*refs, split_input):
    if split_input:
        x_ref, ctx_ref, *refs = refs
    else:
        x_ref, *refs = refs
    (mod_ref, g1_ref, w_ref, cos_ref, sin_ref, gq_ref, gk_ref, nq_ref, nk_ref, qkv_ref, u_ref) = refs
    x = x_ref[0]
    if split_input:
        x = jnp.where(pl.program_id(1) == 0, ctx_ref[0], x)
    sh1, sc1, _ = _mod_slices(mod_ref, 0)
    h = (_rms_rows(x) * g1_ref[...]) * (1.0 + sc1) + sh1
    p = jnp.dot(h.astype(BF16), w_ref[...], preferred_element_type=F32)
    cos = cos_ref[...]
    sin = sin_ref[...]

    def tile(j):
        return p[:, j * LANES:(j + 1) * LANES]

    out = [
        _rope(_head_norm(tile(0), gq_ref[...]), cos, sin),
        _rope(_head_norm(tile(1), gq_ref[...]), cos, sin),
        _rope(_head_norm(tile(2), gk_ref[...]), cos, sin),
        tile(3),
        _head_norm(tile(4), nq_ref[...]),
        _head_norm(tile(5), nq_ref[...]),
        _head_norm(tile(6), nk_ref[...]),
        _head_norm(tile(7), nk_ref[...]),
        tile(8), tile(9), tile(10), tile(11),
    ]
    for j, t in enumerate(out):
        qkv_ref[0, :, j * LANES:(j + 1) * LANES] = t.astype(BF16)
    u_ref[0] = p[:, QKV_W:]


def _inproj(xs, mod, g1, w_in, cos_t, sin_t, gq, gk, nq, nk, *, n_batch, n_tok):
    split_input = len(xs) == 2
    n_tiles = n_tok // TILE
    tok = (1, TILE, D_MODEL)
    if split_input:
        x_specs = [pl.BlockSpec(tok, lambda b, i: (b, jnp.maximum(i - 1, 0), 0)),
                   pl.BlockSpec(tok, lambda b, i: (b, 0, 0))]
    else:
        x_specs = [pl.BlockSpec(tok, lambda b, i: (b, i, 0))]
    vec = lambda w: pl.BlockSpec((1, w), lambda b, i: (0, 0))
    return pl.pallas_call(
        functools.partial(_inproj_kernel, split_input=split_input),
        grid=(n_batch, n_tiles),
        in_specs=x_specs + [
            pl.BlockSpec((1, 1, N_MOD * D_MODEL), lambda b, i: (jnp.where(i == 0, n_batch, b), 0, 0)),
            vec(D_MODEL),
            pl.BlockSpec((D_MODEL, IN_W), lambda b, i: (0, 0)),
            pl.BlockSpec((TILE, LANES), lambda b, i: (i, 0)),
            pl.BlockSpec((TILE, LANES), lambda b, i: (i, 0)),
            vec(LANES), vec(LANES), vec(LANES), vec(LANES),
        ],
        out_specs=[pl.BlockSpec((1, TILE, QKV_W), lambda b, i: (b, i, 0)),
                   pl.BlockSpec((1, TILE, GROUP_W), lambda b, i: (b, i, 0))],
        out_shape=[_sds((n_batch, n_tok, QKV_W), BF16), _sds((n_batch, n_tok, GROUP_W), F32)],
        compiler_params=_params(("arbitrary", "arbitrary")),
        name="inproj",
    )(*xs, mod, g1, w_in, cos_t, sin_t, gq, gk, nq, nk)


def _gqa_kernel(flag_ref, qa_ref, qb_ref, k_ref, v_ref, o_ref, *, n_ctx, tile_off):
    def attend(n_keys, bounded):
        k = k_ref[0, :n_keys, :]
        v = v_ref[0, :n_keys, :]
        outs = []
        for q_ref in (qa_ref, qb_ref):
            q2 = _split_heads(q_ref[0])
            outs.append(_merge_heads(_softmax_pv(_nt_dot(q2, k), v, bounded), TILE))
        o_ref[0] = jnp.concatenate(outs, axis=-1).astype(o_ref.dtype)

    if tile_off == 0:
        is_ctx = pl.program_id(1) == 0
        pl.when(is_ctx)(lambda: _when_bounded(flag_ref, functools.partial(attend, n_ctx)))
        pl.when(jnp.logical_not(is_ctx))(lambda: _when_bounded(flag_ref, functools.partial(attend, k_ref.shape[1])))
    else:
        _when_bounded(flag_ref, functools.partial(attend, k_ref.shape[1]))


_SMEM_SPEC = pl.BlockSpec(memory_space=pltpu.SMEM)


def _gqa(bounded, qkv, *, need_ctx, n_ctx):
    n_batch, n_tok, _ = qkv.shape
    off = 0 if need_ctx else n_ctx // TILE
    q_spec = lambda col: pl.BlockSpec((1, TILE, LANES), lambda b, i: (b, i + off, col))
    kv_spec = lambda col: pl.BlockSpec((1, n_tok, LANES), lambda b, i: (b, 0, col))
    return pl.pallas_call(
        functools.partial(_gqa_kernel, n_ctx=n_ctx, tile_off=off),
        grid=(n_batch, n_tok // TILE - off),
        in_specs=[_SMEM_SPEC, q_spec(0), q_spec(1), kv_spec(2), kv_spec(3)],
        out_specs=pl.BlockSpec((1, TILE, GROUP_W), lambda b, i: (b, i, 0)),
        out_shape=_sds((n_batch, n_tok - off * TILE, GROUP_W), BF16),
        compiler_params=_params(("arbitrary", "arbitrary")),
        name="gqa",
    )(bounded, qkv, qkv, qkv, qkv)


def _na_kernel(flag_ref, q_ref, kp_ref, kc_ref, kn_ref, kx_ref, vp_ref, vc_ref, vn_ref, vx_ref, tbl_ref, o_ref, *, tile_off):
    def attend(local, bounded):
        outs = []
        for pair in range(2):
            cols = slice(pair * LANES, (pair + 1) * LANES)
            q2 = _split_heads(q_ref[0, :, cols])
            if local:
                k = jnp.concatenate([r[0, :, cols] for r in (kp_ref, kc_ref, kn_ref, kx_ref)], axis=0)
                v = jnp.concatenate([r[0, :, cols] for r in (vp_ref, vc_ref, vn_ref, vx_ref)], axis=0)
                bias = jnp.concatenate([tbl_ref[0, 2 * pair], tbl_ref[0, 2 * pair + 1]], axis=0)
                s = _nt_dot(q2, k)
                s = jnp.concatenate([s[:, :3 * TILE] + bias, s[:, 3 * TILE:]], axis=-1)
            else:
                k = kx_ref[0, :, cols]
                v = vx_ref[0, :, cols]
                s = _nt_dot(q2, k)
            outs.append(_merge_heads(_softmax_pv(s, v, bounded), TILE))
        o_ref[0] = jnp.concatenate(outs, axis=-1).astype(o_ref.dtype)

    if tile_off == 0:
        is_ctx = pl.program_id(1) == 0
        pl.when(is_ctx)(lambda: _when_bounded(flag_ref, functools.partial(attend, False)))
        pl.when(jnp.logical_not(is_ctx))(lambda: _when_bounded(flag_ref, functools.partial(attend, True)))
    else:
        _when_bounded(flag_ref, functools.partial(attend, True))


def _na(bounded, qkv, table, *, need_ctx, n_ctx):
    n_batch, n_tok, _ = qkv.shape
    assert n_ctx == TILE
    n_tiles = n_tok // TILE
    off = 0 if need_ctx else 1
    blk = (1, TILE, GROUP_W)
    cur = lambda col: pl.BlockSpec(blk, lambda b, i: (b, i + off, col))
    prev = lambda col: pl.BlockSpec(blk, lambda b, i: (b, jnp.maximum(i + off - 1, 1), col))
    nxt = lambda col: pl.BlockSpec(blk, lambda b, i: (b, jnp.minimum(i + off + 1, n_tiles - 1), col))
    ctx = lambda col: pl.BlockSpec(blk, lambda b, i: (b, 0, col))

    def cls(b, i):
        t = i + off
        return (jnp.where(t == n_tiles - 1, 2, jnp.where(t <= 1, 0, 1)), 0, 0, 0)

    return pl.pallas_call(
        functools.partial(_na_kernel, tile_off=off),
        grid=(n_batch, n_tiles - off),
        in_specs=[_SMEM_SPEC, cur(2), prev(3), cur(3), nxt(3), ctx(3), prev(4), cur(4), nxt(4), ctx(4),
                  pl.BlockSpec((1, NA_HEADS, TILE, 3 * TILE), cls)],
        out_specs=pl.BlockSpec(blk, lambda b, i: (b, i, 0)),
        out_shape=_sds((n_batch, n_tok - off * TILE, GROUP_W), BF16),
        compiler_params=_params(("arbitrary", "arbitrary")),
        name="na",
    )(bounded, *([qkv] * 9), table)


def _na_bias_table(rel_bias, rows):
    n_bands = rows // BAND_ROWS
    assert rows >= WIN_H and rows % BAND_ROWS == 0
    n_dr, n_dc = 2 * WIN_H - 1, 2 * WIN_W - 1
    w = GRID_W
    shift = w - WIN_W
    vp = jnp.pad(rel_bias.astype(F32) * LOG2E, ((0, 0), (0, 0), (shift, 2 * w - shift - n_dc)))
    skew = jnp.broadcast_to(vp[:, :, None, :], (NA_HEADS, n_dr, w, 2 * w)).reshape(NA_HEADS, n_dr, 2 * w * w)
    col = skew[:, :, :w * (2 * w - 1)].reshape(NA_HEADS, n_dr, w, 2 * w - 1)[:, :, :, w - 1:]
    qc = np.arange(w)
    cs = np.clip(qc - WIN_W // 2, 0, w - WIN_W)
    col_ok = (qc[None, :] >= cs[:, None]) & (qc[None, :] < cs[:, None] + WIN_W)
    col = jnp.where(col_ok[None, None], col, NEG_BIG)
    masked = jnp.full((NA_HEADS, w, w), NEG_BIG, F32)
    classes = []
    for band in (0, min(1, n_bands - 1), n_bands - 1):
        q_rows = []
        for i in range(BAND_ROWS):
            r = BAND_ROWS * band + i
            rs = min(max(r - WIN_H // 2, 0), rows - WIN_H)
            blocks = []
            for kk in range(3 * BAND_ROWS):
                kr = BAND_ROWS * (band - 1) + kk
                ok = 0 <= kr < rows and rs <= kr < rs + WIN_H
                blocks.append(col[:, kr - r + WIN_H - 1] if ok else masked)
            q_rows.append(jnp.concatenate(blocks, axis=-1))
        classes.append(jnp.concatenate(q_rows, axis=-2))
    return jnp.stack(classes)


def _fourier_kernel(x_ref, cs_ref, dft_ref, dftc_ref, w_ref, o_ref, x12_ref, *, n_ctx, n_lat, tile_off):
    t = pl.program_id(1) + tile_off

    def head_dft(x):
        x12 = jnp.dot(x, cs_ref[...], preferred_element_type=F32).astype(BF16)
        return jnp.concatenate([x12[:, :GROUP_W], x12[:, GROUP_W:]], axis=0)

    def finish(y, n):
        y = y * (1.0 / math.sqrt(n * FNO_HEAD_W))
        o_ref[0] = jnp.dot(y.astype(BF16), w_ref[...], preferred_element_type=F32).astype(o_ref.dtype)

    if tile_off == 0:
        @pl.when(t == 0)
        def _():
            x12 = head_dft(x_ref[0, :n_ctx, :])
            finish(jnp.dot(dftc_ref[...], x12, preferred_element_type=F32), n_ctx)

    @pl.when(t == n_ctx // TILE)
    def _():
        x12_ref[...] = head_dft(x_ref[0, n_ctx:, :])

    @pl.when(t >= n_ctx // TILE)
    def _():
        row0 = pl.multiple_of((t - n_ctx // TILE) * TILE, TILE)
        finish(jnp.dot(dft_ref[pl.ds(row0, TILE), :], x12_ref[...], preferred_element_type=F32), n_lat)


def _fourier(qkv, cs_bd, dft, dft_ctx, w_f, *, need_ctx, n_ctx):
    n_batch, n_tok, _ = qkv.shape
    n_lat = n_tok - n_ctx
    off = 0 if need_ctx else n_ctx // TILE
    const = lambda shape: pl.BlockSpec(shape, lambda b, i: (0,) * len(shape))
    return pl.pallas_call(
        functools.partial(_fourier_kernel, n_ctx=n_ctx, n_lat=n_lat, tile_off=off),
        grid=(n_batch, n_tok // TILE - off),
        in_specs=[pl.BlockSpec((1, n_tok, GROUP_W), lambda b, i: (b, 0, 5)),
                  const(cs_bd.shape), const(dft.shape), const(dft_ctx.shape), const(w_f.shape)],
        out_specs=pl.BlockSpec((1, TILE, GROUP_W), lambda b, i: (b, i, 0)),
        out_shape=_sds((n_batch, n_tok - off * TILE, GROUP_W), BF16),
        scratch_shapes=[pltpu.VMEM((2 * n_lat, GROUP_W), BF16)],
        compiler_params=_params(("arbitrary", "arbitrary")),
        name="fourier",
    )(qkv, cs_bd, dft, dft_ctx, w_f)


def _dft_matrix(n):
    f = 32
    assert n % f == 0
    k = lax.broadcasted_iota(jnp.int32, (n, 1), 0)
    m1 = lax.broadcasted_iota(jnp.int32, (1, n // f), 1)
    m0 = lax.broadcasted_iota(jnp.int32, (1, f), 1)
    ang_a = ((k * (m1 * f)) % n).astype(F32) * (2.0 * math.pi / n)
    ang_b = ((k * m0) % n).astype(F32) * (2.0 * math.pi / n)
    ca, sa = jnp.cos(ang_a)[:, :, None], jnp.sin(ang_a)[:, :, None]
    cb, sb = jnp.cos(ang_b)[:, None, :], jnp.sin(ang_b)[:, None, :]
    cos = (ca * cb - sa * sb).reshape(n, n)
    sin = (sa * cb + ca * sb).reshape(n, n)
    return jnp.concatenate([cos, -sin], axis=1).astype(BF16)


def _head_dft_matrix():
    a = np.arange(GROUP_W)
    same = (a[:, None] // FNO_HEAD_W) == (a[None, :] // FNO_HEAD_W)
    ang = 2.0 * np.pi * (((a[:, None] % FNO_HEAD_W) * (a[None, :] % FNO_HEAD_W)) % FNO_HEAD_W) / FNO_HEAD_W
    m = np.concatenate([np.where(same, np.cos(ang), 0.0), np.where(same, np.sin(ang), 0.0)], axis=1)
    return jnp.asarray(m, F32).astype(BF16)


def _s5_kernel(u_ref, a_ref, bd_ref, cd_ref, y_ref, bu_ref, hb_ref, h_ref, *, n_batch, reverse):
    @pl.when(pl.program_id(0) == 0)
    def _():
        h_ref[...] = jnp.zeros_like(h_ref)

    sub_rows = S5_SUB * n_batch
    chunk = 4 * LANES
    u = pltpu.einshape("btc->tbc", u_ref[...]).reshape(S5_STEPS * n_batch, GROUP_W).astype(BF16)

    def project(q):
        r = slice(q * sub_rows, (q + 1) * sub_rows)
        bu_ref[r, :] = jnp.dot(u[r], bd_ref[0], preferred_element_type=F32)

    def scan(q):
        steps = range(S5_SUB - 1, -1, -1) if reverse else range(S5_SUB)
        for sc in range(SSM_W // chunk):
            re = slice(sc * chunk, (sc + 1) * chunk)
            im = slice(SSM_W + sc * chunk, SSM_W + (sc + 1) * chunk)
            a_re = jnp.broadcast_to(a_ref[0, :, re], (n_batch, chunk))
            a_im = jnp.broadcast_to(a_ref[0, :, im], (n_batch, chunk))
            h_re, h_im = h_ref[:, re], h_ref[:, im]
            for t in steps:
                r = slice(q * sub_rows + t * n_batch, q * sub_rows + (t + 1) * n_batch)
                h_re, h_im = (a_re * h_re - a_im * h_im + bu_ref[r, re],
                              a_re * h_im + a_im * h_re + bu_ref[r, im])
                hb_ref[r, re] = h_re.astype(BF16)
                hb_ref[r, im] = h_im.astype(BF16)
            h_ref[:, re] = h_re
            h_ref[:, im] = h_im

    def readout(q):
        y = jnp.dot(hb_ref[q * sub_rows:(q + 1) * sub_rows, :], cd_ref[0], preferred_element_type=F32)
        y = pltpu.einshape("tbc->btc", y.reshape(S5_SUB, n_batch, GROUP_W))
        y_ref[:, q * S5_SUB:(q + 1) * S5_SUB, :] = y.astype(y_ref.dtype)

    order = list(range(S5_STEPS // S5_SUB))
    if reverse:
        order.reverse()
    project(order[0])
    for j, q in enumerate(order):
        if j + 1 < len(order):
            project(order[j + 1])
        scan(q)
        if j > 0:
            readout(order[j - 1])
    readout(order[-1])


def _s5(u, a_tab, b_dense, c_dense, *, n_ctx, reverse):
    n_batch, n_tok, _ = u.shape
    n_blk = n_tok // S5_STEPS
    n_cblk = n_ctx // S5_STEPS
    d = int(reverse)
    if reverse:
        blk = lambda s: jnp.where(s < n_cblk, n_cblk - 1 - s, n_blk - 1 - (s - n_cblk))
    else:
        blk = lambda s: s
    return pl.pallas_call(
        functools.partial(_s5_kernel, n_batch=n_batch, reverse=reverse),
        grid=(n_blk,),
        in_specs=[pl.BlockSpec((n_batch, S5_STEPS, GROUP_W), lambda s: (0, blk(s), 0)),
                  pl.BlockSpec((1, 1, 2 * SSM_W), lambda s: (d, 0, 0)),
                  pl.BlockSpec((1, GROUP_W, 2 * SSM_W), lambda s: (d, 0, 0)),
                  pl.BlockSpec((1, 2 * SSM_W, GROUP_W), lambda s: (d, 0, 0))],
        out_specs=pl.BlockSpec((n_batch, S5_STEPS, GROUP_W), lambda s: (0, blk(s), 0)),
        out_shape=_sds((n_batch, n_tok, GROUP_W), BF16),
        scratch_shapes=[pltpu.VMEM((S5_STEPS * n_batch, 2 * SSM_W), F32),
                        pltpu.VMEM((S5_STEPS * n_batch, 2 * SSM_W), BF16),
                        pltpu.VMEM((n_batch, 2 * SSM_W), F32)],
        compiler_params=_params(("arbitrary",)),
        name="s5",
    )(u, a_tab, b_dense, c_dense)


def _s5_tables(lam_re, lam_im, log_dt, b_re, b_im, c_re, c_im):
    lam = lax.complex(lam_re.astype(F32), lam_im.astype(F32))
    dt = jnp.exp(log_dt.astype(F32))[..., None]
    a_bar = jnp.exp(lam * dt)
    b_bar = ((a_bar - 1.0) / lam)[..., None] * lax.complex(b_re.astype(F32), b_im.astype(F32))
    eye = jnp.eye(SSM_GROUPS, dtype=F32)

    def embed_b(m):
        m = jnp.transpose(m, (0, 1, 3, 2))[:, :, :, None, :] * eye[None, :, None, :, None]
        return m.reshape(2, GROUP_W, SSM_W)

    def embed_c(m):
        m = jnp.transpose(m, (0, 1, 3, 2))[:, :, :, None, :] * eye[None, :, None, :, None]
        return m.reshape(2, SSM_W, GROUP_W)

    a_tab = jnp.concatenate([jnp.real(a_bar).reshape(2, 1, SSM_W), jnp.imag(a_bar).reshape(2, 1, SSM_W)], axis=-1)
    b_dense = jnp.concatenate([embed_b(jnp.real(b_bar)), embed_b(jnp.imag(b_bar))], axis=-1).astype(BF16)
    c_dense = jnp.concatenate([embed_c(c_re.astype(F32)), -embed_c(c_im.astype(F32))], axis=1).astype(BF16)
    return a_tab, b_dense, c_dense


def _gelu_tanh(x):
    return x * (0.5 * (1.0 + jnp.tanh(math.sqrt(2.0 / math.pi) * (x + 0.044715 * (x * x * x)))))


def _mixout_kernel(*refs, split_input):
    if split_input:
        x_ref, ctx_ref, *refs = refs
    else:
        x_ref, *refs = refs
    (mod_ref, oa_ref, od_ref, of_ref, u_ref, yf_ref, yb_ref, d_ref, wg_ref, bg_ref, gg_ref, wo_ref, o_ref) = refs
    x = x_ref[0]
    if split_input:
        x = jnp.where(pl.program_id(1) == 0, ctx_ref[0], x)
    _, _, ga1 = _mod_slices(mod_ref, 0)
    y = u_ref[0] * d_ref[...] + yf_ref[0].astype(F32) + yb_ref[0].astype(F32)
    g = _gelu_tanh(y)
    z = jnp.dot(g.astype(BF16), wg_ref[...], preferred_element_type=F32) + bg_ref[...]
    o_s = g * jax.nn.sigmoid(z)
    parts = [oa_ref[0].astype(F32), od_ref[0].astype(F32), of_ref[0].astype(F32), o_s]
    normed = [(_rms_rows(p) * gg_ref[:, j * GROUP_W:(j + 1) * GROUP_W]).astype(BF16) for j, p in enumerate(parts)]
    r = jnp.dot(jnp.concatenate(normed, axis=-1), wo_ref[...], preferred_element_type=F32)
    o_ref[0] = x + ga1 * r


def _mixout(xs, mod, o_a, o_d, o_f, u, y_fwd, y_bwd, d_skip, w_glu, b_glu, g_group, w_out, *, need_ctx, n_ctx):
    split_input = len(xs) == 2
    n_batch, n_tok, _ = u.shape
    off = 0 if need_ctx else n_ctx // TILE
    n_out = n_tok - off * TILE
    tok = (1, TILE, D_MODEL)
    if split_input:
        assert need_ctx
        x_specs = [pl.BlockSpec(tok, lambda b, i: (b, jnp.maximum(i - 1, 0), 0)),
                   pl.BlockSpec(tok, lambda b, i: (b, 0, 0))]
    else:
        x_specs = [pl.BlockSpec(tok, lambda b, i: (b, i + off, 0))]
    grp = pl.BlockSpec((1, TILE, GROUP_W), lambda b, i: (b, i + off, 0))
    mix = pl.BlockSpec((1, TILE, GROUP_W), lambda b, i: (b, i, 0))
    const = lambda shape: pl.BlockSpec(shape, lambda b, i: (0,) * len(shape))
    return pl.pallas_call(
        functools.partial(_mixout_kernel, split_input=split_input),
        grid=(n_batch, n_out // TILE),
        in_specs=x_specs + [
            pl.BlockSpec((1, 1, N_MOD * D_MODEL), lambda b, i: (jnp.where(i + off == 0, n_batch, b), 0, 0)),
            mix, mix, mix, grp, grp, grp,
            const((1, GROUP_W)), const((GROUP_W, GROUP_W)), const((1, GROUP_W)),
            const((1, D_MODEL)), const((D_MODEL, D_MODEL)),
        ],
        out_specs=pl.BlockSpec(tok, lambda b, i: (b, i, 0)),
        out_shape=_sds((n_batch, n_out, D_MODEL), F32),
        compiler_params=_params(("arbitrary", "arbitrary")),
        name="mixout",
    )(*xs, mod, o_a, o_d, o_f, u, y_fwd, y_bwd, d_skip, w_glu, b_glu, g_group, w_out)


def _ffn_kernel(x_ref, mod_ref, modc_ref, g2_ref, w1_ref, w3_ref, w2_ref, o_ref, *, n_ctx_rows):
    x = x_ref[0]
    mods = _mod_slices(mod_ref, 3)
    if n_ctx_rows:
        rows = lax.broadcasted_iota(jnp.int32, (x.shape[0], 1), 0)
        is_ctx = jnp.logical_and(rows < n_ctx_rows, pl.program_id(1) == 0)
        mods = [jnp.where(is_ctx, mc, ml) for mc, ml in zip(_mod_slices(modc_ref, 3), mods)]
    sh2, sc2, ga2 = mods
    h = ((_rms_rows(x) * g2_ref[...]) * (1.0 + sc2) + sh2).astype(BF16)
    acc = jnp.zeros(x.shape, F32)
    for c in range(D_FF // FF_CHUNK):
        cols = slice(c * FF_CHUNK, (c + 1) * FF_CHUNK)
        a = jnp.dot(h, w1_ref[:, cols], preferred_element_type=F32)
        b = jnp.dot(h, w3_ref[:, cols], preferred_element_type=F32)
        t = (a * jax.nn.sigmoid(a)) * b
        acc = acc + jnp.dot(t.astype(BF16), w2_ref[cols, :], preferred_element_type=F32)
    o_ref[0] = x + ga2 * acc


def _ffn(x, mod, g2, w1, w3, w2, *, n_ctx_rows):
    n_batch, n_tok, _ = x.shape
    tile = next(t for t in FFN_TILES if n_tok % t == 0 and n_ctx_rows <= t)
    tok = pl.BlockSpec((1, tile, D_MODEL), lambda b, i: (b, i, 0))
    const = lambda shape: pl.BlockSpec(shape, lambda b, i: (0,) * len(shape), pipeline_mode=pl.Buffered(1))
    mod_blk = (1, 1, N_MOD * D_MODEL)
    return pl.pallas_call(
        functools.partial(_ffn_kernel, n_ctx_rows=n_ctx_rows),
        grid=(n_batch, n_tok // tile),
        in_specs=[tok, pl.BlockSpec(mod_blk, lambda b, i: (b, 0, 0)),
                  pl.BlockSpec(mod_blk, lambda b, i: (n_batch, 0, 0)), const((1, D_MODEL)),
                  const(w1.shape), const(w3.shape), const(w2.shape)],
        out_specs=tok,
        out_shape=_sds(x.shape, F32),
        compiler_params=_params(("arbitrary", "arbitrary")),
        name="ffn",
    )(x, mod, mod, g2, w1, w3, w2)


def _rope_tables(n_lat, n_ctx):
    t = jnp.arange(n_lat, dtype=jnp.int32)
    rows = (t // GRID_W).astype(F32)
    cols = (t % GRID_W).astype(F32)
    axis_dim = HEAD_DIM // 2
    inv_freq = ROPE_THETA ** (-jnp.arange(0, axis_dim, 2, dtype=F32) / axis_dim)
    ang_r = rows[:, None] * inv_freq[None, :]
    ang_c = cols[:, None] * inv_freq[None, :]
    cos = jnp.concatenate([jnp.cos(ang_r)] * 2 + [jnp.cos(ang_c)] * 2, axis=-1)
    sin = jnp.concatenate([-jnp.sin(ang_r), jnp.sin(ang_r), -jnp.sin(ang_c), jnp.sin(ang_c)], axis=-1)
    cos = jnp.concatenate([jnp.ones((n_ctx, HEAD_DIM), F32), cos], axis=0)
    sin = jnp.concatenate([jnp.zeros((n_ctx, HEAD_DIM), F32), sin], axis=0)
    return jnp.tile(cos, (1, 2)), jnp.tile(sin, (1, 2))


def kernel(x, c, ctx, c_ctx, w_mod, b_mod, g_norm1, w_in, att_q_gain, att_k_gain, na_q_gain, na_k_gain, na_rel_bias, w_fourier, ssm_lam_re, ssm_lam_im, ssm_log_dt, ssm_b_re, ssm_b_im, ssm_c_re, ssm_c_im, ssm_d, w_glu, b_glu, g_group, w_out, g_norm2, w_ff1, w_ff3, w_ff2):
    n_batch, n_lat, d = x.shape
    n_ctx = ctx.shape[1]
    n_tok = n_lat + n_ctx
    assert d == D_MODEL and n_ctx == TILE and n_lat % TILE == 0 and n_lat % GRID_W == 0
    assert n_batch % 8 == 0 and n_tok % S5_STEPS == 0

    mod_rows = -(-(n_batch + 1) // 8) * 8
    cc = jnp.concatenate([c, c_ctx[None, :], jnp.zeros((mod_rows - n_batch - 1, d), F32)], axis=0)
    mod = _modulation(cc, w_mod, b_mod.reshape(DEPTH, 1, N_MOD * d)).reshape(DEPTH, mod_rows, 1, N_MOD * d)

    cos_t, sin_t = _rope_tables(n_lat, n_ctx)
    cs_bd = _head_dft_matrix()
    dft = _dft_matrix(n_lat)
    dft_ctx = _dft_matrix(n_ctx)
    def q_perm(a, axis):
        heads = [lax.slice_in_dim(a, h * HEAD_DIM, (h + 1) * HEAD_DIM, axis=axis) for h in _Q_ORDER]
        return jnp.concatenate(heads + [lax.slice_in_dim(a, GROUP_W, a.shape[axis], axis=axis)], axis=axis)

    scale = HEAD_DIM ** -0.5 * LOG2E
    tile2 = lambda g, s=1.0: jnp.tile(g.astype(F32) * s, 2)[None, :]
    row = lambda v: v.astype(F32)[None, :]

    xs = (x, ctx)
    for l in range(DEPTH):
        need_ctx = l < DEPTH - 1
        qkv, u = _inproj(xs, mod[l], row(g_norm1[l]), q_perm(w_in[l], 1).astype(BF16), cos_t, sin_t,
                         tile2(att_q_gain[l], scale), tile2(att_k_gain[l]),
                         tile2(na_q_gain[l], scale), tile2(na_k_gain[l]),
                         n_batch=n_batch, n_tok=n_tok)
        o_a = _gqa(_logits_bounded(att_q_gain[l], att_k_gain[l]), qkv, need_ctx=need_ctx, n_ctx=n_ctx)
        o_d = _na(_logits_bounded(na_q_gain[l], na_k_gain[l], jnp.max(jnp.abs(na_rel_bias[l]))), qkv,
                  _na_bias_table(na_rel_bias[l], n_lat // GRID_W), need_ctx=need_ctx, n_ctx=n_ctx)
        o_f = _fourier(qkv, cs_bd, dft, dft_ctx, w_fourier[l].astype(BF16), need_ctx=need_ctx, n_ctx=n_ctx)
        a_tab, b_dense, c_dense = _s5_tables(ssm_lam_re[l], ssm_lam_im[l], ssm_log_dt[l], ssm_b_re[l],
                                             ssm_b_im[l], ssm_c_re[l], ssm_c_im[l])
        y_fwd = _s5(u, a_tab, b_dense, c_dense, n_ctx=n_ctx, reverse=False)
        y_bwd = _s5(u, a_tab, b_dense, c_dense, n_ctx=n_ctx, reverse=True)
        x1 = _mixout(xs, mod[l], o_a, o_d, o_f, u, y_fwd, y_bwd, row(ssm_d[l]), w_glu[l].astype(BF16),
                     row(b_glu[l]), row(q_perm(g_group[l], 0)), q_perm(w_out[l], 0).astype(BF16),
                     need_ctx=need_ctx, n_ctx=n_ctx)
        x2 = _ffn(x1, mod[l], row(g_norm2[l]), w_ff1[l].astype(BF16), w_ff3[l].astype(BF16),
                  w_ff2[l].astype(BF16), n_ctx_rows=n_ctx if need_ctx else 0)
        xs = (x2,)
    return xs[0]
```

```python
import functools
import math

import numpy as np
import jax
import jax.numpy as jnp
from jax import lax
from jax.experimental import pallas as pl
from jax.experimental.pallas import tpu as pltpu

F32 = jnp.float32
BF16 = jnp.bfloat16

D_MODEL = 1024
DEPTH = 2
GRID_W = 64
HEAD_DIM = 64
N_GROUPS = 4
GROUP_W = D_MODEL // N_GROUPS
WIN_H = 8
WIN_W = 16
NA_HEADS = GROUP_W // HEAD_DIM
FNO_HEAD_W = 64
SSM_GROUP = 16
SSM_GROUPS = GROUP_W // SSM_GROUP
SSM_STATE = 64
SSM_W = SSM_GROUPS * SSM_STATE
ROPE_THETA = 10000.0
D_FF = ((8 * D_MODEL // 3 + 255) // 256) * 256
N_MOD = 6
EPS = 1e-6
IN_W = 7 * GROUP_W
QKV_W = 6 * GROUP_W

LANES = 128
TILE = 256
BAND_ROWS = TILE // GRID_W
S5_STEPS = 128
S5_SUB = 32
FF_CHUNK = D_FF // 2
FOURIER_ROWS = (1024, 512, 256)
VMEM_LIMIT_BYTES = 56 * 2**20
NEG_BIG = -1e30
LOG2E = 1.0 / math.log(2.0)
EXP2_SAFE = 60.0

_Q_ORDER = (0, 2, 1, 3)


def _params(sem):
    return pltpu.CompilerParams(dimension_semantics=sem, vmem_limit_bytes=VMEM_LIMIT_BYTES)


def _sds(shape, dtype):
    return jax.ShapeDtypeStruct(shape, dtype)


def _mod_kernel(c_ref, w_ref, b_ref, o_ref):
    c = c_ref[...]
    s = c * jax.nn.sigmoid(c)
    o_ref[0] = jnp.dot(s, w_ref[0], preferred_element_type=F32, precision=lax.Precision.HIGHEST) + b_ref[0]


def _modulation(cc, w_mod, b_mod):
    rows, d = cc.shape
    n_blk = 4
    bw = N_MOD * d // n_blk
    return pl.pallas_call(
        _mod_kernel,
        grid=(DEPTH, n_blk),
        in_specs=[
            pl.BlockSpec((rows, d), lambda l, j: (0, 0)),
            pl.BlockSpec((1, d, bw), lambda l, j: (l, 0, j)),
            pl.BlockSpec((1, 1, bw), lambda l, j: (l, 0, j)),
        ],
        out_specs=pl.BlockSpec((1, rows, bw), lambda l, j: (l, 0, j)),
        out_shape=_sds((DEPTH, rows, N_MOD * d), F32),
        compiler_params=_params(("arbitrary", "arbitrary")),
        name="modulation",
    )(cc, w_mod, b_mod)


def _rms_rows(x):
    return x * lax.rsqrt(jnp.mean(x * x, axis=-1, keepdims=True) + EPS)


def _mod_slices(mod_ref, first):
    m = mod_ref[0]
    return [m[:, (first + k) * D_MODEL:(first + k + 1) * D_MODEL] for k in range(3)]


def _head_norm(t, gain):
    lo = lax.broadcasted_iota(jnp.int32, t.shape, 1) < HEAD_DIM
    sq = t * t
    s_lo = jnp.sum(jnp.where(lo, sq, 0.0), axis=-1, keepdims=True)
    s_hi = jnp.sum(jnp.where(lo, 0.0, sq), axis=-1, keepdims=True)
    ms = jnp.where(lo, s_lo, s_hi) * (1.0 / HEAD_DIM)
    return t * lax.rsqrt(ms + EPS) * gain


def _rope(t, cos, sin):
    first = (lax.broadcasted_iota(jnp.int32, t.shape, 1) & 16) == 0
    partner = jnp.where(first, pltpu.roll(t, LANES - 16, axis=1), pltpu.roll(t, 16, axis=1))
    return t * cos + partner * sin


def _softmax_pv(s, v, bounded):
    p = jnp.exp2(s if bounded else s - jnp.max(s, axis=-1, keepdims=True))
    l = jnp.sum(p, axis=-1, keepdims=True)
    return jnp.dot(p.astype(BF16), v, preferred_element_type=F32) / l


def _when_bounded(flag_ref, body):
    bounded = flag_ref[0] == 1
    pl.when(bounded)(lambda: body(True))
    pl.when(jnp.logical_not(bounded))(lambda: body(False))


def _logits_bounded(gq, gk, extra=0.0):
    bound = math.sqrt(HEAD_DIM) * jnp.max(jnp.abs(gq)) * jnp.max(jnp.abs(gk)) + extra
    return (bound * LOG2E <= EXP2_SAFE).astype(jnp.int32).reshape(1)


def _split_heads(q):
    lo = lax.broadcasted_iota(jnp.int32, q.shape, 1) < HEAD_DIM
    zero = jnp.zeros_like(q)
    return jnp.concatenate([jnp.where(lo, q, zero), jnp.where(lo, zero, q)], axis=0)


def _merge_heads(o, rows):
    lo = lax.broadcasted_iota(jnp.int32, (rows, LANES), 1) < HEAD_DIM
    return jnp.where(lo, o[:rows], o[rows:])


def _nt_dot(a, b):
    return lax.dot_general(a, b, (((1,), (1,)), ((), ())), preferred_element_type=F32)


def _inproj_kernel(*refs, split_input):
    if split_input:
        x_ref, ctx_ref, *refs = refs
    else:
        x_ref, *refs = refs
    (mod_ref, g1_ref, w_ref, cos_ref, sin_ref, gq_ref, gk_ref, nq_ref, nk_ref, qkv_ref, u_ref) = refs
    x = x_ref[0]
    if split_input:
        x = jnp.where(pl.program_id(1) == 0, ctx_ref[0], x)
    sh1, sc1, _ = _mod_slices(mod_ref, 0)
    h = (_rms_rows(x) * g1_ref[...]) * (1.0 + sc1) + sh1
    p = jnp.dot(h.astype(BF16), w_ref[...], preferred_element_type=F32)
    cos = cos_ref[...]
    sin = sin_ref[...]

    def tile(j):
        return p[:, j * LANES:(j + 1) * LANES]

    out = [
        _rope(_head_norm(tile(0), gq_ref[...]), cos, sin),
        _rope(_head_norm(tile(1), gq_ref[...]), cos, sin),
        _rope(_head_norm(tile(2), gk_ref[...]), cos, sin),
        tile(3),
        _head_norm(tile(4), nq_ref[...]),
        _head_norm(tile(5), nq_ref[...]),
        _head_norm(tile(6), nk_ref[...]),
        _head_norm(tile(7), nk_ref[...]),
        tile(8), tile(9), tile(10), tile(11),
    ]
    for j, t in enumerate(out):
        qkv_ref[0, :, j * LANES:(j + 1) * LANES] = t.astype(BF16)
    u_ref[0] = p[:, QKV_W:]


def _inproj(xs, mod, g1, w_in, cos_t, sin_t, gq, gk, nq, nk, *, n_batch, n_tok):
    split_input = len(xs) == 2
    n_tiles = n_tok // TILE
    tok = (1, TILE, D_MODEL)
    if split_input:
        x_specs = [pl.BlockSpec(tok, lambda b, i: (b, jnp.maximum(i - 1, 0), 0)),
                   pl.BlockSpec(tok, lambda b, i: (b, 0, 0))]
    else:
        x_specs = [pl.BlockSpec(tok, lambda b, i: (b, i, 0))]
    vec = lambda w: pl.BlockSpec((1, w), lambda b, i: (0, 0))
    return pl.pallas_call(
        functools.partial(_inproj_kernel, split_input=split_input),
        grid=(n_batch, n_tiles),
        in_specs=x_specs + [
            pl.BlockSpec((1, 1, N_MOD * D_MODEL), lambda b, i: (jnp.where(i == 0, n_batch, b), 0, 0)),
            vec(D_MODEL),
            pl.BlockSpec((D_MODEL, IN_W), lambda b, i: (0, 0)),
            pl.BlockSpec((TILE, LANES), lambda b, i: (i, 0)),
            pl.BlockSpec((TILE, LANES), lambda b, i: (i, 0)),
            vec(LANES), vec(LANES), vec(LANES), vec(LANES),
        ],
        out_specs=[pl.BlockSpec((1, TILE, QKV_W), lambda b, i: (b, i, 0)),
                   pl.BlockSpec((1, TILE, GROUP_W), lambda b, i: (b, i, 0))],
        out_shape=[_sds((n_batch, n_tok, QKV_W), BF16), _sds((n_batch, n_tok, GROUP_W), F32)],
        compiler_params=_params(("arbitrary", "arbitrary")),
        name="inproj",
    )(*xs, mod, g1, w_in, cos_t, sin_t, gq, gk, nq, nk)


def _gqa_kernel(flag_ref, qa_ref, qb_ref, k_ref, v_ref, o_ref, *, n_ctx, tile_off):
    def attend(n_keys, bounded):
        k = k_ref[0, :n_keys, :]
        v = v_ref[0, :n_keys, :]
        outs = []
        for q_ref in (qa_ref, qb_ref):
            q2 = _split_heads(q_ref[0])
            outs.append(_merge_heads(_softmax_pv(_nt_dot(q2, k), v, bounded), TILE))
        o_ref[0] = jnp.concatenate(outs, axis=-1).astype(o_ref.dtype)

    if tile_off == 0:
        is_ctx = pl.program_id(1) == 0
        pl.when(is_ctx)(lambda: _when_bounded(flag_ref, functools.partial(attend, n_ctx)))
        pl.when(jnp.logical_not(is_ctx))(lambda: _when_bounded(flag_ref, functools.partial(attend, k_ref.shape[1])))
    else:
        _when_bounded(flag_ref, functools.partial(attend, k_ref.shape[1]))


_SMEM_SPEC = pl.BlockSpec(memory_space=pltpu.SMEM)


def _gqa(bounded, qkv, *, need_ctx, n_ctx):
    n_batch, n_tok, _ = qkv.shape
    off = 0 if need_ctx else n_ctx // TILE
    q_spec = lambda col: pl.BlockSpec((1, TILE, LANES), lambda b, i: (b, i + off, col))
    kv_spec = lambda col: pl.BlockSpec((1, n_tok, LANES), lambda b, i: (b, 0, col))
    return pl.pallas_call(
        functools.partial(_gqa_kernel, n_ctx=n_ctx, tile_off=off),
        grid=(n_batch, n_tok // TILE - off),
        in_specs=[_SMEM_SPEC, q_spec(0), q_spec(1), kv_spec(2), kv_spec(3)],
        out_specs=pl.BlockSpec((1, TILE, GROUP_W), lambda b, i: (b, i, 0)),
        out_shape=_sds((n_batch, n_tok - off * TILE, GROUP_W), BF16),
        compiler_params=_params(("arbitrary", "arbitrary")),
        name="gqa",
    )(bounded, qkv, qkv, qkv, qkv)


def _na_kernel(flag_ref, q_ref, kp_ref, kc_ref, kn_ref, kx_ref, vp_ref, vc_ref, vn_ref, vx_ref, tbl_ref, o_ref, *, tile_off):
    def attend(local, bounded):
        outs = []
        for pair in range(2):
            cols = slice(pair * LANES, (pair + 1) * LANES)
            q2 = _split_heads(q_ref[0, :, cols])
            if local:
                k = jnp.concatenate([r[0, :, cols] for r in (kp_ref, kc_ref, kn_ref, kx_ref)], axis=0)
                v = jnp.concatenate([r[0, :, cols] for r in (vp_ref, vc_ref, vn_ref, vx_ref)], axis=0)
                bias = jnp.concatenate([tbl_ref[0, 2 * pair], tbl_ref[0, 2 * pair + 1]], axis=0)
                s = _nt_dot(q2, k)
                s = jnp.concatenate([s[:, :3 * TILE] + bias, s[:, 3 * TILE:]], axis=-1)
            else:
                k = kx_ref[0, :, cols]
                v = vx_ref[0, :, cols]
                s = _nt_dot(q2, k)
            outs.append(_merge_heads(_softmax_pv(s, v, bounded), TILE))
        o_ref[0] = jnp.concatenate(outs, axis=-1).astype(o_ref.dtype)

    if tile_off == 0:
        is_ctx = pl.program_id(1) == 0
        pl.when(is_ctx)(lambda: _when_bounded(flag_ref, functools.partial(attend, False)))
        pl.when(jnp.logical_not(is_ctx))(lambda: _when_bounded(flag_ref, functools.partial(attend, True)))
    else:
        _when_bounded(flag_ref, functools.partial(attend, True))


def _na(bounded, qkv, table, *, need_ctx, n_ctx):
    n_batch, n_tok, _ = qkv.shape
    assert n_ctx == TILE
    n_tiles = n_tok // TILE
    off = 0 if need_ctx else 1
    blk = (1, TILE, GROUP_W)
    cur = lambda col: pl.BlockSpec(blk, lambda b, i: (b, i + off, col))
    prev = lambda col: pl.BlockSpec(blk, lambda b, i: (b, jnp.maximum(i + off - 1, 1), col))
    nxt = lambda col: pl.BlockSpec(blk, lambda b, i: (b, jnp.minimum(i + off + 1, n_tiles - 1), col))
    ctx = lambda col: pl.BlockSpec(blk, lambda b, i: (b, 0, col))

    def cls(b, i):
        t = i + off
        return (jnp.where(t == n_tiles - 1, 2, jnp.where(t <= 1, 0, 1)), 0, 0, 0)

    return pl.pallas_call(
        functools.partial(_na_kernel, tile_off=off),
        grid=(n_batch, n_tiles - off),
        in_specs=[_SMEM_SPEC, cur(2), prev(3), cur(3), nxt(3), ctx(3), prev(4), cur(4), nxt(4), ctx(4),
                  pl.BlockSpec((1, NA_HEADS, TILE, 3 * TILE), cls)],
        out_specs=pl.BlockSpec(blk, lambda b, i: (b, i, 0)),
        out_shape=_sds((n_batch, n_tok - off * TILE, GROUP_W), BF16),
        compiler_params=_params(("arbitrary", "arbitrary")),
        name="na",
    )(bounded, *([qkv] * 9), table)


def _na_bias_table(rel_bias, rows):
    n_bands = rows // BAND_ROWS
    assert rows >= WIN_H and rows % BAND_ROWS == 0
    n_dr, n_dc = 2 * WIN_H - 1, 2 * WIN_W - 1
    w = GRID_W
    shift = w - WIN_W
    vp = jnp.pad(rel_bias.astype(F32) * LOG2E, ((0, 0), (0, 0), (shift, 2 * w - shift - n_dc)))
    skew = jnp.broadcast_to(vp[:, :, None, :], (NA_HEADS, n_dr, w, 2 * w)).reshape(NA_HEADS, n_dr, 2 * w * w)
    col = skew[:, :, :w * (2 * w - 1)].reshape(NA_HEADS, n_dr, w, 2 * w - 1)[:, :, :, w - 1:]
    qc = np.arange(w)
    cs = np.clip(qc - WIN_W // 2, 0, w - WIN_W)
    col_ok = (qc[None, :] >= cs[:, None]) & (qc[None, :] < cs[:, None] + WIN_W)
    col = jnp.where(col_ok[None, None], col, NEG_BIG)
    masked = jnp.full((NA_HEADS, w, w), NEG_BIG, F32)
    classes = []
    for band in (0, min(1, n_bands - 1), n_bands - 1):
        q_rows = []
        for i in range(BAND_ROWS):
            r = BAND_ROWS * band + i
            rs = min(max(r - WIN_H // 2, 0), rows - WIN_H)
            blocks = []
            for kk in range(3 * BAND_ROWS):
                kr = BAND_ROWS * (band - 1) + kk
                ok = 0 <= kr < rows and rs <= kr < rs + WIN_H
                blocks.append(col[:, kr - r + WIN_H - 1] if ok else masked)
            q_rows.append(jnp.concatenate(blocks, axis=-1))
        classes.append(jnp.concatenate(q_rows, axis=-2))
    return jnp.stack(classes)


def _fourier_kernel(x_ref, cs_ref, dft_ref, dftc_ref, w_ref, o_ref, x12_ref, y_ref, *, n_ctx, n_lat, tile_off):
    t = pl.program_id(1) + tile_off
    lat_t = t - n_ctx // TILE
    group = y_ref.shape[0] // TILE

    def head_dft(x):
        x12 = jnp.dot(x, cs_ref[...], preferred_element_type=F32).astype(BF16)
        return jnp.concatenate([x12[:, :GROUP_W], x12[:, GROUP_W:]], axis=0)

    def finish(y, n):
        y = y * (1.0 / math.sqrt(n * FNO_HEAD_W))
        return jnp.dot(y.astype(BF16), w_ref[...], preferred_element_type=F32).astype(o_ref.dtype)

    if tile_off == 0:
        @pl.when(t == 0)
        def _():
            x12 = head_dft(x_ref[0, :n_ctx, :])
            o_ref[0] = finish(jnp.dot(dftc_ref[...], x12, preferred_element_type=F32), n_ctx)

    @pl.when(lat_t == 0)
    def _():
        x12_ref[...] = head_dft(x_ref[0, n_ctx:, :])

    @pl.when(jnp.logical_and(lat_t >= 0, lax.rem(lat_t, group) == 0))
    def _():
        row0 = pl.multiple_of(lat_t * TILE, y_ref.shape[0])
        rows = dft_ref[pl.ds(row0, y_ref.shape[0]), :]
        y_ref[...] = finish(jnp.dot(rows, x12_ref[...], preferred_element_type=F32), n_lat)

    @pl.when(lat_t >= 0)
    def _():
        o_ref[0] = y_ref[pl.ds(pl.multiple_of(lax.rem(lat_t, group) * TILE, TILE), TILE), :]


def _fourier(qkv, cs_bd, dft, dft_ctx, w_f, *, need_ctx, n_ctx):
    n_batch, n_tok, _ = qkv.shape
    n_lat = n_tok - n_ctx
    off = 0 if need_ctx else n_ctx // TILE
    const = lambda shape: pl.BlockSpec(shape, lambda b, i: (0,) * len(shape))
    return pl.pallas_call(
        functools.partial(_fourier_kernel, n_ctx=n_ctx, n_lat=n_lat, tile_off=off),
        grid=(n_batch, n_tok // TILE - off),
        in_specs=[pl.BlockSpec((1, n_tok, GROUP_W), lambda b, i: (b, 0, 5)),
                  const(cs_bd.shape), const(dft.shape), const(dft_ctx.shape), const(w_f.shape)],
        out_specs=pl.BlockSpec((1, TILE, GROUP_W), lambda b, i: (b, i, 0)),
        out_shape=_sds((n_batch, n_tok - off * TILE, GROUP_W), BF16),
        scratch_shapes=[pltpu.VMEM((2 * n_lat, GROUP_W), BF16),
                        pltpu.VMEM((next(r for r in FOURIER_ROWS if n_lat % r == 0), GROUP_W), BF16)],
        compiler_params=_params(("arbitrary", "arbitrary")),
        name="fourier",
    )(qkv, cs_bd, dft, dft_ctx, w_f)


def _dft_matrix(n):
    f = 32
    assert n % f == 0
    k = lax.broadcasted_iota(jnp.int32, (n, 1), 0)
    m1 = lax.broadcasted_iota(jnp.int32, (1, n // f), 1)
    m0 = lax.broadcasted_iota(jnp.int32, (1, f), 1)
    ang_a = ((k * (m1 * f)) % n).astype(F32) * (2.0 * math.pi / n)
    ang_b = ((k * m0) % n).astype(F32) * (2.0 * math.pi / n)
    ca, sa = jnp.cos(ang_a)[:, :, None], jnp.sin(ang_a)[:, :, None]
    cb, sb = jnp.cos(ang_b)[:, None, :], jnp.sin(ang_b)[:, None, :]
    cos = (ca * cb - sa * sb).reshape(n, n)
    sin = (sa * cb + ca * sb).reshape(n, n)
    return jnp.concatenate([cos, -sin], axis=1).astype(BF16)


def _head_dft_matrix():
    a = np.arange(GROUP_W)
    same = (a[:, None] // FNO_HEAD_W) == (a[None, :] // FNO_HEAD_W)
    ang = 2.0 * np.pi * (((a[:, None] % FNO_HEAD_W) * (a[None, :] % FNO_HEAD_W)) % FNO_HEAD_W) / FNO_HEAD_W
    m = np.concatenate([np.where(same, np.cos(ang), 0.0), np.where(same, np.sin(ang), 0.0)], axis=1)
    return jnp.asarray(m, F32).astype(BF16)


def _s5_kernel(u_ref, a_ref, bd_ref, cd_ref, y_ref, bu_ref, hb_ref, h_ref, *, n_batch, reverse):
    @pl.when(pl.program_id(0) == 0)
    def _():
        h_ref[...] = jnp.zeros_like(h_ref)

    sub_rows = S5_SUB * n_batch
    chunk = 4 * LANES
    u = pltpu.einshape("btc->tbc", u_ref[...]).reshape(S5_STEPS * n_batch, GROUP_W).astype(BF16)

    def project(q):
        r = slice(q * sub_rows, (q + 1) * sub_rows)
        bu_ref[r, :] = jnp.dot(u[r], bd_ref[0], preferred_element_type=F32)

    def scan(q):
        steps = range(S5_SUB - 1, -1, -1) if reverse else range(S5_SUB)
        for sc in range(SSM_W // chunk):
            re = slice(sc * chunk, (sc + 1) * chunk)
            im = slice(SSM_W + sc * chunk, SSM_W + (sc + 1) * chunk)
            a_re = jnp.broadcast_to(a_ref[0, :, re], (n_batch, chunk))
            a_im = jnp.broadcast_to(a_ref[0, :, im], (n_batch, chunk))
            h_re, h_im = h_ref[:, re], h_ref[:, im]
            for t in steps:
                r = slice(q * sub_rows + t * n_batch, q * sub_rows + (t + 1) * n_batch)
                h_re, h_im = (a_re * h_re - a_im * h_im + bu_ref[r, re],
                              a_re * h_im + a_im * h_re + bu_ref[r, im])
                hb_ref[r, re] = h_re.astype(BF16)
                hb_ref[r, im] = h_im.astype(BF16)
            h_ref[:, re] = h_re
            h_ref[:, im] = h_im

    def readout(q):
        y = jnp.dot(hb_ref[q * sub_rows:(q + 1) * sub_rows, :], cd_ref[0], preferred_element_type=F32)
        y = pltpu.einshape("tbc->btc", y.reshape(S5_SUB, n_batch, GROUP_W))
        y_ref[:, q * S5_SUB:(q + 1) * S5_SUB, :] = y.astype(y_ref.dtype)

    order = list(range(S5_STEPS // S5_SUB))
    if reverse:
        order.reverse()
    project(order[0])
    for j, q in enumerate(order):
        if j + 1 < len(order):
            project(order[j + 1])
        scan(q)
        if j > 0:
            readout(order[j - 1])
    readout(order[-1])


def _s5(u, a_tab, b_dense, c_dense, *, n_ctx, reverse):
    n_batch, n_tok, _ = u.shape
    n_blk = n_tok // S5_STEPS
    n_cblk = n_ctx // S5_STEPS
    d = int(reverse)
    if reverse:
        blk = lambda s: jnp.where(s < n_cblk, n_cblk - 1 - s, n_blk - 1 - (s - n_cblk))
    else:
        blk = lambda s: s
    return pl.pallas_call(
        functools.partial(_s5_kernel, n_batch=n_batch, reverse=reverse),
        grid=(n_blk,),
        in_specs=[pl.BlockSpec((n_batch, S5_STEPS, GROUP_W), lambda s: (0, blk(s), 0)),
                  pl.BlockSpec((1, 1, 2 * SSM_W), lambda s: (d, 0, 0)),
                  pl.BlockSpec((1, GROUP_W, 2 * SSM_W), lambda s: (d, 0, 0)),
                  pl.BlockSpec((1, 2 * SSM_W, GROUP_W), lambda s: (d, 0, 0))],
        out_specs=pl.BlockSpec((n_batch, S5_STEPS, GROUP_W), lambda s: (0, blk(s), 0)),
        out_shape=_sds((n_batch, n_tok, GROUP_W), BF16),
        scratch_shapes=[pltpu.VMEM((S5_STEPS * n_batch, 2 * SSM_W), F32),
                        pltpu.VMEM((S5_STEPS * n_batch, 2 * SSM_W), BF16),
                        pltpu.VMEM((n_batch, 2 * SSM_W), F32)],
        compiler_params=_params(("arbitrary",)),
        name="s5",
    )(u, a_tab, b_dense, c_dense)


def _s5_tables(lam_re, lam_im, log_dt, b_re, b_im, c_re, c_im):
    lam = lax.complex(lam_re.astype(F32), lam_im.astype(F32))
    dt = jnp.exp(log_dt.astype(F32))[..., None]
    a_bar = jnp.exp(lam * dt)
    b_bar = ((a_bar - 1.0) / lam)[..., None] * lax.complex(b_re.astype(F32), b_im.astype(F32))
    eye = jnp.eye(SSM_GROUPS, dtype=F32)

    def embed_b(m):
        m = jnp.transpose(m, (0, 1, 3, 2))[:, :, :, None, :] * eye[None, :, None, :, None]
        return m.reshape(2, GROUP_W, SSM_W)

    def embed_c(m):
        m = jnp.transpose(m, (0, 1, 3, 2))[:, :, :, None, :] * eye[None, :, None, :, None]
        return m.reshape(2, SSM_W, GROUP_W)

    a_tab = jnp.concatenate([jnp.real(a_bar).reshape(2, 1, SSM_W), jnp.imag(a_bar).reshape(2, 1, SSM_W)], axis=-1)
    b_dense = jnp.concatenate([embed_b(jnp.real(b_bar)), embed_b(jnp.imag(b_bar))], axis=-1).astype(BF16)
    c_dense = jnp.concatenate([embed_c(c_re.astype(F32)), -embed_c(c_im.astype(F32))], axis=1).astype(BF16)
    return a_tab, b_dense, c_dense


def _gelu_tanh(x):
    return x * (0.5 * (1.0 + jnp.tanh(math.sqrt(2.0 / math.pi) * (x + 0.044715 * (x * x * x)))))


def _mixout_kernel(*refs, split_input):
    if split_input:
        x_ref, ctx_ref, *refs = refs
    else:
        x_ref, *refs = refs
    (mod_ref, oa_ref, od_ref, of_ref, u_ref, yf_ref, yb_ref, d_ref, wg_ref, bg_ref, gg_ref, wo_ref,
     g2_ref, w1_ref, w3_ref, w2_ref, o_ref) = refs
    x = x_ref[0]
    if split_input:
        x = jnp.where(pl.program_id(1) == 0, ctx_ref[0], x)
    _, _, ga1 = _mod_slices(mod_ref, 0)
    sh2, sc2, ga2 = _mod_slices(mod_ref, 3)
    y = u_ref[0] * d_ref[...] + yf_ref[0].astype(F32) + yb_ref[0].astype(F32)
    g = _gelu_tanh(y)
    z = jnp.dot(g.astype(BF16), wg_ref[...], preferred_element_type=F32) + bg_ref[...]
    o_s = g * jax.nn.sigmoid(z)
    parts = [oa_ref[0].astype(F32), od_ref[0].astype(F32), of_ref[0].astype(F32), o_s]
    normed = [(_rms_rows(p) * gg_ref[:, j * GROUP_W:(j + 1) * GROUP_W]).astype(BF16) for j, p in enumerate(parts)]
    r = jnp.dot(jnp.concatenate(normed, axis=-1), wo_ref[...], preferred_element_type=F32)
    x = x + ga1 * r
    h = ((_rms_rows(x) * g2_ref[...]) * (1.0 + sc2) + sh2).astype(BF16)
    acc = jnp.zeros(x.shape, F32)
    for c in range(D_FF // FF_CHUNK):
        cols = slice(c * FF_CHUNK, (c + 1) * FF_CHUNK)
        a = jnp.dot(h, w1_ref[:, cols], preferred_element_type=F32)
        b = jnp.dot(h, w3_ref[:, cols], preferred_element_type=F32)
        t = (a * jax.nn.sigmoid(a)) * b
        acc = acc + jnp.dot(t.astype(BF16), w2_ref[cols, :], preferred_element_type=F32)
    o_ref[0] = x + ga2 * acc


def _mixout_ffn(xs, mod, o_a, o_d, o_f, u, y_fwd, y_bwd, d_skip, w_glu, b_glu, g_group, w_out, g2, w1, w3, w2, *,
                need_ctx, n_ctx):
    split_input = len(xs) == 2
    n_batch, n_tok, _ = u.shape
    off = 0 if need_ctx else n_ctx // TILE
    n_out = n_tok - off * TILE
    tok = (1, TILE, D_MODEL)
    if split_input:
        assert need_ctx
        x_specs = [pl.BlockSpec(tok, lambda b, i: (b, jnp.maximum(i - 1, 0), 0)),
                   pl.BlockSpec(tok, lambda b, i: (b, 0, 0))]
    else:
        x_specs = [pl.BlockSpec(tok, lambda b, i: (b, i + off, 0))]
    grp = pl.BlockSpec((1, TILE, GROUP_W), lambda b, i: (b, i + off, 0))
    mix = pl.BlockSpec((1, TILE, GROUP_W), lambda b, i: (b, i, 0))
    const = lambda shape: pl.BlockSpec(shape, lambda b, i: (0,) * len(shape), pipeline_mode=pl.Buffered(1))
    return pl.pallas_call(
        functools.partial(_mixout_kernel, split_input=split_input),
        grid=(n_batch, n_out // TILE),
        in_specs=x_specs + [
            pl.BlockSpec((1, 1, N_MOD * D_MODEL), lambda b, i: (jnp.where(i + off == 0, n_batch, b), 0, 0)),
            mix, mix, mix, grp, grp, grp,
            const((1, GROUP_W)), const((GROUP_W, GROUP_W)), const((1, GROUP_W)),
            const((1, D_MODEL)), const((D_MODEL, D_MODEL)),
            const((1, D_MODEL)), const(w1.shape), const(w3.shape), const(w2.shape),
        ],
        out_specs=pl.BlockSpec(tok, lambda b, i: (b, i, 0)),
        out_shape=_sds((n_batch, n_out, D_MODEL), F32),
        compiler_params=_params(("arbitrary", "arbitrary")),
        name="mixffn",
    )(*xs, mod, o_a, o_d, o_f, u, y_fwd, y_bwd, d_skip, w_glu, b_glu, g_group, w_out, g2, w1, w3, w2)


def _rope_tables(n_lat, n_ctx):
    t = jnp.arange(n_lat, dtype=jnp.int32)
    rows = (t // GRID_W).astype(F32)
    cols = (t % GRID_W).astype(F32)
    axis_dim = HEAD_DIM // 2
    inv_freq = ROPE_THETA ** (-jnp.arange(0, axis_dim, 2, dtype=F32) / axis_dim)
    ang_r = rows[:, None] * inv_freq[None, :]
    ang_c = cols[:, None] * inv_freq[None, :]
    cos = jnp.concatenate([jnp.cos(ang_r)] * 2 + [jnp.cos(ang_c)] * 2, axis=-1)
    sin = jnp.concatenate([-jnp.sin(ang_r), jnp.sin(ang_r), -jnp.sin(ang_c), jnp.sin(ang_c)], axis=-1)
    cos = jnp.concatenate([jnp.ones((n_ctx, HEAD_DIM), F32), cos], axis=0)
    sin = jnp.concatenate([jnp.zeros((n_ctx, HEAD_DIM), F32), sin], axis=0)
    return jnp.tile(cos, (1, 2)), jnp.tile(sin, (1, 2))


def kernel(x, c, ctx, c_ctx, w_mod, b_mod, g_norm1, w_in, att_q_gain, att_k_gain, na_q_gain, na_k_gain, na_rel_bias, w_fourier, ssm_lam_re, ssm_lam_im, ssm_log_dt, ssm_b_re, ssm_b_im, ssm_c_re, ssm_c_im, ssm_d, w_glu, b_glu, g_group, w_out, g_norm2, w_ff1, w_ff3, w_ff2):
    n_batch, n_lat, d = x.shape
    n_ctx = ctx.shape[1]
    n_tok = n_lat + n_ctx
    assert d == D_MODEL and n_ctx == TILE and n_lat % TILE == 0 and n_lat % GRID_W == 0
    assert n_batch % 8 == 0 and n_tok % S5_STEPS == 0

    mod_rows = -(-(n_batch + 1) // 8) * 8
    cc = jnp.concatenate([c, c_ctx[None, :], jnp.zeros((mod_rows - n_batch - 1, d), F32)], axis=0)
    mod = _modulation(cc, w_mod, b_mod.reshape(DEPTH, 1, N_MOD * d)).reshape(DEPTH, mod_rows, 1, N_MOD * d)

    cos_t, sin_t = _rope_tables(n_lat, n_ctx)
    cs_bd = _head_dft_matrix()
    dft = _dft_matrix(n_lat)
    dft_ctx = _dft_matrix(n_ctx)
    def q_perm(a, axis):
        heads = [lax.slice_in_dim(a, h * HEAD_DIM, (h + 1) * HEAD_DIM, axis=axis) for h in _Q_ORDER]
        return jnp.concatenate(heads + [lax.slice_in_dim(a, GROUP_W, a.shape[axis], axis=axis)], axis=axis)

    scale = HEAD_DIM ** -0.5 * LOG2E
    tile2 = lambda g, s=1.0: jnp.tile(g.astype(F32) * s, 2)[None, :]
    row = lambda v: v.astype(F32)[None, :]

    xs = (x, ctx)
    for l in range(DEPTH):
        need_ctx = l < DEPTH - 1
        qkv, u = _inproj(xs, mod[l], row(g_norm1[l]), q_perm(w_in[l], 1).astype(BF16), cos_t, sin_t,
                         tile2(att_q_gain[l], scale), tile2(att_k_gain[l]),
                         tile2(na_q_gain[l], scale), tile2(na_k_gain[l]),
                         n_batch=n_batch, n_tok=n_tok)
        o_a = _gqa(_logits_bounded(att_q_gain[l], att_k_gain[l]), qkv, need_ctx=need_ctx, n_ctx=n_ctx)
        o_d = _na(_logits_bounded(na_q_gain[l], na_k_gain[l], jnp.max(jnp.abs(na_rel_bias[l]))), qkv,
                  _na_bias_table(na_rel_bias[l], n_lat // GRID_W), need_ctx=need_ctx, n_ctx=n_ctx)
        o_f = _fourier(qkv, cs_bd, dft, dft_ctx, w_fourier[l].astype(BF16), need_ctx=need_ctx, n_ctx=n_ctx)
        a_tab, b_dense, c_dense = _s5_tables(ssm_lam_re[l], ssm_lam_im[l], ssm_log_dt[l], ssm_b_re[l],
                                             ssm_b_im[l], ssm_c_re[l], ssm_c_im[l])
        y_fwd = _s5(u, a_tab, b_dense, c_dense, n_ctx=n_ctx, reverse=False)
        y_bwd = _s5(u, a_tab, b_dense, c_dense, n_ctx=n_ctx, reverse=True)
        x2 = _mixout_ffn(xs, mod[l], o_a, o_d, o_f, u, y_fwd, y_bwd, row(ssm_d[l]), w_glu[l].astype(BF16),
                         row(b_glu[l]), row(q_perm(g_group[l], 0)), q_perm(w_out[l], 0).astype(BF16),
                         row(g_norm2[l]), w_ff1[l].astype(BF16), w_ff3[l].astype(BF16), w_ff2[l].astype(BF16),
                         need_ctx=need_ctx, n_ctx=n_ctx)
        xs = (x2,)
    return xs[0]
```

```python
import functools
import math

import numpy as np
import jax
import jax.numpy as jnp
from jax import lax
from jax.experimental import pallas as pl
from jax.experimental.pallas import tpu as pltpu

F32 = jnp.float32
BF16 = jnp.bfloat16

D_MODEL = 1024
DEPTH = 2
GRID_W = 64
HEAD_DIM = 64
N_GROUPS = 4
GROUP_W = D_MODEL // N_GROUPS
WIN_H = 8
WIN_W = 16
NA_HEADS = GROUP_W // HEAD_DIM
FNO_HEAD_W = 64
SSM_GROUP = 16
SSM_GROUPS = GROUP_W // SSM_GROUP
SSM_STATE = 64
SSM_W = SSM_GROUPS * SSM_STATE
ROPE_THETA = 10000.0
D_FF = ((8 * D_MODEL // 3 + 255) // 256) * 256
N_MOD = 6
EPS = 1e-6
IN_W = 7 * GROUP_W
QKV_W = 6 * GROUP_W

LANES = 128
TILE = 256
BAND_ROWS = TILE // GRID_W
S5_STEPS = 128
S5_SUB = 32
FF_CHUNK = D_FF // 2
VMEM_LIMIT_BYTES = 56 * 2**20
NEG_BIG = -1e30
LOG2E = 1.0 / math.log(2.0)
EXP2_SAFE = 60.0

_Q_ORDER = (0, 2, 1, 3)


def _params(sem):
    return pltpu.CompilerParams(dimension_semantics=sem, vmem_limit_bytes=VMEM_LIMIT_BYTES)


def _sds(shape, dtype):
    return jax.ShapeDtypeStruct(shape, dtype)


def _mod_kernel(c_ref, w_ref, b_ref, o_ref):
    c = c_ref[...]
    s = c * jax.nn.sigmoid(c)
    o_ref[0] = jnp.dot(s, w_ref[0], preferred_element_type=F32, precision=lax.Precision.HIGHEST) + b_ref[0]


def _modulation(cc, w_mod, b_mod):
    rows, d = cc.shape
    n_blk = 4
    bw = N_MOD * d // n_blk
    return pl.pallas_call(
        _mod_kernel,
        grid=(DEPTH, n_blk),
        in_specs=[
            pl.BlockSpec((rows, d), lambda l, j: (0, 0)),
            pl.BlockSpec((1, d, bw), lambda l, j: (l, 0, j)),
            pl.BlockSpec((1, 1, bw), lambda l, j: (l, 0, j)),
        ],
        out_specs=pl.BlockSpec((1, rows, bw), lambda l, j: (l, 0, j)),
        out_shape=_sds((DEPTH, rows, N_MOD * d), F32),
        compiler_params=_params(("arbitrary", "arbitrary")),
        name="modulation",
    )(cc, w_mod, b_mod)


def _rms_rows(x):
    return x * lax.rsqrt(jnp.mean(x * x, axis=-1, keepdims=True) + EPS)


def _mod_slices(m, first):
    return [m[:, (first + k) * D_MODEL:(first + k + 1) * D_MODEL] for k in range(3)]


def _head_norm(t, gain):
    lo = lax.broadcasted_iota(jnp.int32, t.shape, 1) < HEAD_DIM
    sq = t * t
    s_lo = jnp.sum(jnp.where(lo, sq, 0.0), axis=-1, keepdims=True)
    s_hi = jnp.sum(jnp.where(lo, 0.0, sq), axis=-1, keepdims=True)
    ms = jnp.where(lo, s_lo, s_hi) * (1.0 / HEAD_DIM)
    return t * lax.rsqrt(ms + EPS) * gain


def _rope(t, cos, sin):
    first = (lax.broadcasted_iota(jnp.int32, t.shape, 1) & 16) == 0
    partner = jnp.where(first, pltpu.roll(t, LANES - 16, axis=1), pltpu.roll(t, 16, axis=1))
    return t * cos + partner * sin


def _softmax_pv(s, v, bounded):
    p = jnp.exp2(s if bounded else s - jnp.max(s, axis=-1, keepdims=True))
    l = jnp.sum(p, axis=-1, keepdims=True)
    return jnp.dot(p.astype(BF16), v, preferred_element_type=F32) / l


def _when_bounded(flag_ref, body):
    bounded = flag_ref[0] == 1
    pl.when(bounded)(lambda: body(True))
    pl.when(jnp.logical_not(bounded))(lambda: body(False))


def _logits_bounded(gq, gk, extra=0.0):
    bound = math.sqrt(HEAD_DIM) * jnp.max(jnp.abs(gq)) * jnp.max(jnp.abs(gk)) + extra
    return (bound * LOG2E <= EXP2_SAFE).astype(jnp.int32).reshape(1)


def _split_heads(q):
    lo = lax.broadcasted_iota(jnp.int32, q.shape, 1) < HEAD_DIM
    zero = jnp.zeros_like(q)
    return jnp.concatenate([jnp.where(lo, q, zero), jnp.where(lo, zero, q)], axis=0)


def _merge_heads(o, rows):
    lo = lax.broadcasted_iota(jnp.int32, (rows, LANES), 1) < HEAD_DIM
    return jnp.where(lo, o[:rows], o[rows:])


def _nt_dot(a, b):
    return lax.dot_general(a, b, (((1,), (1,)), ((), ())), preferred_element_type=F32)


def _inproj_kernel(*refs, split_input):
    if split_input:
        x_ref, ctx_ref, *refs = refs
    else:
        x_ref, *refs = refs
    (mod_ref, g1_ref, w_ref, cos_ref, sin_ref, gq_ref, gk_ref, nq_ref, nk_ref, qkv_ref, u_ref) = refs
    x = x_ref[0]
    if split_input:
        x = jnp.where(pl.program_id(1) == 0, ctx_ref[0], x)
    sh1, sc1, _ = _mod_slices(mod_ref[0], 0)
    h = (_rms_rows(x) * g1_ref[...]) * (1.0 + sc1) + sh1
    p = jnp.dot(h.astype(BF16), w_ref[...], preferred_element_type=F32)
    cos = cos_ref[...]
    sin = sin_ref[...]

    def tile(j):
        return p[:, j * LANES:(j + 1) * LANES]

    out = [
        _rope(_head_norm(tile(0), gq_ref[...]), cos, sin),
        _rope(_head_norm(tile(1), gq_ref[...]), cos, sin),
        _rope(_head_norm(tile(2), gk_ref[...]), cos, sin),
        tile(3),
        _head_norm(tile(4), nq_ref[...]),
        _head_norm(tile(5), nq_ref[...]),
        _head_norm(tile(6), nk_ref[...]),
        _head_norm(tile(7), nk_ref[...]),
        tile(8), tile(9), tile(10), tile(11),
    ]
    for j, t in enumerate(out):
        qkv_ref[0, :, j * LANES:(j + 1) * LANES] = t.astype(BF16)
    u_ref[0] = p[:, QKV_W:]


def _inproj(xs, mod, g1, w_in, cos_t, sin_t, gq, gk, nq, nk, *, n_batch, n_tok):
    split_input = len(xs) == 2
    n_tiles = n_tok // TILE
    tok = (1, TILE, D_MODEL)
    if split_input:
        x_specs = [pl.BlockSpec(tok, lambda b, i: (b, jnp.maximum(i - 1, 0), 0)),
                   pl.BlockSpec(tok, lambda b, i: (b, 0, 0))]
    else:
        x_specs = [pl.BlockSpec(tok, lambda b, i: (b, i, 0))]
    vec = lambda w: pl.BlockSpec((1, w), lambda b, i: (0, 0))
    return pl.pallas_call(
        functools.partial(_inproj_kernel, split_input=split_input),
        grid=(n_batch, n_tiles),
        in_specs=x_specs + [
            pl.BlockSpec((1, 1, N_MOD * D_MODEL), lambda b, i: (jnp.where(i == 0, n_batch, b), 0, 0)),
            vec(D_MODEL),
            pl.BlockSpec((D_MODEL, IN_W), lambda b, i: (0, 0)),
            pl.BlockSpec((TILE, LANES), lambda b, i: (i, 0)),
            pl.BlockSpec((TILE, LANES), lambda b, i: (i, 0)),
            vec(LANES), vec(LANES), vec(LANES), vec(LANES),
        ],
        out_specs=[pl.BlockSpec((1, TILE, QKV_W), lambda b, i: (b, i, 0)),
                   pl.BlockSpec((1, TILE, GROUP_W), lambda b, i: (b, i, 0))],
        out_shape=[_sds((n_batch, n_tok, QKV_W), BF16), _sds((n_batch, n_tok, GROUP_W), F32)],
        compiler_params=_params(("arbitrary", "arbitrary")),
        name="inproj",
    )(*xs, mod, g1, w_in, cos_t, sin_t, gq, gk, nq, nk)


def _gqa_kernel(flag_ref, qa_ref, qb_ref, k_ref, v_ref, o_ref, *, n_ctx, tile_off):
    def attend(n_keys, bounded):
        k = k_ref[0, :n_keys, :]
        v = v_ref[0, :n_keys, :]
        outs = []
        for q_ref in (qa_ref, qb_ref):
            q2 = _split_heads(q_ref[0])
            outs.append(_merge_heads(_softmax_pv(_nt_dot(q2, k), v, bounded), TILE))
        o_ref[0] = jnp.concatenate(outs, axis=-1).astype(o_ref.dtype)

    if tile_off == 0:
        is_ctx = pl.program_id(1) == 0
        pl.when(is_ctx)(lambda: _when_bounded(flag_ref, functools.partial(attend, n_ctx)))
        pl.when(jnp.logical_not(is_ctx))(lambda: _when_bounded(flag_ref, functools.partial(attend, k_ref.shape[1])))
    else:
        _when_bounded(flag_ref, functools.partial(attend, k_ref.shape[1]))


_SMEM_SPEC = pl.BlockSpec(memory_space=pltpu.SMEM)


def _gqa(bounded, qkv, *, need_ctx, n_ctx):
    n_batch, n_tok, _ = qkv.shape
    off = 0 if need_ctx else n_ctx // TILE
    q_spec = lambda col: pl.BlockSpec((1, TILE, LANES), lambda b, i: (b, i + off, col))
    kv_spec = lambda col: pl.BlockSpec((1, n_tok, LANES), lambda b, i: (b, 0, col))
    return pl.pallas_call(
        functools.partial(_gqa_kernel, n_ctx=n_ctx, tile_off=off),
        grid=(n_batch, n_tok // TILE - off),
        in_specs=[_SMEM_SPEC, q_spec(0), q_spec(1), kv_spec(2), kv_spec(3)],
        out_specs=pl.BlockSpec((1, TILE, GROUP_W), lambda b, i: (b, i, 0)),
        out_shape=_sds((n_batch, n_tok - off * TILE, GROUP_W), BF16),
        compiler_params=_params(("arbitrary", "arbitrary")),
        name="gqa",
    )(bounded, qkv, qkv, qkv, qkv)


def _na_kernel(flag_ref, q_ref, kp_ref, kc_ref, kn_ref, kx_ref, vp_ref, vc_ref, vn_ref, vx_ref, tbl_ref, o_ref, *, tile_off):
    def attend(local, bounded):
        outs = []
        for pair in range(2):
            cols = slice(pair * LANES, (pair + 1) * LANES)
            q2 = _split_heads(q_ref[0, :, cols])
            if local:
                k = jnp.concatenate([r[0, :, cols] for r in (kp_ref, kc_ref, kn_ref, kx_ref)], axis=0)
                v = jnp.concatenate([r[0, :, cols] for r in (vp_ref, vc_ref, vn_ref, vx_ref)], axis=0)
                bias = jnp.concatenate([tbl_ref[0, 2 * pair], tbl_ref[0, 2 * pair + 1]], axis=0)
                s = _nt_dot(q2, k)
                s = jnp.concatenate([s[:, :3 * TILE] + bias, s[:, 3 * TILE:]], axis=-1)
            else:
                k = kx_ref[0, :, cols]
                v = vx_ref[0, :, cols]
                s = _nt_dot(q2, k)
            outs.append(_merge_heads(_softmax_pv(s, v, bounded), TILE))
        o_ref[0] = jnp.concatenate(outs, axis=-1).astype(o_ref.dtype)

    if tile_off == 0:
        is_ctx = pl.program_id(1) == 0
        pl.when(is_ctx)(lambda: _when_bounded(flag_ref, functools.partial(attend, False)))
        pl.when(jnp.logical_not(is_ctx))(lambda: _when_bounded(flag_ref, functools.partial(attend, True)))
    else:
        _when_bounded(flag_ref, functools.partial(attend, True))


def _na(bounded, qkv, table, *, need_ctx, n_ctx):
    n_batch, n_tok, _ = qkv.shape
    assert n_ctx == TILE
    n_tiles = n_tok // TILE
    off = 0 if need_ctx else 1
    blk = (1, TILE, GROUP_W)
    cur = lambda col: pl.BlockSpec(blk, lambda b, i: (b, i + off, col))
    prev = lambda col: pl.BlockSpec(blk, lambda b, i: (b, jnp.maximum(i + off - 1, 1), col))
    nxt = lambda col: pl.BlockSpec(blk, lambda b, i: (b, jnp.minimum(i + off + 1, n_tiles - 1), col))
    ctx = lambda col: pl.BlockSpec(blk, lambda b, i: (b, 0, col))

    def cls(b, i):
        t = i + off
        return (jnp.where(t == n_tiles - 1, 2, jnp.where(t <= 1, 0, 1)), 0, 0, 0)

    return pl.pallas_call(
        functools.partial(_na_kernel, tile_off=off),
        grid=(n_batch, n_tiles - off),
        in_specs=[_SMEM_SPEC, cur(2), prev(3), cur(3), nxt(3), ctx(3), prev(4), cur(4), nxt(4), ctx(4),
                  pl.BlockSpec((1, NA_HEADS, TILE, 3 * TILE), cls)],
        out_specs=pl.BlockSpec(blk, lambda b, i: (b, i, 0)),
        out_shape=_sds((n_batch, n_tok - off * TILE, GROUP_W), BF16),
        compiler_params=_params(("arbitrary", "arbitrary")),
        name="na",
    )(bounded, *([qkv] * 9), table)


def _na_bias_table(rel_bias, rows):
    n_bands = rows // BAND_ROWS
    assert rows >= WIN_H and rows % BAND_ROWS == 0
    n_dr, n_dc = 2 * WIN_H - 1, 2 * WIN_W - 1
    w = GRID_W
    shift = w - WIN_W
    vp = jnp.pad(rel_bias.astype(F32) * LOG2E, ((0, 0), (0, 0), (shift, 2 * w - shift - n_dc)))
    skew = jnp.broadcast_to(vp[:, :, None, :], (NA_HEADS, n_dr, w, 2 * w)).reshape(NA_HEADS, n_dr, 2 * w * w)
    col = skew[:, :, :w * (2 * w - 1)].reshape(NA_HEADS, n_dr, w, 2 * w - 1)[:, :, :, w - 1:]
    qc = np.arange(w)
    cs = np.clip(qc - WIN_W // 2, 0, w - WIN_W)
    col_ok = (qc[None, :] >= cs[:, None]) & (qc[None, :] < cs[:, None] + WIN_W)
    col = jnp.where(col_ok[None, None], col, NEG_BIG)
    masked = jnp.full((NA_HEADS, w, w), NEG_BIG, F32)
    classes = []
    for band in (0, min(1, n_bands - 1), n_bands - 1):
        q_rows = []
        for i in range(BAND_ROWS):
            r = BAND_ROWS * band + i
            rs = min(max(r - WIN_H // 2, 0), rows - WIN_H)
            blocks = []
            for kk in range(3 * BAND_ROWS):
                kr = BAND_ROWS * (band - 1) + kk
                ok = 0 <= kr < rows and rs <= kr < rs + WIN_H
                blocks.append(col[:, kr - r + WIN_H - 1] if ok else masked)
            q_rows.append(jnp.concatenate(blocks, axis=-1))
        classes.append(jnp.concatenate(q_rows, axis=-2))
    return jnp.stack(classes)


def _fourier_kernel(x_ref, cs_ref, dft_ref, dftc_ref, w_ref, o_ref, x12_ref, *, n_ctx, n_lat, tile_off):
    t = pl.program_id(1) + tile_off

    def head_dft(x):
        x12 = jnp.dot(x, cs_ref[...], preferred_element_type=F32).astype(BF16)
        return jnp.concatenate([x12[:, :GROUP_W], x12[:, GROUP_W:]], axis=0)

    def finish(y, n):
        y = y * (1.0 / math.sqrt(n * FNO_HEAD_W))
        o_ref[0] = jnp.dot(y.astype(BF16), w_ref[...], preferred_element_type=F32).astype(o_ref.dtype)

    if tile_off == 0:
        @pl.when(t == 0)
        def _():
            x12 = head_dft(x_ref[0, :n_ctx, :])
            finish(jnp.dot(dftc_ref[...], x12, preferred_element_type=F32), n_ctx)

    @pl.when(t == n_ctx // TILE)
    def _():
        x12_ref[...] = head_dft(x_ref[0, n_ctx:, :])

    @pl.when(t >= n_ctx // TILE)
    def _():
        row0 = pl.multiple_of((t - n_ctx // TILE) * TILE, TILE)
        finish(jnp.dot(dft_ref[pl.ds(row0, TILE), :], x12_ref[...], preferred_element_type=F32), n_lat)


def _fourier(qkv, cs_bd, dft, dft_ctx, w_f, *, need_ctx, n_ctx):
    n_batch, n_tok, _ = qkv.shape
    n_lat = n_tok - n_ctx
    off = 0 if need_ctx else n_ctx // TILE
    const = lambda shape: pl.BlockSpec(shape, lambda b, i: (0,) * len(shape))
    return pl.pallas_call(
        functools.partial(_fourier_kernel, n_ctx=n_ctx, n_lat=n_lat, tile_off=off),
        grid=(n_batch, n_tok // TILE - off),
        in_specs=[pl.BlockSpec((1, n_tok, GROUP_W), lambda b, i: (b, 0, 5)),
                  const(cs_bd.shape), const(dft.shape), const(dft_ctx.shape), const(w_f.shape)],
        out_specs=pl.BlockSpec((1, TILE, GROUP_W), lambda b, i: (b, i, 0)),
        out_shape=_sds((n_batch, n_tok - off * TILE, GROUP_W), BF16),
        scratch_shapes=[pltpu.VMEM((2 * n_lat, GROUP_W), BF16)],
        compiler_params=_params(("arbitrary", "arbitrary")),
        name="fourier",
    )(qkv, cs_bd, dft, dft_ctx, w_f)


def _dft_matrix(n):
    f = 32
    assert n % f == 0
    k = lax.broadcasted_iota(jnp.int32, (n, 1), 0)
    m1 = lax.broadcasted_iota(jnp.int32, (1, n // f), 1)
    m0 = lax.broadcasted_iota(jnp.int32, (1, f), 1)
    ang_a = ((k * (m1 * f)) % n).astype(F32) * (2.0 * math.pi / n)
    ang_b = ((k * m0) % n).astype(F32) * (2.0 * math.pi / n)
    ca, sa = jnp.cos(ang_a)[:, :, None], jnp.sin(ang_a)[:, :, None]
    cb, sb = jnp.cos(ang_b)[:, None, :], jnp.sin(ang_b)[:, None, :]
    cos = (ca * cb - sa * sb).reshape(n, n)
    sin = (sa * cb + ca * sb).reshape(n, n)
    return jnp.concatenate([cos, -sin], axis=1).astype(BF16)


def _head_dft_matrix():
    a = np.arange(GROUP_W)
    same = (a[:, None] // FNO_HEAD_W) == (a[None, :] // FNO_HEAD_W)
    ang = 2.0 * np.pi * (((a[:, None] % FNO_HEAD_W) * (a[None, :] % FNO_HEAD_W)) % FNO_HEAD_W) / FNO_HEAD_W
    m = np.concatenate([np.where(same, np.cos(ang), 0.0), np.where(same, np.sin(ang), 0.0)], axis=1)
    return jnp.asarray(m, F32).astype(BF16)


def _s5_kernel(u_ref, a_ref, bd_ref, cd_ref, y_ref, bu_ref, hb_ref, h_ref, *, n_batch, reverse):
    @pl.when(pl.program_id(0) == 0)
    def _():
        h_ref[...] = jnp.zeros_like(h_ref)

    sub_rows = S5_SUB * n_batch
    chunk = 4 * LANES
    u = pltpu.einshape("btc->tbc", u_ref[...]).reshape(S5_STEPS * n_batch, GROUP_W).astype(BF16)

    def project(q):
        r = slice(q * sub_rows, (q + 1) * sub_rows)
        bu_ref[r, :] = jnp.dot(u[r], bd_ref[0], preferred_element_type=F32)

    def scan(q):
        steps = range(S5_SUB - 1, -1, -1) if reverse else range(S5_SUB)
        for sc in range(SSM_W // chunk):
            re = slice(sc * chunk, (sc + 1) * chunk)
            im = slice(SSM_W + sc * chunk, SSM_W + (sc + 1) * chunk)
            a_re = jnp.broadcast_to(a_ref[0, :, re], (n_batch, chunk))
            a_im = jnp.broadcast_to(a_ref[0, :, im], (n_batch, chunk))
            h_re, h_im = h_ref[:, re], h_ref[:, im]
            for t in steps:
                r = slice(q * sub_rows + t * n_batch, q * sub_rows + (t + 1) * n_batch)
                h_re, h_im = (a_re * h_re - a_im * h_im + bu_ref[r, re],
                              a_re * h_im + a_im * h_re + bu_ref[r, im])
                hb_ref[r, re] = h_re.astype(BF16)
                hb_ref[r, im] = h_im.astype(BF16)
            h_ref[:, re] = h_re
            h_ref[:, im] = h_im

    def readout(q):
        y = jnp.dot(hb_ref[q * sub_rows:(q + 1) * sub_rows, :], cd_ref[0], preferred_element_type=F32)
        y = pltpu.einshape("tbc->btc", y.reshape(S5_SUB, n_batch, GROUP_W))
        y_ref[:, q * S5_SUB:(q + 1) * S5_SUB, :] = y.astype(y_ref.dtype)

    order = list(range(S5_STEPS // S5_SUB))
    if reverse:
        order.reverse()
    project(order[0])
    for j, q in enumerate(order):
        if j + 1 < len(order):
            project(order[j + 1])
        scan(q)
        if j > 0:
            readout(order[j - 1])
    readout(order[-1])


def _s5(u, a_tab, b_dense, c_dense, *, n_ctx, reverse):
    n_batch, n_tok, _ = u.shape
    n_blk = n_tok // S5_STEPS
    n_cblk = n_ctx // S5_STEPS
    d = int(reverse)
    if reverse:
        blk = lambda s: jnp.where(s < n_cblk, n_cblk - 1 - s, n_blk - 1 - (s - n_cblk))
    else:
        blk = lambda s: s
    return pl.pallas_call(
        functools.partial(_s5_kernel, n_batch=n_batch, reverse=reverse),
        grid=(n_blk,),
        in_specs=[pl.BlockSpec((n_batch, S5_STEPS, GROUP_W), lambda s: (0, blk(s), 0)),
                  pl.BlockSpec((1, 1, 2 * SSM_W), lambda s: (d, 0, 0)),
                  pl.BlockSpec((1, GROUP_W, 2 * SSM_W), lambda s: (d, 0, 0)),
                  pl.BlockSpec((1, 2 * SSM_W, GROUP_W), lambda s: (d, 0, 0))],
        out_specs=pl.BlockSpec((n_batch, S5_STEPS, GROUP_W), lambda s: (0, blk(s), 0)),
        out_shape=_sds((n_batch, n_tok, GROUP_W), BF16),
        scratch_shapes=[pltpu.VMEM((S5_STEPS * n_batch, 2 * SSM_W), F32),
                        pltpu.VMEM((S5_STEPS * n_batch, 2 * SSM_W), BF16),
                        pltpu.VMEM((n_batch, 2 * SSM_W), F32)],
        compiler_params=_params(("arbitrary",)),
        name="s5",
    )(u, a_tab, b_dense, c_dense)


def _s5_tables(lam_re, lam_im, log_dt, b_re, b_im, c_re, c_im):
    lam = lax.complex(lam_re.astype(F32), lam_im.astype(F32))
    dt = jnp.exp(log_dt.astype(F32))[..., None]
    a_bar = jnp.exp(lam * dt)
    b_bar = ((a_bar - 1.0) / lam)[..., None] * lax.complex(b_re.astype(F32), b_im.astype(F32))
    eye = jnp.eye(SSM_GROUPS, dtype=F32)

    def embed_b(m):
        m = jnp.transpose(m, (0, 1, 3, 2))[:, :, :, None, :] * eye[None, :, None, :, None]
        return m.reshape(2, GROUP_W, SSM_W)

    def embed_c(m):
        m = jnp.transpose(m, (0, 1, 3, 2))[:, :, :, None, :] * eye[None, :, None, :, None]
        return m.reshape(2, SSM_W, GROUP_W)

    a_tab = jnp.concatenate([jnp.real(a_bar).reshape(2, 1, SSM_W), jnp.imag(a_bar).reshape(2, 1, SSM_W)], axis=-1)
    b_dense = jnp.concatenate([embed_b(jnp.real(b_bar)), embed_b(jnp.imag(b_bar))], axis=-1).astype(BF16)
    c_dense = jnp.concatenate([embed_c(c_re.astype(F32)), -embed_c(c_im.astype(F32))], axis=1).astype(BF16)
    return a_tab, b_dense, c_dense


def _gelu_tanh(x):
    return x * (0.5 * (1.0 + jnp.tanh(math.sqrt(2.0 / math.pi) * (x + 0.044715 * (x * x * x)))))


def _mixout_kernel(*refs, split_input, tiles_per_batch, tile_off):
    if split_input:
        x_ref, ctx_ref, *refs = refs
    else:
        x_ref, *refs = refs
    (mod_ref, oa_ref, od_ref, of_ref, u_ref, yf_ref, yb_ref, d_ref, wg_ref, bg_ref, gg_ref, wo_ref,
     g2_ref, w1_ref, w3_ref, w2_ref, o_ref, x1_ref, h_ref, ga2_ref) = refs
    s = pl.program_id(0)
    n_items = pl.num_programs(0) - 1

    def feed_forward():
        h = h_ref[...]
        acc = jnp.zeros((TILE, D_MODEL), F32)
        for c in range(D_FF // FF_CHUNK):
            cols = slice(c * FF_CHUNK, (c + 1) * FF_CHUNK)
            a = jnp.dot(h, w1_ref[:, cols], preferred_element_type=F32)
            b = jnp.dot(h, w3_ref[:, cols], preferred_element_type=F32)
            t = (a * jax.nn.sigmoid(a)) * b
            acc = acc + jnp.dot(t.astype(BF16), w2_ref[cols, :], preferred_element_type=F32)
        o_ref[0] = x1_ref[...] + ga2_ref[...] * acc

    def mix():
        x = x_ref[0]
        if split_input:
            is_ctx = lax.rem(jnp.minimum(s, n_items - 1), tiles_per_batch) + tile_off == 0
            x = jnp.where(is_ctx, ctx_ref[0], x)
        _, _, ga1 = _mod_slices(mod_ref[0], 0)
        sh2, sc2, ga2 = _mod_slices(mod_ref[0], 3)
        y = u_ref[0] * d_ref[...] + yf_ref[0].astype(F32) + yb_ref[0].astype(F32)
        g = _gelu_tanh(y)
        z = jnp.dot(g.astype(BF16), wg_ref[...], preferred_element_type=F32) + bg_ref[...]
        o_s = g * jax.nn.sigmoid(z)
        parts = [oa_ref[0].astype(F32), od_ref[0].astype(F32), of_ref[0].astype(F32), o_s]
        normed = [(_rms_rows(p) * gg_ref[:, k * GROUP_W:(k + 1) * GROUP_W]).astype(BF16)
                  for k, p in enumerate(parts)]
        r = jnp.dot(jnp.concatenate(normed, axis=-1), wo_ref[...], preferred_element_type=F32)
        x = x + ga1 * r
        x1_ref[...] = x
        h_ref[...] = ((_rms_rows(x) * g2_ref[...]) * (1.0 + sc2) + sh2).astype(BF16)
        ga2_ref[...] = ga2

    pl.when(s == 0)(mix)

    @pl.when(jnp.logical_and(s > 0, s < n_items))
    def _():
        feed_forward()
        mix()

    pl.when(s == n_items)(feed_forward)


def _mixout_ffn(xs, mod, o_a, o_d, o_f, u, y_fwd, y_bwd, d_skip, w_glu, b_glu, g_group, w_out, g2, w1, w3, w2, *,
                need_ctx, n_ctx):
    split_input = len(xs) == 2
    n_batch, n_tok, _ = u.shape
    off = 0 if need_ctx else n_ctx // TILE
    n_out = n_tok - off * TILE
    nt = n_out // TILE
    n_items = n_batch * nt
    tok = (1, TILE, D_MODEL)

    def item(fn):
        def index_map(s):
            a = jnp.minimum(s, n_items - 1)
            return fn(a // nt, a % nt)
        return index_map

    def out_map(s):
        a = jnp.maximum(s - 1, 0)
        return (a // nt, a % nt, 0)

    if split_input:
        assert need_ctx
        x_specs = [pl.BlockSpec(tok, item(lambda b, i: (b, jnp.maximum(i - 1, 0), 0))),
                   pl.BlockSpec(tok, item(lambda b, i: (b, 0, 0)))]
    else:
        x_specs = [pl.BlockSpec(tok, item(lambda b, i: (b, i + off, 0)))]
    grp = pl.BlockSpec((1, TILE, GROUP_W), item(lambda b, i: (b, i + off, 0)))
    mix = pl.BlockSpec((1, TILE, GROUP_W), item(lambda b, i: (b, i, 0)))
    const = lambda shape: pl.BlockSpec(shape, lambda s: (0,) * len(shape), pipeline_mode=pl.Buffered(1))
    return pl.pallas_call(
        functools.partial(_mixout_kernel, split_input=split_input, tiles_per_batch=nt, tile_off=off),
        grid=(n_items + 1,),
        in_specs=x_specs + [
            pl.BlockSpec((1, 1, N_MOD * D_MODEL), item(lambda b, i: (jnp.where(i + off == 0, n_batch, b), 0, 0))),
            mix, mix, mix, grp, grp, grp,
            const((1, GROUP_W)), const((GROUP_W, GROUP_W)), const((1, GROUP_W)),
            const((1, D_MODEL)), const((D_MODEL, D_MODEL)),
            const((1, D_MODEL)), const(w1.shape), const(w3.shape), const(w2.shape),
        ],
        out_specs=pl.BlockSpec(tok, out_map),
        out_shape=_sds((n_batch, n_out, D_MODEL), F32),
        scratch_shapes=[pltpu.VMEM((TILE, D_MODEL), F32), pltpu.VMEM((TILE, D_MODEL), BF16),
                        pltpu.VMEM((1, D_MODEL), F32)],
        compiler_params=_params(("arbitrary",)),
        name="mixffn",
    )(*xs, mod, o_a, o_d, o_f, u, y_fwd, y_bwd, d_skip, w_glu, b_glu, g_group, w_out, g2, w1, w3, w2)


def _rope_tables(n_lat, n_ctx):
    t = jnp.arange(n_lat, dtype=jnp.int32)
    rows = (t // GRID_W).astype(F32)
    cols = (t % GRID_W).astype(F32)
    axis_dim = HEAD_DIM // 2
    inv_freq = ROPE_THETA ** (-jnp.arange(0, axis_dim, 2, dtype=F32) / axis_dim)
    ang_r = rows[:, None] * inv_freq[None, :]
    ang_c = cols[:, None] * inv_freq[None, :]
    cos = jnp.concatenate([jnp.cos(ang_r)] * 2 + [jnp.cos(ang_c)] * 2, axis=-1)
    sin = jnp.concatenate([-jnp.sin(ang_r), jnp.sin(ang_r), -jnp.sin(ang_c), jnp.sin(ang_c)], axis=-1)
    cos = jnp.concatenate([jnp.ones((n_ctx, HEAD_DIM), F32), cos], axis=0)
    sin = jnp.concatenate([jnp.zeros((n_ctx, HEAD_DIM), F32), sin], axis=0)
    return jnp.tile(cos, (1, 2)), jnp.tile(sin, (1, 2))


def kernel(x, c, ctx, c_ctx, w_mod, b_mod, g_norm1, w_in, att_q_gain, att_k_gain, na_q_gain, na_k_gain, na_rel_bias, w_fourier, ssm_lam_re, ssm_lam_im, ssm_log_dt, ssm_b_re, ssm_b_im, ssm_c_re, ssm_c_im, ssm_d, w_glu, b_glu, g_group, w_out, g_norm2, w_ff1, w_ff3, w_ff2):
    n_batch, n_lat, d = x.shape
    n_ctx = ctx.shape[1]
    n_tok = n_lat + n_ctx
    assert d == D_MODEL and n_ctx == TILE and n_lat % TILE == 0 and n_lat % GRID_W == 0
    assert n_batch % 8 == 0 and n_tok % S5_STEPS == 0

    mod_rows = -(-(n_batch + 1) // 8) * 8
    cc = jnp.concatenate([c, c_ctx[None, :], jnp.zeros((mod_rows - n_batch - 1, d), F32)], axis=0)
    mod = _modulation(cc, w_mod, b_mod.reshape(DEPTH, 1, N_MOD * d)).reshape(DEPTH, mod_rows, 1, N_MOD * d)

    cos_t, sin_t = _rope_tables(n_lat, n_ctx)
    cs_bd = _head_dft_matrix()
    dft = _dft_matrix(n_lat)
    dft_ctx = _dft_matrix(n_ctx)
    def q_perm(a, axis):
        heads = [lax.slice_in_dim(a, h * HEAD_DIM, (h + 1) * HEAD_DIM, axis=axis) for h in _Q_ORDER]
        return jnp.concatenate(heads + [lax.slice_in_dim(a, GROUP_W, a.shape[axis], axis=axis)], axis=axis)

    scale = HEAD_DIM ** -0.5 * LOG2E
    tile2 = lambda g, s=1.0: jnp.tile(g.astype(F32) * s, 2)[None, :]
    row = lambda v: v.astype(F32)[None, :]

    xs = (x, ctx)
    for l in range(DEPTH):
        need_ctx = l < DEPTH - 1
        qkv, u = _inproj(xs, mod[l], row(g_norm1[l]), q_perm(w_in[l], 1).astype(BF16), cos_t, sin_t,
                         tile2(att_q_gain[l], scale), tile2(att_k_gain[l]),
                         tile2(na_q_gain[l], scale), tile2(na_k_gain[l]),
                         n_batch=n_batch, n_tok=n_tok)
        o_a = _gqa(_logits_bounded(att_q_gain[l], att_k_gain[l]), qkv, need_ctx=need_ctx, n_ctx=n_ctx)
        o_d = _na(_logits_bounded(na_q_gain[l], na_k_gain[l], jnp.max(jnp.abs(na_rel_bias[l]))), qkv,
                  _na_bias_table(na_rel_bias[l], n_lat // GRID_W), need_ctx=need_ctx, n_ctx=n_ctx)
        o_f = _fourier(qkv, cs_bd, dft, dft_ctx, w_fourier[l].astype(BF16), need_ctx=need_ctx, n_ctx=n_ctx)
        a_tab, b_dense, c_dense = _s5_tables(ssm_lam_re[l], ssm_lam_im[l], ssm_log_dt[l], ssm_b_re[l],
                                             ssm_b_im[l], ssm_c_re[l], ssm_c_im[l])
        y_fwd = _s5(u, a_tab, b_dense, c_dense, n_ctx=n_ctx, reverse=False)
        y_bwd = _s5(u, a_tab, b_dense, c_dense, n_ctx=n_ctx, reverse=True)
        x2 = _mixout_ffn(xs, mod[l], o_a, o_d, o_f, u, y_fwd, y_bwd, row(ssm_d[l]), w_glu[l].astype(BF16),
                         row(b_glu[l]), row(q_perm(g_group[l], 0)), q_perm(w_out[l], 0).astype(BF16),
                         row(g_norm2[l]), w_ff1[l].astype(BF16), w_ff3[l].astype(BF16), w_ff2[l].astype(BF16),
                         need_ctx=need_ctx, n_ctx=n_ctx)
        xs = (x2,)
    return xs[0]
```

```python
import functools
import math

import numpy as np
import jax
import jax.numpy as jnp
from jax import lax
from jax.experimental import pallas as pl
from jax.experimental.pallas import tpu as pltpu

F32 = jnp.float32
BF16 = jnp.bfloat16

D_MODEL = 1024
DEPTH = 2
GRID_W = 64
HEAD_DIM = 64
N_GROUPS = 4
GROUP_W = D_MODEL // N_GROUPS
WIN_H = 8
WIN_W = 16
NA_HEADS = GROUP_W // HEAD_DIM
FNO_HEAD_W = 64
SSM_GROUP = 16
SSM_GROUPS = GROUP_W // SSM_GROUP
SSM_STATE = 64
SSM_W = SSM_GROUPS * SSM_STATE
ROPE_THETA = 10000.0
D_FF = ((8 * D_MODEL // 3 + 255) // 256) * 256
N_MOD = 6
EPS = 1e-6
IN_W = 7 * GROUP_W
QKV_W = 6 * GROUP_W

LANES = 128
TILE = 256
BAND_ROWS = TILE // GRID_W
S5_STEPS = 128
S5_SUB = 32
FF_CHUNK = D_FF // 2
VMEM_LIMIT_BYTES = 56 * 2**20
NEG_BIG = -1e30
LOG2E = 1.0 / math.log(2.0)
EXP2_SAFE = 60.0

_Q_ORDER = (0, 2, 1, 3)


def _params(sem):
    return pltpu.CompilerParams(dimension_semantics=sem, vmem_limit_bytes=VMEM_LIMIT_BYTES)


def _sds(shape, dtype):
    return jax.ShapeDtypeStruct(shape, dtype)


def _mod_kernel(c_ref, w_ref, b_ref, o_ref):
    c = c_ref[...]
    s = c * jax.nn.sigmoid(c)
    w = w_ref[0]
    s_hi, w_hi = s.astype(BF16), w.astype(BF16)
    s_lo = (s - s_hi.astype(F32)).astype(BF16)
    w_lo = (w - w_hi.astype(F32)).astype(BF16)
    dot = functools.partial(jnp.dot, preferred_element_type=F32)
    o_ref[0] = dot(s_hi, w_hi) + (dot(s_lo, w_hi) + dot(s_hi, w_lo)) + b_ref[0]


def _modulation(cc, w_mod, b_mod):
    rows, d = cc.shape
    n_blk = 4
    bw = N_MOD * d // n_blk
    return pl.pallas_call(
        _mod_kernel,
        grid=(DEPTH, n_blk),
        in_specs=[
            pl.BlockSpec((rows, d), lambda l, j: (0, 0)),
            pl.BlockSpec((1, d, bw), lambda l, j: (l, 0, j)),
            pl.BlockSpec((1, 1, bw), lambda l, j: (l, 0, j)),
        ],
        out_specs=pl.BlockSpec((1, rows, bw), lambda l, j: (l, 0, j)),
        out_shape=_sds((DEPTH, rows, N_MOD * d), F32),
        compiler_params=_params(("arbitrary", "arbitrary")),
        name="modulation",
    )(cc, w_mod, b_mod)


def _rms_rows(x):
    return x * lax.rsqrt(jnp.mean(x * x, axis=-1, keepdims=True) + EPS)


def _mod_slices(m, first):
    return [m[:, (first + k) * D_MODEL:(first + k + 1) * D_MODEL] for k in range(3)]


def _head_norm(t, gain):
    lo = lax.broadcasted_iota(jnp.int32, t.shape, 1) < HEAD_DIM
    sq = t * t
    s_lo = jnp.sum(jnp.where(lo, sq, 0.0), axis=-1, keepdims=True)
    s_hi = jnp.sum(jnp.where(lo, 0.0, sq), axis=-1, keepdims=True)
    ms = jnp.where(lo, s_lo, s_hi) * (1.0 / HEAD_DIM)
    return t * lax.rsqrt(ms + EPS) * gain


def _rope(t, cos, sin):
    first = (lax.broadcasted_iota(jnp.int32, t.shape, 1) & 16) == 0
    partner = jnp.where(first, pltpu.roll(t, LANES - 16, axis=1), pltpu.roll(t, 16, axis=1))
    return t * cos + partner * sin


def _softmax_pv(s, v, bounded):
    p = jnp.exp2(s if bounded else s - jnp.max(s, axis=-1, keepdims=True))
    l = jnp.sum(p, axis=-1, keepdims=True)
    return jnp.dot(p.astype(BF16), v, preferred_element_type=F32) / l


def _when_bounded(flag_ref, body):
    bounded = flag_ref[0] == 1
    pl.when(bounded)(lambda: body(True))
    pl.when(jnp.logical_not(bounded))(lambda: body(False))


def _logits_bounded(gq, gk, extra=0.0):
    bound = math.sqrt(HEAD_DIM) * jnp.max(jnp.abs(gq)) * jnp.max(jnp.abs(gk)) + extra
    return (bound * LOG2E <= EXP2_SAFE).astype(jnp.int32).reshape(1)


def _split_heads(q):
    lo = lax.broadcasted_iota(jnp.int32, q.shape, 1) < HEAD_DIM
    zero = jnp.zeros_like(q)
    return jnp.concatenate([jnp.where(lo, q, zero), jnp.where(lo, zero, q)], axis=0)


def _merge_heads(o, rows):
    lo = lax.broadcasted_iota(jnp.int32, (rows, LANES), 1) < HEAD_DIM
    return jnp.where(lo, o[:rows], o[rows:])


def _nt_dot(a, b):
    return lax.dot_general(a, b, (((1,), (1,)), ((), ())), preferred_element_type=F32)


def _inproj_kernel(*refs, split_input):
    if split_input:
        x_ref, ctx_ref, *refs = refs
    else:
        x_ref, *refs = refs
    (mod_ref, g1_ref, w_ref, cos_ref, sin_ref, gq_ref, gk_ref, nq_ref, nk_ref, qkv_ref, u_ref) = refs
    x = x_ref[0]
    if split_input:
        x = jnp.where(pl.program_id(1) == 0, ctx_ref[0], x)
    sh1, sc1, _ = _mod_slices(mod_ref[0], 0)
    h = (_rms_rows(x) * g1_ref[...]) * (1.0 + sc1) + sh1
    p = jnp.dot(h.astype(BF16), w_ref[...], preferred_element_type=F32)
    cos = cos_ref[...]
    sin = sin_ref[...]

    def tile(j):
        return p[:, j * LANES:(j + 1) * LANES]

    out = [
        _rope(_head_norm(tile(0), gq_ref[...]), cos, sin),
        _rope(_head_norm(tile(1), gq_ref[...]), cos, sin),
        _rope(_head_norm(tile(2), gk_ref[...]), cos, sin),
        tile(3),
        _head_norm(tile(4), nq_ref[...]),
        _head_norm(tile(5), nq_ref[...]),
        _head_norm(tile(6), nk_ref[...]),
        _head_norm(tile(7), nk_ref[...]),
        tile(8), tile(9), tile(10), tile(11),
    ]
    for j, t in enumerate(out):
        qkv_ref[0, :, j * LANES:(j + 1) * LANES] = t.astype(BF16)
    u_ref[0] = p[:, QKV_W:]


def _inproj(xs, mod, g1, w_in, cos_t, sin_t, gq, gk, nq, nk, *, n_batch, n_tok):
    split_input = len(xs) == 2
    n_tiles = n_tok // TILE
    tok = (1, TILE, D_MODEL)
    if split_input:
        x_specs = [pl.BlockSpec(tok, lambda b, i: (b, jnp.maximum(i - 1, 0), 0)),
                   pl.BlockSpec(tok, lambda b, i: (b, 0, 0))]
    else:
        x_specs = [pl.BlockSpec(tok, lambda b, i: (b, i, 0))]
    vec = lambda w: pl.BlockSpec((1, w), lambda b, i: (0, 0))
    return pl.pallas_call(
        functools.partial(_inproj_kernel, split_input=split_input),
        grid=(n_batch, n_tiles),
        in_specs=x_specs + [
            pl.BlockSpec((1, 1, N_MOD * D_MODEL), lambda b, i: (jnp.where(i == 0, n_batch, b), 0, 0)),
            vec(D_MODEL),
            pl.BlockSpec((D_MODEL, IN_W), lambda b, i: (0, 0)),
            pl.BlockSpec((TILE, LANES), lambda b, i: (i, 0)),
            pl.BlockSpec((TILE, LANES), lambda b, i: (i, 0)),
            vec(LANES), vec(LANES), vec(LANES), vec(LANES),
        ],
        out_specs=[pl.BlockSpec((1, TILE, QKV_W), lambda b, i: (b, i, 0)),
                   pl.BlockSpec((1, TILE, GROUP_W), lambda b, i: (b, i, 0))],
        out_shape=[_sds((n_batch, n_tok, QKV_W), BF16), _sds((n_batch, n_tok, GROUP_W), F32)],
        compiler_params=_params(("arbitrary", "arbitrary")),
        name="inproj",
    )(*xs, mod, g1, w_in, cos_t, sin_t, gq, gk, nq, nk)


def _gqa_kernel(flag_ref, qa_ref, qb_ref, k_ref, v_ref, o_ref, *, n_ctx, tile_off):
    def attend(n_keys, bounded):
        k = k_ref[0, :n_keys, :]
        v = v_ref[0, :n_keys, :]
        outs = []
        for q_ref in (qa_ref, qb_ref):
            q2 = _split_heads(q_ref[0])
            outs.append(_merge_heads(_softmax_pv(_nt_dot(q2, k), v, bounded), TILE))
        o_ref[0] = jnp.concatenate(outs, axis=-1).astype(o_ref.dtype)

    if tile_off == 0:
        is_ctx = pl.program_id(1) == 0
        pl.when(is_ctx)(lambda: _when_bounded(flag_ref, functools.partial(attend, n_ctx)))
        pl.when(jnp.logical_not(is_ctx))(lambda: _when_bounded(flag_ref, functools.partial(attend, k_ref.shape[1])))
    else:
        _when_bounded(flag_ref, functools.partial(attend, k_ref.shape[1]))


_SMEM_SPEC = pl.BlockSpec(memory_space=pltpu.SMEM)


def _gqa(bounded, qkv, *, need_ctx, n_ctx):
    n_batch, n_tok, _ = qkv.shape
    off = 0 if need_ctx else n_ctx // TILE
    q_spec = lambda col: pl.BlockSpec((1, TILE, LANES), lambda b, i: (b, i + off, col))
    kv_spec = lambda col: pl.BlockSpec((1, n_tok, LANES), lambda b, i: (b, 0, col))
    return pl.pallas_call(
        functools.partial(_gqa_kernel, n_ctx=n_ctx, tile_off=off),
        grid=(n_batch, n_tok // TILE - off),
        in_specs=[_SMEM_SPEC, q_spec(0), q_spec(1), kv_spec(2), kv_spec(3)],
        out_specs=pl.BlockSpec((1, TILE, GROUP_W), lambda b, i: (b, i, 0)),
        out_shape=_sds((n_batch, n_tok - off * TILE, GROUP_W), BF16),
        compiler_params=_params(("arbitrary", "arbitrary")),
        name="gqa",
    )(bounded, qkv, qkv, qkv, qkv)


def _na_kernel(flag_ref, q_ref, kp_ref, kc_ref, kn_ref, kx_ref, vp_ref, vc_ref, vn_ref, vx_ref, tbl_ref, o_ref, *, tile_off):
    def attend(local, bounded):
        outs = []
        for pair in range(2):
            cols = slice(pair * LANES, (pair + 1) * LANES)
            q2 = _split_heads(q_ref[0, :, cols])
            if local:
                k = jnp.concatenate([r[0, :, cols] for r in (kp_ref, kc_ref, kn_ref, kx_ref)], axis=0)
                v = jnp.concatenate([r[0, :, cols] for r in (vp_ref, vc_ref, vn_ref, vx_ref)], axis=0)
                bias = jnp.concatenate([tbl_ref[0, 2 * pair], tbl_ref[0, 2 * pair + 1]], axis=0)
                s = _nt_dot(q2, k)
                s = jnp.concatenate([s[:, :3 * TILE] + bias, s[:, 3 * TILE:]], axis=-1)
            else:
                k = kx_ref[0, :, cols]
                v = vx_ref[0, :, cols]
                s = _nt_dot(q2, k)
            outs.append(_merge_heads(_softmax_pv(s, v, bounded), TILE))
        o_ref[0] = jnp.concatenate(outs, axis=-1).astype(o_ref.dtype)

    if tile_off == 0:
        is_ctx = pl.program_id(1) == 0
        pl.when(is_ctx)(lambda: _when_bounded(flag_ref, functools.partial(attend, False)))
        pl.when(jnp.logical_not(is_ctx))(lambda: _when_bounded(flag_ref, functools.partial(attend, True)))
    else:
        _when_bounded(flag_ref, functools.partial(attend, True))


def _na(bounded, qkv, table, *, need_ctx, n_ctx):
    n_batch, n_tok, _ = qkv.shape
    assert n_ctx == TILE
    n_tiles = n_tok // TILE
    off = 0 if need_ctx else 1
    blk = (1, TILE, GROUP_W)
    cur = lambda col: pl.BlockSpec(blk, lambda b, i: (b, i + off, col))
    prev = lambda col: pl.BlockSpec(blk, lambda b, i: (b, jnp.maximum(i + off - 1, 1), col))
    nxt = lambda col: pl.BlockSpec(blk, lambda b, i: (b, jnp.minimum(i + off + 1, n_tiles - 1), col))
    ctx = lambda col: pl.BlockSpec(blk, lambda b, i: (b, 0, col))

    def cls(b, i):
        t = i + off
        return (jnp.where(t == n_tiles - 1, 2, jnp.where(t <= 1, 0, 1)), 0, 0, 0)

    return pl.pallas_call(
        functools.partial(_na_kernel, tile_off=off),
        grid=(n_batch, n_tiles - off),
        in_specs=[_SMEM_SPEC, cur(2), prev(3), cur(3), nxt(3), ctx(3), prev(4), cur(4), nxt(4), ctx(4),
                  pl.BlockSpec((1, NA_HEADS, TILE, 3 * TILE), cls)],
        out_specs=pl.BlockSpec(blk, lambda b, i: (b, i, 0)),
        out_shape=_sds((n_batch, n_tok - off * TILE, GROUP_W), BF16),
        compiler_params=_params(("arbitrary", "arbitrary")),
        name="na",
    )(bounded, *([qkv] * 9), table)


def _na_bias_table(rel_bias, rows):
    n_bands = rows // BAND_ROWS
    assert rows >= WIN_H and rows % BAND_ROWS == 0
    n_dr, n_dc = 2 * WIN_H - 1, 2 * WIN_W - 1
    w = GRID_W
    shift = w - WIN_W
    vp = jnp.pad(rel_bias.astype(F32) * LOG2E, ((0, 0), (0, 0), (shift, 2 * w - shift - n_dc)))
    skew = jnp.broadcast_to(vp[:, :, None, :], (NA_HEADS, n_dr, w, 2 * w)).reshape(NA_HEADS, n_dr, 2 * w * w)
    col = skew[:, :, :w * (2 * w - 1)].reshape(NA_HEADS, n_dr, w, 2 * w - 1)[:, :, :, w - 1:]
    qc = np.arange(w)
    cs = np.clip(qc - WIN_W // 2, 0, w - WIN_W)
    col_ok = (qc[None, :] >= cs[:, None]) & (qc[None, :] < cs[:, None] + WIN_W)
    col = jnp.where(col_ok[None, None], col, NEG_BIG)
    masked = jnp.full((NA_HEADS, w, w), NEG_BIG, F32)
    classes = []
    for band in (0, min(1, n_bands - 1), n_bands - 1):
        q_rows = []
        for i in range(BAND_ROWS):
            r = BAND_ROWS * band + i
            rs = min(max(r - WIN_H // 2, 0), rows - WIN_H)
            blocks = []
            for kk in range(3 * BAND_ROWS):
                kr = BAND_ROWS * (band - 1) + kk
                ok = 0 <= kr < rows and rs <= kr < rs + WIN_H
                blocks.append(col[:, kr - r + WIN_H - 1] if ok else masked)
            q_rows.append(jnp.concatenate(blocks, axis=-1))
        classes.append(jnp.concatenate(q_rows, axis=-2))
    return jnp.stack(classes)


def _fourier_kernel(x_ref, cs_ref, dft_ref, dftc_ref, w_ref, o_ref, x12_ref, *, n_ctx, n_lat, tile_off):
    t = pl.program_id(1) + tile_off

    def head_dft(x):
        x12 = jnp.dot(x, cs_ref[...], preferred_element_type=F32).astype(BF16)
        return jnp.concatenate([x12[:, :GROUP_W], x12[:, GROUP_W:]], axis=0)

    def finish(y, n):
        y = y * (1.0 / math.sqrt(n * FNO_HEAD_W))
        o_ref[0] = jnp.dot(y.astype(BF16), w_ref[...], preferred_element_type=F32).astype(o_ref.dtype)

    if tile_off == 0:
        @pl.when(t == 0)
        def _():
            x12 = head_dft(x_ref[0, :n_ctx, :])
            finish(jnp.dot(dftc_ref[...], x12, preferred_element_type=F32), n_ctx)

    @pl.when(t == n_ctx // TILE)
    def _():
        x12_ref[...] = head_dft(x_ref[0, n_ctx:, :])

    @pl.when(t >= n_ctx // TILE)
    def _():
        row0 = pl.multiple_of((t - n_ctx // TILE) * TILE, TILE)
        finish(jnp.dot(dft_ref[pl.ds(row0, TILE), :], x12_ref[...], preferred_element_type=F32), n_lat)


def _fourier(qkv, cs_bd, dft, dft_ctx, w_f, *, need_ctx, n_ctx):
    n_batch, n_tok, _ = qkv.shape
    n_lat = n_tok - n_ctx
    off = 0 if need_ctx else n_ctx // TILE
    const = lambda shape: pl.BlockSpec(shape, lambda b, i: (0,) * len(shape))
    return pl.pallas_call(
        functools.partial(_fourier_kernel, n_ctx=n_ctx, n_lat=n_lat, tile_off=off),
        grid=(n_batch, n_tok // TILE - off),
        in_specs=[pl.BlockSpec((1, n_tok, GROUP_W), lambda b, i: (b, 0, 5)),
                  const(cs_bd.shape), const(dft.shape), const(dft_ctx.shape), const(w_f.shape)],
        out_specs=pl.BlockSpec((1, TILE, GROUP_W), lambda b, i: (b, i, 0)),
        out_shape=_sds((n_batch, n_tok - off * TILE, GROUP_W), BF16),
        scratch_shapes=[pltpu.VMEM((2 * n_lat, GROUP_W), BF16)],
        compiler_params=_params(("arbitrary", "arbitrary")),
        name="fourier",
    )(qkv, cs_bd, dft, dft_ctx, w_f)


def _dft_matrix(n):
    f = 32
    assert n % f == 0
    k = lax.broadcasted_iota(jnp.int32, (n, 1), 0)
    m1 = lax.broadcasted_iota(jnp.int32, (1, n // f), 1)
    m0 = lax.broadcasted_iota(jnp.int32, (1, f), 1)
    ang_a = ((k * (m1 * f)) % n).astype(F32) * (2.0 * math.pi / n)
    ang_b = ((k * m0) % n).astype(F32) * (2.0 * math.pi / n)
    ca, sa = jnp.cos(ang_a)[:, :, None], jnp.sin(ang_a)[:, :, None]
    cb, sb = jnp.cos(ang_b)[:, None, :], jnp.sin(ang_b)[:, None, :]
    cos = (ca * cb - sa * sb).reshape(n, n)
    sin = (sa * cb + ca * sb).reshape(n, n)
    return jnp.concatenate([cos, -sin], axis=1).astype(BF16)


def _head_dft_matrix():
    a = np.arange(GROUP_W)
    same = (a[:, None] // FNO_HEAD_W) == (a[None, :] // FNO_HEAD_W)
    ang = 2.0 * np.pi * (((a[:, None] % FNO_HEAD_W) * (a[None, :] % FNO_HEAD_W)) % FNO_HEAD_W) / FNO_HEAD_W
    m = np.concatenate([np.where(same, np.cos(ang), 0.0), np.where(same, np.sin(ang), 0.0)], axis=1)
    return jnp.asarray(m, F32).astype(BF16)


def _s5_kernel(u_ref, a_ref, bd_ref, cd_ref, y_ref, bu_ref, hb_ref, h_ref, *, n_batch, reverse):
    @pl.when(pl.program_id(0) == 0)
    def _():
        h_ref[...] = jnp.zeros_like(h_ref)

    sub_rows = S5_SUB * n_batch
    chunk = 4 * LANES
    def project(q):
        u = pltpu.einshape("btc->tbc", u_ref[:, q * S5_SUB:(q + 1) * S5_SUB, :])
        u = u.reshape(sub_rows, GROUP_W).astype(BF16)
        bu_ref[q * sub_rows:(q + 1) * sub_rows, :] = jnp.dot(u, bd_ref[0], preferred_element_type=F32)

    def scan(q):
        steps = range(S5_SUB - 1, -1, -1) if reverse else range(S5_SUB)
        for sc in range(SSM_W // chunk):
            re = slice(sc * chunk, (sc + 1) * chunk)
            im = slice(SSM_W + sc * chunk, SSM_W + (sc + 1) * chunk)
            a_re = jnp.broadcast_to(a_ref[0, :, re], (n_batch, chunk))
            a_im = jnp.broadcast_to(a_ref[0, :, im], (n_batch, chunk))
            h_re, h_im = h_ref[:, re], h_ref[:, im]
            for t in steps:
                r = slice(q * sub_rows + t * n_batch, q * sub_rows + (t + 1) * n_batch)
                h_re, h_im = (a_re * h_re - a_im * h_im + bu_ref[r, re],
                              a_re * h_im + a_im * h_re + bu_ref[r, im])
                hb_ref[r, re] = h_re.astype(BF16)
                hb_ref[r, im] = h_im.astype(BF16)
            h_ref[:, re] = h_re
            h_ref[:, im] = h_im

    def readout(q):
        y = jnp.dot(hb_ref[q * sub_rows:(q + 1) * sub_rows, :], cd_ref[0], preferred_element_type=F32)
        y = pltpu.einshape("tbc->btc", y.reshape(S5_SUB, n_batch, GROUP_W))
        y_ref[:, q * S5_SUB:(q + 1) * S5_SUB, :] = y.astype(y_ref.dtype)

    order = list(range(S5_STEPS // S5_SUB))
    if reverse:
        order.reverse()
    project(order[0])
    for j, q in enumerate(order):
        if j + 1 < len(order):
            project(order[j + 1])
        scan(q)
        if j > 0:
            readout(order[j - 1])
    readout(order[-1])


def _s5(u, a_tab, b_dense, c_dense, *, n_ctx, reverse):
    n_batch, n_tok, _ = u.shape
    n_blk = n_tok // S5_STEPS
    n_cblk = n_ctx // S5_STEPS
    d = int(reverse)
    if reverse:
        blk = lambda s: jnp.where(s < n_cblk, n_cblk - 1 - s, n_blk - 1 - (s - n_cblk))
    else:
        blk = lambda s: s
    return pl.pallas_call(
        functools.partial(_s5_kernel, n_batch=n_batch, reverse=reverse),
        grid=(n_blk,),
        in_specs=[pl.BlockSpec((n_batch, S5_STEPS, GROUP_W), lambda s: (0, blk(s), 0)),
                  pl.BlockSpec((1, 1, 2 * SSM_W), lambda s: (d, 0, 0)),
                  pl.BlockSpec((1, GROUP_W, 2 * SSM_W), lambda s: (d, 0, 0)),
                  pl.BlockSpec((1, 2 * SSM_W, GROUP_W), lambda s: (d, 0, 0))],
        out_specs=pl.BlockSpec((n_batch, S5_STEPS, GROUP_W), lambda s: (0, blk(s), 0)),
        out_shape=_sds((n_batch, n_tok, GROUP_W), BF16),
        scratch_shapes=[pltpu.VMEM((S5_STEPS * n_batch, 2 * SSM_W), F32),
                        pltpu.VMEM((S5_STEPS * n_batch, 2 * SSM_W), BF16),
                        pltpu.VMEM((n_batch, 2 * SSM_W), F32)],
        compiler_params=_params(("arbitrary",)),
        name="s5",
    )(u, a_tab, b_dense, c_dense)


def _s5_tables(lam_re, lam_im, log_dt, b_re, b_im, c_re, c_im):
    lam = lax.complex(lam_re.astype(F32), lam_im.astype(F32))
    dt = jnp.exp(log_dt.astype(F32))[..., None]
    a_bar = jnp.exp(lam * dt)
    b_bar = ((a_bar - 1.0) / lam)[..., None] * lax.complex(b_re.astype(F32), b_im.astype(F32))
    eye = jnp.eye(SSM_GROUPS, dtype=F32)

    def embed_b(m):
        m = jnp.transpose(m, (0, 1, 3, 2))[:, :, :, None, :] * eye[None, :, None, :, None]
        return m.reshape(2, GROUP_W, SSM_W)

    def embed_c(m):
        m = jnp.transpose(m, (0, 1, 3, 2))[:, :, :, None, :] * eye[None, :, None, :, None]
        return m.reshape(2, SSM_W, GROUP_W)

    a_tab = jnp.concatenate([jnp.real(a_bar).reshape(2, 1, SSM_W), jnp.imag(a_bar).reshape(2, 1, SSM_W)], axis=-1)
    b_dense = jnp.concatenate([embed_b(jnp.real(b_bar)), embed_b(jnp.imag(b_bar))], axis=-1).astype(BF16)
    c_dense = jnp.concatenate([embed_c(c_re.astype(F32)), -embed_c(c_im.astype(F32))], axis=1).astype(BF16)
    return a_tab, b_dense, c_dense


def _gelu_tanh(x):
    return x * (0.5 * (1.0 + jnp.tanh(math.sqrt(2.0 / math.pi) * (x + 0.044715 * (x * x * x)))))


def _mixout_kernel(*refs, split_input, tiles_per_batch, tile_off):
    if split_input:
        x_ref, ctx_ref, *refs = refs
    else:
        x_ref, *refs = refs
    (mod_ref, oa_ref, od_ref, of_ref, u_ref, yf_ref, yb_ref, d_ref, wg_ref, bg_ref, gg_ref, wo_ref,
     g2_ref, w1_ref, w3_ref, w2_ref, o_ref, x1_ref, h_ref, ga2_ref) = refs
    s = pl.program_id(0)
    n_items = pl.num_programs(0) - 1

    def feed_forward():
        h = h_ref[...]
        acc = jnp.zeros((TILE, D_MODEL), F32)
        for c in range(D_FF // FF_CHUNK):
            cols = slice(c * FF_CHUNK, (c + 1) * FF_CHUNK)
            a = jnp.dot(h, w1_ref[:, cols], preferred_element_type=F32)
            b = jnp.dot(h, w3_ref[:, cols], preferred_element_type=F32)
            t = (a * jax.nn.sigmoid(a)) * b
            acc = acc + jnp.dot(t.astype(BF16), w2_ref[cols, :], preferred_element_type=F32)
        o_ref[0] = x1_ref[...] + ga2_ref[...] * acc

    def mix():
        x = x_ref[0]
        if split_input:
            is_ctx = lax.rem(jnp.minimum(s, n_items - 1), tiles_per_batch) + tile_off == 0
            x = jnp.where(is_ctx, ctx_ref[0], x)
        _, _, ga1 = _mod_slices(mod_ref[0], 0)
        sh2, sc2, ga2 = _mod_slices(mod_ref[0], 3)
        y = u_ref[0] * d_ref[...] + yf_ref[0].astype(F32) + yb_ref[0].astype(F32)
        g = _gelu_tanh(y)
        z = jnp.dot(g.astype(BF16), wg_ref[...], preferred_element_type=F32) + bg_ref[...]
        o_s = g * jax.nn.sigmoid(z)
        parts = [oa_ref[0].astype(F32), od_ref[0].astype(F32), of_ref[0].astype(F32), o_s]
        normed = [(_rms_rows(p) * gg_ref[:, k * GROUP_W:(k + 1) * GROUP_W]).astype(BF16)
                  for k, p in enumerate(parts)]
        r = jnp.dot(jnp.concatenate(normed, axis=-1), wo_ref[...], preferred_element_type=F32)
        x = x + ga1 * r
        x1_ref[...] = x
        h_ref[...] = ((_rms_rows(x) * g2_ref[...]) * (1.0 + sc2) + sh2).astype(BF16)
        ga2_ref[...] = ga2

    pl.when(s == 0)(mix)

    @pl.when(jnp.logical_and(s > 0, s < n_items))
    def _():
        feed_forward()
        mix()

    pl.when(s == n_items)(feed_forward)


def _mixout_ffn(xs, mod, o_a, o_d, o_f, u, y_fwd, y_bwd, d_skip, w_glu, b_glu, g_group, w_out, g2, w1, w3, w2, *,
                need_ctx, n_ctx):
    split_input = len(xs) == 2
    n_batch, n_tok, _ = u.shape
    off = 0 if need_ctx else n_ctx // TILE
    n_out = n_tok - off * TILE
    nt = n_out // TILE
    n_items = n_batch * nt
    tok = (1, TILE, D_MODEL)

    def item(fn):
        def index_map(s):
            a = jnp.minimum(s, n_items - 1)
            return fn(a // nt, a % nt)
        return index_map

    def out_map(s):
        a = jnp.maximum(s - 1, 0)
        return (a // nt, a % nt, 0)

    if split_input:
        assert need_ctx
        x_specs = [pl.BlockSpec(tok, item(lambda b, i: (b, jnp.maximum(i - 1, 0), 0))),
                   pl.BlockSpec(tok, item(lambda b, i: (b, 0, 0)))]
    else:
        x_specs = [pl.BlockSpec(tok, item(lambda b, i: (b, i + off, 0)))]
    grp = pl.BlockSpec((1, TILE, GROUP_W), item(lambda b, i: (b, i + off, 0)))
    mix = pl.BlockSpec((1, TILE, GROUP_W), item(lambda b, i: (b, i, 0)))
    const = lambda shape: pl.BlockSpec(shape, lambda s: (0,) * len(shape), pipeline_mode=pl.Buffered(1))
    return pl.pallas_call(
        functools.partial(_mixout_kernel, split_input=split_input, tiles_per_batch=nt, tile_off=off),
        grid=(n_items + 1,),
        in_specs=x_specs + [
            pl.BlockSpec((1, 1, N_MOD * D_MODEL), item(lambda b, i: (jnp.where(i + off == 0, n_batch, b), 0, 0))),
            mix, mix, mix, grp, grp, grp,
            const((1, GROUP_W)), const((GROUP_W, GROUP_W)), const((1, GROUP_W)),
            const((1, D_MODEL)), const((D_MODEL, D_MODEL)),
            const((1, D_MODEL)), const(w1.shape), const(w3.shape), const(w2.shape),
        ],
        out_specs=pl.BlockSpec(tok, out_map),
        out_shape=_sds((n_batch, n_out, D_MODEL), F32),
        scratch_shapes=[pltpu.VMEM((TILE, D_MODEL), F32), pltpu.VMEM((TILE, D_MODEL), BF16),
                        pltpu.VMEM((1, D_MODEL), F32)],
        compiler_params=_params(("arbitrary",)),
        name="mixffn",
    )(*xs, mod, o_a, o_d, o_f, u, y_fwd, y_bwd, d_skip, w_glu, b_glu, g_group, w_out, g2, w1, w3, w2)


def _rope_tables(n_lat, n_ctx):
    t = jnp.arange(n_lat, dtype=jnp.int32)
    rows = (t // GRID_W).astype(F32)
    cols = (t % GRID_W).astype(F32)
    axis_dim = HEAD_DIM // 2
    inv_freq = ROPE_THETA ** (-jnp.arange(0, axis_dim, 2, dtype=F32) / axis_dim)
    ang_r = rows[:, None] * inv_freq[None, :]
    ang_c = cols[:, None] * inv_freq[None, :]
    cos = jnp.concatenate([jnp.cos(ang_r)] * 2 + [jnp.cos(ang_c)] * 2, axis=-1)
    sin = jnp.concatenate([-jnp.sin(ang_r), jnp.sin(ang_r), -jnp.sin(ang_c), jnp.sin(ang_c)], axis=-1)
    cos = jnp.concatenate([jnp.ones((n_ctx, HEAD_DIM), F32), cos], axis=0)
    sin = jnp.concatenate([jnp.zeros((n_ctx, HEAD_DIM), F32), sin], axis=0)
    return jnp.tile(cos, (1, 2)), jnp.tile(sin, (1, 2))


def kernel(x, c, ctx, c_ctx, w_mod, b_mod, g_norm1, w_in, att_q_gain, att_k_gain, na_q_gain, na_k_gain, na_rel_bias, w_fourier, ssm_lam_re, ssm_lam_im, ssm_log_dt, ssm_b_re, ssm_b_im, ssm_c_re, ssm_c_im, ssm_d, w_glu, b_glu, g_group, w_out, g_norm2, w_ff1, w_ff3, w_ff2):
    n_batch, n_lat, d = x.shape
    n_ctx = ctx.shape[1]
    n_tok = n_lat + n_ctx
    assert d == D_MODEL and n_ctx == TILE and n_lat % TILE == 0 and n_lat % GRID_W == 0
    assert n_batch % 8 == 0 and n_tok % S5_STEPS == 0

    mod_rows = -(-(n_batch + 1) // 8) * 8
    cc = jnp.concatenate([c, c_ctx[None, :], jnp.zeros((mod_rows - n_batch - 1, d), F32)], axis=0)
    mod = _modulation(cc, w_mod, b_mod.reshape(DEPTH, 1, N_MOD * d)).reshape(DEPTH, mod_rows, 1, N_MOD * d)

    cos_t, sin_t = _rope_tables(n_lat, n_ctx)
    cs_bd = _head_dft_matrix()
    dft = _dft_matrix(n_lat)
    dft_ctx = _dft_matrix(n_ctx)
    def q_perm(a, axis):
        heads = [lax.slice_in_dim(a, h * HEAD_DIM, (h + 1) * HEAD_DIM, axis=axis) for h in _Q_ORDER]
        return jnp.concatenate(heads + [lax.slice_in_dim(a, GROUP_W, a.shape[axis], axis=axis)], axis=axis)

    scale = HEAD_DIM ** -0.5 * LOG2E
    tile2 = lambda g, s=1.0: jnp.tile(g.astype(F32) * s, 2)[None, :]
    row = lambda v: v.astype(F32)[None, :]

    xs = (x, ctx)
    for l in range(DEPTH):
        need_ctx = l < DEPTH - 1
        qkv, u = _inproj(xs, mod[l], row(g_norm1[l]), q_perm(w_in[l], 1).astype(BF16), cos_t, sin_t,
                         tile2(att_q_gain[l], scale), tile2(att_k_gain[l]),
                         tile2(na_q_gain[l], scale), tile2(na_k_gain[l]),
                         n_batch=n_batch, n_tok=n_tok)
        o_a = _gqa(_logits_bounded(att_q_gain[l], att_k_gain[l]), qkv, need_ctx=need_ctx, n_ctx=n_ctx)
        o_d = _na(_logits_bounded(na_q_gain[l], na_k_gain[l], jnp.max(jnp.abs(na_rel_bias[l]))), qkv,
                  _na_bias_table(na_rel_bias[l], n_lat // GRID_W), need_ctx=need_ctx, n_ctx=n_ctx)
        o_f = _fourier(qkv, cs_bd, dft, dft_ctx, w_fourier[l].astype(BF16), need_ctx=need_ctx, n_ctx=n_ctx)
        a_tab, b_dense, c_dense = _s5_tables(ssm_lam_re[l], ssm_lam_im[l], ssm_log_dt[l], ssm_b_re[l],
                                             ssm_b_im[l], ssm_c_re[l], ssm_c_im[l])
        y_fwd = _s5(u, a_tab, b_dense, c_dense, n_ctx=n_ctx, reverse=False)
        y_bwd = _s5(u, a_tab, b_dense, c_dense, n_ctx=n_ctx, reverse=True)
        x2 = _mixout_ffn(xs, mod[l], o_a, o_d, o_f, u, y_fwd, y_bwd, row(ssm_d[l]), w_glu[l].astype(BF16),
                         row(b_glu[l]), row(q_perm(g_group[l], 0)), q_perm(w_out[l], 0).astype(BF16),
                         row(g_norm2[l]), w_ff1[l].astype(BF16), w_ff3[l].astype(BF16), w_ff2[l].astype(BF16),
                         need_ctx=need_ctx, n_ctx=n_ctx)
        xs = (x2,)
    return xs[0]
```

```python
import functools
import math

import numpy as np
import jax
import jax.numpy as jnp
from jax import lax
from jax.experimental import pallas as pl
from jax.experimental.pallas import tpu as pltpu

F32 = jnp.float32
BF16 = jnp.bfloat16

D_MODEL = 1024
DEPTH = 2
GRID_W = 64
HEAD_DIM = 64
N_GROUPS = 4
GROUP_W = D_MODEL // N_GROUPS
WIN_H = 8
WIN_W = 16
NA_HEADS = GROUP_W // HEAD_DIM
FNO_HEAD_W = 64
SSM_GROUP = 16
SSM_GROUPS = GROUP_W // SSM_GROUP
SSM_STATE = 64
SSM_W = SSM_GROUPS * SSM_STATE
ROPE_THETA = 10000.0
D_FF = ((8 * D_MODEL // 3 + 255) // 256) * 256
N_MOD = 6
EPS = 1e-6
IN_W = 7 * GROUP_W
QKV_W = 6 * GROUP_W

LANES = 128
TILE = 256
BAND_ROWS = TILE // GRID_W
S5_STEPS = 128
S5_SUB = 32
FF_CHUNK = D_FF
ROPE_BLOCK = HEAD_DIM // 4
VMEM_LIMIT_BYTES = 56 * 2**20
NEG_BIG = -1e30
LOG2E = 1.0 / math.log(2.0)
EXP2_SAFE = 60.0

_Q_ORDER = (0, 2, 1, 3)


def _params(sem):
    return pltpu.CompilerParams(dimension_semantics=sem, vmem_limit_bytes=VMEM_LIMIT_BYTES)


def _sds(shape, dtype):
    return jax.ShapeDtypeStruct(shape, dtype)


def _mod_kernel(c_ref, w_ref, b_ref, o_ref):
    c = c_ref[...]
    s = c * jax.nn.sigmoid(c)
    w = w_ref[0]
    s_hi, w_hi = s.astype(BF16), w.astype(BF16)
    s_lo = (s - s_hi.astype(F32)).astype(BF16)
    w_lo = (w - w_hi.astype(F32)).astype(BF16)
    dot = functools.partial(jnp.dot, preferred_element_type=F32)
    o_ref[0] = dot(s_hi, w_hi) + (dot(s_lo, w_hi) + dot(s_hi, w_lo)) + b_ref[0]


def _modulation(cc, w_mod, b_mod):
    rows, d = cc.shape
    n_blk = 4
    bw = N_MOD * d // n_blk
    return pl.pallas_call(
        _mod_kernel,
        grid=(DEPTH, n_blk),
        in_specs=[
            pl.BlockSpec((rows, d), lambda l, j: (0, 0)),
            pl.BlockSpec((1, d, bw), lambda l, j: (l, 0, j)),
            pl.BlockSpec((1, 1, bw), lambda l, j: (l, 0, j)),
        ],
        out_specs=pl.BlockSpec((1, rows, bw), lambda l, j: (l, 0, j)),
        out_shape=_sds((DEPTH, rows, N_MOD * d), F32),
        compiler_params=_params(("arbitrary", "arbitrary")),
        name="modulation",
    )(cc, w_mod, b_mod)


def _rms_rows(x):
    return x * lax.rsqrt(jnp.mean(x * x, axis=-1, keepdims=True) + EPS)


def _mod_slices(m, first):
    return [m[:, (first + k) * D_MODEL:(first + k + 1) * D_MODEL] for k in range(3)]


def _head_norm(t, gain):
    lo = lax.broadcasted_iota(jnp.int32, t.shape, 1) < HEAD_DIM
    sq = t * t
    s_lo = jnp.sum(jnp.where(lo, sq, 0.0), axis=-1, keepdims=True)
    s_hi = jnp.sum(jnp.where(lo, 0.0, sq), axis=-1, keepdims=True)
    ms = jnp.where(lo, s_lo, s_hi) * (1.0 / HEAD_DIM)
    return t * lax.rsqrt(ms + EPS) * gain


def _rope(t, cos, sin):
    first = (lax.broadcasted_iota(jnp.int32, t.shape, 1) & ROPE_BLOCK) == 0
    partner = jnp.where(first, pltpu.roll(t, LANES - ROPE_BLOCK, axis=1), pltpu.roll(t, ROPE_BLOCK, axis=1))
    return t * cos + partner * sin


def _softmax_pv(s, v, bounded):
    p = jnp.exp2(s if bounded else s - jnp.max(s, axis=-1, keepdims=True))
    l = jnp.sum(p, axis=-1, keepdims=True)
    return jnp.dot(p.astype(BF16), v, preferred_element_type=F32) / l


def _when_bounded(flag_ref, body):
    bounded = flag_ref[0] == 1
    pl.when(bounded)(lambda: body(True))
    pl.when(jnp.logical_not(bounded))(lambda: body(False))


def _logits_bounded(gq, gk, extra=0.0):
    bound = math.sqrt(HEAD_DIM) * jnp.max(jnp.abs(gq)) * jnp.max(jnp.abs(gk)) + extra
    return (bound * LOG2E <= EXP2_SAFE).astype(jnp.int32).reshape(1)


def _split_heads(q):
    lo = lax.broadcasted_iota(jnp.int32, q.shape, 1) < HEAD_DIM
    zero = jnp.zeros_like(q)
    return jnp.concatenate([jnp.where(lo, q, zero), jnp.where(lo, zero, q)], axis=0)


def _merge_heads(o, rows):
    lo = lax.broadcasted_iota(jnp.int32, (rows, LANES), 1) < HEAD_DIM
    return jnp.where(lo, o[:rows], o[rows:])


def _nt_dot(a, b):
    return lax.dot_general(a, b, (((1,), (1,)), ((), ())), preferred_element_type=F32)


def _inproj_kernel(*refs, split_input):
    if split_input:
        x_ref, ctx_ref, *refs = refs
    else:
        x_ref, *refs = refs
    (mod_ref, g1_ref, w_ref, cos_ref, sin_ref, gq_ref, gk_ref, nq_ref, nk_ref, qkv_ref, u_ref) = refs
    x = x_ref[0]
    if split_input:
        x = jnp.where(pl.program_id(1) == 0, ctx_ref[0], x)
    sh1, sc1, _ = _mod_slices(mod_ref[0], 0)
    h = (_rms_rows(x) * g1_ref[...]) * (1.0 + sc1) + sh1
    p = jnp.dot(h.astype(BF16), w_ref[...], preferred_element_type=F32)
    cos = cos_ref[...]
    sin = sin_ref[...]

    def tile(j):
        return p[:, j * LANES:(j + 1) * LANES]

    out = [
        _rope(_head_norm(tile(0), gq_ref[...]), cos, sin),
        _rope(_head_norm(tile(1), gq_ref[...]), cos, sin),
        _rope(_head_norm(tile(2), gk_ref[...]), cos, sin),
        tile(3),
        _head_norm(tile(4), nq_ref[...]),
        _head_norm(tile(5), nq_ref[...]),
        _head_norm(tile(6), nk_ref[...]),
        _head_norm(tile(7), nk_ref[...]),
        tile(8), tile(9), tile(10), tile(11),
    ]
    for j, t in enumerate(out):
        qkv_ref[0, :, j * LANES:(j + 1) * LANES] = t.astype(BF16)
    u_ref[0] = p[:, QKV_W:]


def _inproj(xs, mod, g1, w_in, cos_t, sin_t, gq, gk, nq, nk, *, n_batch, n_tok):
    split_input = len(xs) == 2
    n_tiles = n_tok // TILE
    tok = (1, TILE, D_MODEL)
    if split_input:
        x_specs = [pl.BlockSpec(tok, lambda b, i: (b, jnp.maximum(i - 1, 0), 0)),
                   pl.BlockSpec(tok, lambda b, i: (b, 0, 0))]
    else:
        x_specs = [pl.BlockSpec(tok, lambda b, i: (b, i, 0))]
    vec = lambda w: pl.BlockSpec((1, w), lambda b, i: (0, 0))
    return pl.pallas_call(
        functools.partial(_inproj_kernel, split_input=split_input),
        grid=(n_batch, n_tiles),
        in_specs=x_specs + [
            pl.BlockSpec((1, 1, N_MOD * D_MODEL), lambda b, i: (jnp.where(i == 0, n_batch, b), 0, 0)),
            vec(D_MODEL),
            pl.BlockSpec((D_MODEL, IN_W), lambda b, i: (0, 0)),
            pl.BlockSpec((TILE, LANES), lambda b, i: (i, 0)),
            pl.BlockSpec((TILE, LANES), lambda b, i: (i, 0)),
            vec(LANES), vec(LANES), vec(LANES), vec(LANES),
        ],
        out_specs=[pl.BlockSpec((1, TILE, QKV_W), lambda b, i: (b, i, 0)),
                   pl.BlockSpec((1, TILE, GROUP_W), lambda b, i: (b, i, 0))],
        out_shape=[_sds((n_batch, n_tok, QKV_W), BF16), _sds((n_batch, n_tok, GROUP_W), F32)],
        compiler_params=_params(("arbitrary", "arbitrary")),
        name="inproj",
    )(*xs, mod, g1, w_in, cos_t, sin_t, gq, gk, nq, nk)


def _gqa_kernel(flag_ref, qa_ref, qb_ref, k_ref, v_ref, o_ref, *, n_ctx, tile_off):
    def attend(n_keys, bounded):
        k = k_ref[0, :n_keys, :]
        v = v_ref[0, :n_keys, :]
        outs = []
        for q_ref in (qa_ref, qb_ref):
            q2 = _split_heads(q_ref[0])
            outs.append(_merge_heads(_softmax_pv(_nt_dot(q2, k), v, bounded), TILE))
        o_ref[0] = jnp.concatenate(outs, axis=-1).astype(o_ref.dtype)

    if tile_off == 0:
        is_ctx = pl.program_id(1) == 0
        pl.when(is_ctx)(lambda: _when_bounded(flag_ref, functools.partial(attend, n_ctx)))
        pl.when(jnp.logical_not(is_ctx))(lambda: _when_bounded(flag_ref, functools.partial(attend, k_ref.shape[1])))
    else:
        _when_bounded(flag_ref, functools.partial(attend, k_ref.shape[1]))


_SMEM_SPEC = pl.BlockSpec(memory_space=pltpu.SMEM)


def _gqa(bounded, qkv, *, need_ctx, n_ctx):
    n_batch, n_tok, _ = qkv.shape
    off = 0 if need_ctx else n_ctx // TILE
    q_spec = lambda col: pl.BlockSpec((1, TILE, LANES), lambda b, i: (b, i + off, col))
    kv_spec = lambda col: pl.BlockSpec((1, n_tok, LANES), lambda b, i: (b, 0, col))
    return pl.pallas_call(
        functools.partial(_gqa_kernel, n_ctx=n_ctx, tile_off=off),
        grid=(n_batch, n_tok // TILE - off),
        in_specs=[_SMEM_SPEC, q_spec(0), q_spec(1), kv_spec(2), kv_spec(3)],
        out_specs=pl.BlockSpec((1, TILE, GROUP_W), lambda b, i: (b, i, 0)),
        out_shape=_sds((n_batch, n_tok - off * TILE, GROUP_W), BF16),
        compiler_params=_params(("arbitrary", "arbitrary")),
        name="gqa",
    )(bounded, qkv, qkv, qkv, qkv)


def _na_kernel(flag_ref, q_ref, kp_ref, kc_ref, kn_ref, kx_ref, vp_ref, vc_ref, vn_ref, vx_ref, tbl_ref, o_ref, *, tile_off):
    def attend(local, bounded):
        outs = []
        for pair in range(2):
            cols = slice(pair * LANES, (pair + 1) * LANES)
            q2 = _split_heads(q_ref[0, :, cols])
            if local:
                k = jnp.concatenate([r[0, :, cols] for r in (kp_ref, kc_ref, kn_ref, kx_ref)], axis=0)
                v = jnp.concatenate([r[0, :, cols] for r in (vp_ref, vc_ref, vn_ref, vx_ref)], axis=0)
                bias = jnp.concatenate([tbl_ref[0, 2 * pair], tbl_ref[0, 2 * pair + 1]], axis=0)
                s = _nt_dot(q2, k)
                s = jnp.concatenate([s[:, :3 * TILE] + bias, s[:, 3 * TILE:]], axis=-1)
            else:
                k = kx_ref[0, :, cols]
                v = vx_ref[0, :, cols]
                s = _nt_dot(q2, k)
            outs.append(_merge_heads(_softmax_pv(s, v, bounded), TILE))
        o_ref[0] = jnp.concatenate(outs, axis=-1).astype(o_ref.dtype)

    if tile_off == 0:
        is_ctx = pl.program_id(1) == 0
        pl.when(is_ctx)(lambda: _when_bounded(flag_ref, functools.partial(attend, False)))
        pl.when(jnp.logical_not(is_ctx))(lambda: _when_bounded(flag_ref, functools.partial(attend, True)))
    else:
        _when_bounded(flag_ref, functools.partial(attend, True))


def _na(bounded, qkv, table, *, need_ctx, n_ctx):
    n_batch, n_tok, _ = qkv.shape
    assert n_ctx == TILE
    n_tiles = n_tok // TILE
    off = 0 if need_ctx else 1
    blk = (1, TILE, GROUP_W)
    cur = lambda col: pl.BlockSpec(blk, lambda b, i: (b, i + off, col))
    prev = lambda col: pl.BlockSpec(blk, lambda b, i: (b, jnp.maximum(i + off - 1, 1), col))
    nxt = lambda col: pl.BlockSpec(blk, lambda b, i: (b, jnp.minimum(i + off + 1, n_tiles - 1), col))
    ctx = lambda col: pl.BlockSpec(blk, lambda b, i: (b, 0, col))

    def cls(b, i):
        t = i + off
        return (jnp.where(t == n_tiles - 1, 2, jnp.where(t <= 1, 0, 1)), 0, 0, 0)

    return pl.pallas_call(
        functools.partial(_na_kernel, tile_off=off),
        grid=(n_batch, n_tiles - off),
        in_specs=[_SMEM_SPEC, cur(2), prev(3), cur(3), nxt(3), ctx(3), prev(4), cur(4), nxt(4), ctx(4),
                  pl.BlockSpec((1, NA_HEADS, TILE, 3 * TILE), cls)],
        out_specs=pl.BlockSpec(blk, lambda b, i: (b, i, 0)),
        out_shape=_sds((n_batch, n_tok - off * TILE, GROUP_W), BF16),
        compiler_params=_params(("arbitrary", "arbitrary")),
        name="na",
    )(bounded, *([qkv] * 9), table)


def _na_bias_table(rel_bias, rows):
    n_bands = rows // BAND_ROWS
    assert rows >= WIN_H and rows % BAND_ROWS == 0
    n_dr, n_dc = 2 * WIN_H - 1, 2 * WIN_W - 1
    w = GRID_W
    shift = w - WIN_W
    vp = jnp.pad(rel_bias.astype(F32) * LOG2E, ((0, 0), (0, 0), (shift, 2 * w - shift - n_dc)))
    skew = jnp.broadcast_to(vp[:, :, None, :], (NA_HEADS, n_dr, w, 2 * w)).reshape(NA_HEADS, n_dr, 2 * w * w)
    col = skew[:, :, :w * (2 * w - 1)].reshape(NA_HEADS, n_dr, w, 2 * w - 1)[:, :, :, w - 1:]
    qc = np.arange(w)
    cs = np.clip(qc - WIN_W // 2, 0, w - WIN_W)
    col_ok = (qc[None, :] >= cs[:, None]) & (qc[None, :] < cs[:, None] + WIN_W)
    col = jnp.where(col_ok[None, None], col, NEG_BIG)
    masked = jnp.full((NA_HEADS, w, w), NEG_BIG, F32)
    classes = []
    for band in (0, min(1, n_bands - 1), n_bands - 1):
        q_rows = []
        for i in range(BAND_ROWS):
            r = BAND_ROWS * band + i
            rs = min(max(r - WIN_H // 2, 0), rows - WIN_H)
            blocks = []
            for kk in range(3 * BAND_ROWS):
                kr = BAND_ROWS * (band - 1) + kk
                ok = 0 <= kr < rows and rs <= kr < rs + WIN_H
                blocks.append(col[:, kr - r + WIN_H - 1] if ok else masked)
            q_rows.append(jnp.concatenate(blocks, axis=-1))
        classes.append(jnp.concatenate(q_rows, axis=-2))
    return jnp.stack(classes)


def _fourier_kernel(x_ref, cs_ref, dft_ref, dftc_ref, w_ref, o_ref, x12_ref, *, n_ctx, n_lat, tile_off):
    t = pl.program_id(1) + tile_off

    def head_dft(x):
        x12 = jnp.dot(x, cs_ref[...], preferred_element_type=F32).astype(BF16)
        return jnp.concatenate([x12[:, :GROUP_W], x12[:, GROUP_W:]], axis=0)

    def finish(y, n):
        y = y * (1.0 / math.sqrt(n * FNO_HEAD_W))
        o_ref[0] = jnp.dot(y.astype(BF16), w_ref[...], preferred_element_type=F32).astype(o_ref.dtype)

    if tile_off == 0:
        @pl.when(t == 0)
        def _():
            x12 = head_dft(x_ref[0, :n_ctx, :])
            finish(jnp.dot(dftc_ref[...], x12, preferred_element_type=F32), n_ctx)

    @pl.when(t == n_ctx // TILE)
    def _():
        x12_ref[...] = head_dft(x_ref[0, n_ctx:, :])

    @pl.when(t >= n_ctx // TILE)
    def _():
        row0 = pl.multiple_of((t - n_ctx // TILE) * TILE, TILE)
        finish(jnp.dot(dft_ref[pl.ds(row0, TILE), :], x12_ref[...], preferred_element_type=F32), n_lat)


def _fourier(qkv, cs_bd, dft, dft_ctx, w_f, *, need_ctx, n_ctx):
    n_batch, n_tok, _ = qkv.shape
    n_lat = n_tok - n_ctx
    off = 0 if need_ctx else n_ctx // TILE
    const = lambda shape: pl.BlockSpec(shape, lambda b, i: (0,) * len(shape))
    return pl.pallas_call(
        functools.partial(_fourier_kernel, n_ctx=n_ctx, n_lat=n_lat, tile_off=off),
        grid=(n_batch, n_tok // TILE - off),
        in_specs=[pl.BlockSpec((1, n_tok, GROUP_W), lambda b, i: (b, 0, 5)),
                  const(cs_bd.shape), const(dft.shape), const(dft_ctx.shape), const(w_f.shape)],
        out_specs=pl.BlockSpec((1, TILE, GROUP_W), lambda b, i: (b, i, 0)),
        out_shape=_sds((n_batch, n_tok - off * TILE, GROUP_W), BF16),
        scratch_shapes=[pltpu.VMEM((2 * n_lat, GROUP_W), BF16)],
        compiler_params=_params(("arbitrary", "arbitrary")),
        name="fourier",
    )(qkv, cs_bd, dft, dft_ctx, w_f)


def _dft_matrix(n):
    f = 32
    assert n % f == 0
    k = lax.broadcasted_iota(jnp.int32, (n, 1), 0)
    m1 = lax.broadcasted_iota(jnp.int32, (1, n // f), 1)
    m0 = lax.broadcasted_iota(jnp.int32, (1, f), 1)
    ang_a = ((k * (m1 * f)) % n).astype(F32) * (2.0 * math.pi / n)
    ang_b = ((k * m0) % n).astype(F32) * (2.0 * math.pi / n)
    ca, sa = jnp.cos(ang_a)[:, :, None], jnp.sin(ang_a)[:, :, None]
    cb, sb = jnp.cos(ang_b)[:, None, :], jnp.sin(ang_b)[:, None, :]
    cos = (ca * cb - sa * sb).reshape(n, n)
    sin = (sa * cb + ca * sb).reshape(n, n)
    return jnp.concatenate([cos, -sin], axis=1).astype(BF16)


def _head_dft_matrix():
    a = np.arange(GROUP_W)
    same = (a[:, None] // FNO_HEAD_W) == (a[None, :] // FNO_HEAD_W)
    ang = 2.0 * np.pi * (((a[:, None] % FNO_HEAD_W) * (a[None, :] % FNO_HEAD_W)) % FNO_HEAD_W) / FNO_HEAD_W
    m = np.concatenate([np.where(same, np.cos(ang), 0.0), np.where(same, np.sin(ang), 0.0)], axis=1)
    return jnp.asarray(m, F32).astype(BF16)


def _s5_kernel(u_ref, a_ref, bd_ref, cd_ref, y_ref, bu_ref, hb_ref, h_ref, *, n_batch, reverse):
    @pl.when(pl.program_id(0) == 0)
    def _():
        h_ref[...] = jnp.zeros_like(h_ref)

    sub_rows = S5_SUB * n_batch
    chunk = 4 * LANES
    def project(q):
        u = pltpu.einshape("btc->tbc", u_ref[:, q * S5_SUB:(q + 1) * S5_SUB, :])
        u = u.reshape(sub_rows, GROUP_W).astype(BF16)
        bu_ref[q * sub_rows:(q + 1) * sub_rows, :] = jnp.dot(u, bd_ref[0], preferred_element_type=F32)

    def scan(q):
        steps = range(S5_SUB - 1, -1, -1) if reverse else range(S5_SUB)
        for sc in range(SSM_W // chunk):
            re = slice(sc * chunk, (sc + 1) * chunk)
            im = slice(SSM_W + sc * chunk, SSM_W + (sc + 1) * chunk)
            a_re = jnp.broadcast_to(a_ref[0, :, re], (n_batch, chunk))
            a_im = jnp.broadcast_to(a_ref[0, :, im], (n_batch, chunk))
            h_re, h_im = h_ref[:, re], h_ref[:, im]
            for t in steps:
                r = slice(q * sub_rows + t * n_batch, q * sub_rows + (t + 1) * n_batch)
                h_re, h_im = (a_re * h_re - a_im * h_im + bu_ref[r, re],
                              a_re * h_im + a_im * h_re + bu_ref[r, im])
                hb_ref[r, re] = h_re.astype(BF16)
                hb_ref[r, im] = h_im.astype(BF16)
            h_ref[:, re] = h_re
            h_ref[:, im] = h_im

    def readout(q):
        y = jnp.dot(hb_ref[q * sub_rows:(q + 1) * sub_rows, :], cd_ref[0], preferred_element_type=F32)
        y = pltpu.einshape("tbc->btc", y.reshape(S5_SUB, n_batch, GROUP_W))
        y_ref[:, q * S5_SUB:(q + 1) * S5_SUB, :] = y.astype(y_ref.dtype)

    order = list(range(S5_STEPS // S5_SUB))
    if reverse:
        order.reverse()
    project(order[0])
    for j, q in enumerate(order):
        if j + 1 < len(order):
            project(order[j + 1])
        scan(q)
        if j > 0:
            readout(order[j - 1])
    readout(order[-1])


def _s5(u, a_tab, b_dense, c_dense, *, n_ctx, reverse):
    n_batch, n_tok, _ = u.shape
    n_blk = n_tok // S5_STEPS
    n_cblk = n_ctx // S5_STEPS
    d = int(reverse)
    if reverse:
        blk = lambda s: jnp.where(s < n_cblk, n_cblk - 1 - s, n_blk - 1 - (s - n_cblk))
    else:
        blk = lambda s: s
    return pl.pallas_call(
        functools.partial(_s5_kernel, n_batch=n_batch, reverse=reverse),
        grid=(n_blk,),
        in_specs=[pl.BlockSpec((n_batch, S5_STEPS, GROUP_W), lambda s: (0, blk(s), 0)),
                  pl.BlockSpec((1, 1, 2 * SSM_W), lambda s: (d, 0, 0)),
                  pl.BlockSpec((1, GROUP_W, 2 * SSM_W), lambda s: (d, 0, 0)),
                  pl.BlockSpec((1, 2 * SSM_W, GROUP_W), lambda s: (d, 0, 0))],
        out_specs=pl.BlockSpec((n_batch, S5_STEPS, GROUP_W), lambda s: (0, blk(s), 0)),
        out_shape=_sds((n_batch, n_tok, GROUP_W), BF16),
        scratch_shapes=[pltpu.VMEM((S5_STEPS * n_batch, 2 * SSM_W), F32),
                        pltpu.VMEM((S5_STEPS * n_batch, 2 * SSM_W), BF16),
                        pltpu.VMEM((n_batch, 2 * SSM_W), F32)],
        compiler_params=_params(("arbitrary",)),
        name="s5",
    )(u, a_tab, b_dense, c_dense)


def _s5_tables(lam_re, lam_im, log_dt, b_re, b_im, c_re, c_im):
    lam = lax.complex(lam_re.astype(F32), lam_im.astype(F32))
    dt = jnp.exp(log_dt.astype(F32))[..., None]
    a_bar = jnp.exp(lam * dt)
    b_bar = ((a_bar - 1.0) / lam)[..., None] * lax.complex(b_re.astype(F32), b_im.astype(F32))
    eye = jnp.eye(SSM_GROUPS, dtype=F32)

    def embed_b(m):
        m = jnp.transpose(m, (0, 1, 3, 2))[:, :, :, None, :] * eye[None, :, None, :, None]
        return m.reshape(2, GROUP_W, SSM_W)

    def embed_c(m):
        m = jnp.transpose(m, (0, 1, 3, 2))[:, :, :, None, :] * eye[None, :, None, :, None]
        return m.reshape(2, SSM_W, GROUP_W)

    a_tab = jnp.concatenate([jnp.real(a_bar).reshape(2, 1, SSM_W), jnp.imag(a_bar).reshape(2, 1, SSM_W)], axis=-1)
    b_dense = jnp.concatenate([embed_b(jnp.real(b_bar)), embed_b(jnp.imag(b_bar))], axis=-1).astype(BF16)
    c_dense = jnp.concatenate([embed_c(c_re.astype(F32)), -embed_c(c_im.astype(F32))], axis=1).astype(BF16)
    return a_tab, b_dense, c_dense


def _gelu_tanh(x):
    return x * (0.5 * (1.0 + jnp.tanh(math.sqrt(2.0 / math.pi) * (x + 0.044715 * (x * x * x)))))


def _mixout_kernel(*refs, split_input, tiles_per_batch, tile_off):
    if split_input:
        x_ref, ctx_ref, *refs = refs
    else:
        x_ref, *refs = refs
    (mod_ref, oa_ref, od_ref, of_ref, u_ref, yf_ref, yb_ref, d_ref, wg_ref, bg_ref, gg_ref, wo_ref,
     g2_ref, w1_ref, w3_ref, w2_ref, o_ref, x1_ref, h_ref, ga2_ref) = refs
    s = pl.program_id(0)
    n_items = pl.num_programs(0) - 1

    def feed_forward():
        h = h_ref[...]
        acc = jnp.zeros((TILE, D_MODEL), F32)
        for c in range(D_FF // FF_CHUNK):
            cols = slice(c * FF_CHUNK, (c + 1) * FF_CHUNK)
            a = jnp.dot(h, w1_ref[:, cols], preferred_element_type=F32)
            b = jnp.dot(h, w3_ref[:, cols], preferred_element_type=F32)
            t = (a * jax.nn.sigmoid(a)) * b
            acc = acc + jnp.dot(t.astype(BF16), w2_ref[cols, :], preferred_element_type=F32)
        o_ref[0] = x1_ref[...] + ga2_ref[...] * acc

    def mix():
        x = x_ref[0]
        if split_input:
            is_ctx = lax.rem(jnp.minimum(s, n_items - 1), tiles_per_batch) + tile_off == 0
            x = jnp.where(is_ctx, ctx_ref[0], x)
        _, _, ga1 = _mod_slices(mod_ref[0], 0)
        sh2, sc2, ga2 = _mod_slices(mod_ref[0], 3)
        y = u_ref[0] * d_ref[...] + yf_ref[0].astype(F32) + yb_ref[0].astype(F32)
        g = _gelu_tanh(y)
        z = jnp.dot(g.astype(BF16), wg_ref[...], preferred_element_type=F32) + bg_ref[...]
        o_s = g * jax.nn.sigmoid(z)
        parts = [oa_ref[0].astype(F32), od_ref[0].astype(F32), of_ref[0].astype(F32), o_s]
        normed = [(_rms_rows(p) * gg_ref[:, k * GROUP_W:(k + 1) * GROUP_W]).astype(BF16)
                  for k, p in enumerate(parts)]
        r = jnp.dot(jnp.concatenate(normed, axis=-1), wo_ref[...], preferred_element_type=F32)
        x = x + ga1 * r
        x1_ref[...] = x
        h_ref[...] = ((_rms_rows(x) * g2_ref[...]) * (1.0 + sc2) + sh2).astype(BF16)
        ga2_ref[...] = ga2

    pl.when(s == 0)(mix)

    @pl.when(jnp.logical_and(s > 0, s < n_items))
    def _():
        feed_forward()
        mix()

    pl.when(s == n_items)(feed_forward)


def _mixout_ffn(xs, mod, o_a, o_d, o_f, u, y_fwd, y_bwd, d_skip, w_glu, b_glu, g_group, w_out, g2, w1, w3, w2, *,
                need_ctx, n_ctx):
    split_input = len(xs) == 2
    n_batch, n_tok, _ = u.shape
    off = 0 if need_ctx else n_ctx // TILE
    n_out = n_tok - off * TILE
    nt = n_out // TILE
    n_items = n_batch * nt
    tok = (1, TILE, D_MODEL)

    def item(fn):
        def index_map(s):
            a = jnp.minimum(s, n_items - 1)
            return fn(a // nt, a % nt)
        return index_map

    def out_map(s):
        a = jnp.maximum(s - 1, 0)
        return (a // nt, a % nt, 0)

    if split_input:
        assert need_ctx
        x_specs = [pl.BlockSpec(tok, item(lambda b, i: (b, jnp.maximum(i - 1, 0), 0))),
                   pl.BlockSpec(tok, item(lambda b, i: (b, 0, 0)))]
    else:
        x_specs = [pl.BlockSpec(tok, item(lambda b, i: (b, i + off, 0)))]
    grp = pl.BlockSpec((1, TILE, GROUP_W), item(lambda b, i: (b, i + off, 0)))
    mix = pl.BlockSpec((1, TILE, GROUP_W), item(lambda b, i: (b, i, 0)))
    const = lambda shape: pl.BlockSpec(shape, lambda s: (0,) * len(shape), pipeline_mode=pl.Buffered(1))
    return pl.pallas_call(
        functools.partial(_mixout_kernel, split_input=split_input, tiles_per_batch=nt, tile_off=off),
        grid=(n_items + 1,),
        in_specs=x_specs + [
            pl.BlockSpec((1, 1, N_MOD * D_MODEL), item(lambda b, i: (jnp.where(i + off == 0, n_batch, b), 0, 0))),
            mix, mix, mix, grp, grp, grp,
            const((1, GROUP_W)), const((GROUP_W, GROUP_W)), const((1, GROUP_W)),
            const((1, D_MODEL)), const((D_MODEL, D_MODEL)),
            const((1, D_MODEL)), const(w1.shape), const(w3.shape), const(w2.shape),
        ],
        out_specs=pl.BlockSpec(tok, out_map),
        out_shape=_sds((n_batch, n_out, D_MODEL), F32),
        scratch_shapes=[pltpu.VMEM((TILE, D_MODEL), F32), pltpu.VMEM((TILE, D_MODEL), BF16),
                        pltpu.VMEM((1, D_MODEL), F32)],
        compiler_params=_params(("arbitrary",)),
        name="mixffn",
    )(*xs, mod, o_a, o_d, o_f, u, y_fwd, y_bwd, d_skip, w_glu, b_glu, g_group, w_out, g2, w1, w3, w2)


def _rope_tables(n_lat, n_ctx):
    t = jnp.arange(n_lat, dtype=jnp.int32)
    rows = (t // GRID_W).astype(F32)
    cols = (t % GRID_W).astype(F32)
    axis_dim = HEAD_DIM // 2
    inv_freq = ROPE_THETA ** (-jnp.arange(0, axis_dim, 2, dtype=F32) / axis_dim)
    ang_r = rows[:, None] * inv_freq[None, :]
    ang_c = cols[:, None] * inv_freq[None, :]
    cos = jnp.concatenate([jnp.cos(ang_r)] * 2 + [jnp.cos(ang_c)] * 2, axis=-1)
    sin = jnp.concatenate([-jnp.sin(ang_r), jnp.sin(ang_r), -jnp.sin(ang_c), jnp.sin(ang_c)], axis=-1)
    cos = jnp.concatenate([jnp.ones((n_ctx, HEAD_DIM), F32), cos], axis=0)
    sin = jnp.concatenate([jnp.zeros((n_ctx, HEAD_DIM), F32), sin], axis=0)
    return jnp.tile(cos, (1, 2)), jnp.tile(sin, (1, 2))


def kernel(x, c, ctx, c_ctx, w_mod, b_mod, g_norm1, w_in, att_q_gain, att_k_gain, na_q_gain, na_k_gain, na_rel_bias, w_fourier, ssm_lam_re, ssm_lam_im, ssm_log_dt, ssm_b_re, ssm_b_im, ssm_c_re, ssm_c_im, ssm_d, w_glu, b_glu, g_group, w_out, g_norm2, w_ff1, w_ff3, w_ff2):
    n_batch, n_lat, d = x.shape
    n_ctx = ctx.shape[1]
    n_tok = n_lat + n_ctx
    assert d == D_MODEL and n_ctx == TILE and n_lat % TILE == 0 and n_lat % GRID_W == 0
    assert n_batch % 8 == 0 and n_tok % S5_STEPS == 0

    mod_rows = -(-(n_batch + 1) // 8) * 8
    cc = jnp.concatenate([c, c_ctx[None, :], jnp.zeros((mod_rows - n_batch - 1, d), F32)], axis=0)
    mod = _modulation(cc, w_mod, b_mod.reshape(DEPTH, 1, N_MOD * d)).reshape(DEPTH, mod_rows, 1, N_MOD * d)

    cos_t, sin_t = _rope_tables(n_lat, n_ctx)
    cs_bd = _head_dft_matrix()
    dft = _dft_matrix(n_lat)
    dft_ctx = _dft_matrix(n_ctx)
    def q_perm(a, axis):
        heads = [lax.slice_in_dim(a, h * HEAD_DIM, (h + 1) * HEAD_DIM, axis=axis) for h in _Q_ORDER]
        return jnp.concatenate(heads + [lax.slice_in_dim(a, GROUP_W, a.shape[axis], axis=axis)], axis=axis)

    scale = HEAD_DIM ** -0.5 * LOG2E
    tile2 = lambda g, s=1.0: jnp.tile(g.astype(F32) * s, 2)[None, :]
    row = lambda v: v.astype(F32)[None, :]

    xs = (x, ctx)
    for l in range(DEPTH):
        need_ctx = l < DEPTH - 1
        qkv, u = _inproj(xs, mod[l], row(g_norm1[l]), q_perm(w_in[l], 1).astype(BF16), cos_t, sin_t,
                         tile2(att_q_gain[l], scale), tile2(att_k_gain[l]),
                         tile2(na_q_gain[l], scale), tile2(na_k_gain[l]),
                         n_batch=n_batch, n_tok=n_tok)
        o_a = _gqa(_logits_bounded(att_q_gain[l], att_k_gain[l]), qkv, need_ctx=need_ctx, n_ctx=n_ctx)
        o_d = _na(_logits_bounded(na_q_gain[l], na_k_gain[l], jnp.max(jnp.abs(na_rel_bias[l]))), qkv,
                  _na_bias_table(na_rel_bias[l], n_lat // GRID_W), need_ctx=need_ctx, n_ctx=n_ctx)
        o_f = _fourier(qkv, cs_bd, dft, dft_ctx, w_fourier[l].astype(BF16), need_ctx=need_ctx, n_ctx=n_ctx)
        a_tab, b_dense, c_dense = _s5_tables(ssm_lam_re[l], ssm_lam_im[l], ssm_log_dt[l], ssm_b_re[l],
                                             ssm_b_im[l], ssm_c_re[l], ssm_c_im[l])
        y_fwd = _s5(u, a_tab, b_dense, c_dense, n_ctx=n_ctx, reverse=False)
        y_bwd = _s5(u, a_tab, b_dense, c_dense, n_ctx=n_ctx, reverse=True)
        x2 = _mixout_ffn(xs, mod[l], o_a, o_d, o_f, u, y_fwd, y_bwd, row(ssm_d[l]), w_glu[l].astype(BF16),
                         row(b_glu[l]), row(q_perm(g_group[l], 0)), q_perm(w_out[l], 0).astype(BF16),
                         row(g_norm2[l]), w_ff1[l].astype(BF16), w_ff3[l].astype(BF16), w_ff2[l].astype(BF16),
                         need_ctx=need_ctx, n_ctx=n_ctx)
        xs = (x2,)
    return xs[0]
```

```python
import functools
import math

import numpy as np
import jax
import jax.numpy as jnp
from jax import lax
from jax.experimental import pallas as pl
from jax.experimental.pallas import tpu as pltpu

F32 = jnp.float32
BF16 = jnp.bfloat16

D_MODEL = 1024
DEPTH = 2
GRID_W = 64
HEAD_DIM = 64
N_GROUPS = 4
GROUP_W = D_MODEL // N_GROUPS
WIN_H = 8
WIN_W = 16
NA_HEADS = GROUP_W // HEAD_DIM
FNO_HEAD_W = 64
SSM_GROUP = 16
SSM_GROUPS = GROUP_W // SSM_GROUP
SSM_STATE = 64
SSM_W = SSM_GROUPS * SSM_STATE
ROPE_THETA = 10000.0
D_FF = ((8 * D_MODEL // 3 + 255) // 256) * 256
N_MOD = 6
EPS = 1e-6
IN_W = 7 * GROUP_W
QKV_W = 6 * GROUP_W

LANES = 128
TILE = 256
BAND_ROWS = TILE // GRID_W
S5_STEPS = 128
S5_SUB = 32
FF_CHUNK = D_FF
FOURIER_ROWS = 512
ROPE_BLOCK = HEAD_DIM // 4
VMEM_LIMIT_BYTES = 56 * 2**20
NEG_BIG = -1e30
LOG2E = 1.0 / math.log(2.0)
EXP2_SAFE = 60.0

_Q_ORDER = (0, 2, 1, 3)


def _params(sem):
    return pltpu.CompilerParams(dimension_semantics=sem, vmem_limit_bytes=VMEM_LIMIT_BYTES)


def _sds(shape, dtype):
    return jax.ShapeDtypeStruct(shape, dtype)


def _mod_kernel(c_ref, w_ref, b_ref, o_ref):
    c = c_ref[...]
    s = c * jax.nn.sigmoid(c)
    w = w_ref[0]
    s_hi, w_hi = s.astype(BF16), w.astype(BF16)
    s_lo = (s - s_hi.astype(F32)).astype(BF16)
    w_lo = (w - w_hi.astype(F32)).astype(BF16)
    dot = functools.partial(jnp.dot, preferred_element_type=F32)
    o_ref[0] = dot(s_hi, w_hi) + (dot(s_lo, w_hi) + dot(s_hi, w_lo)) + b_ref[0]


def _modulation(cc, w_mod, b_mod):
    rows, d = cc.shape
    n_blk = 4
    bw = N_MOD * d // n_blk
    return pl.pallas_call(
        _mod_kernel,
        grid=(DEPTH, n_blk),
        in_specs=[
            pl.BlockSpec((rows, d), lambda l, j: (0, 0)),
            pl.BlockSpec((1, d, bw), lambda l, j: (l, 0, j)),
            pl.BlockSpec((1, 1, bw), lambda l, j: (l, 0, j)),
        ],
        out_specs=pl.BlockSpec((1, rows, bw), lambda l, j: (l, 0, j)),
        out_shape=_sds((DEPTH, rows, N_MOD * d), F32),
        compiler_params=_params(("arbitrary", "arbitrary")),
        name="modulation",
    )(cc, w_mod, b_mod)


def _rms_rows(x):
    return x * lax.rsqrt(jnp.mean(x * x, axis=-1, keepdims=True) + EPS)


def _mod_slices(m, first):
    return [m[:, (first + k) * D_MODEL:(first + k + 1) * D_MODEL] for k in range(3)]


def _head_norm(t, gain):
    lo = lax.broadcasted_iota(jnp.int32, t.shape, 1) < HEAD_DIM
    sq = t * t
    s_lo = jnp.sum(jnp.where(lo, sq, 0.0), axis=-1, keepdims=True)
    s_hi = jnp.sum(jnp.where(lo, 0.0, sq), axis=-1, keepdims=True)
    ms = jnp.where(lo, s_lo, s_hi) * (1.0 / HEAD_DIM)
    return t * lax.rsqrt(ms + EPS) * gain


def _rope(t, cos, sin):
    first = (lax.broadcasted_iota(jnp.int32, t.shape, 1) & ROPE_BLOCK) == 0
    partner = jnp.where(first, pltpu.roll(t, LANES - ROPE_BLOCK, axis=1), pltpu.roll(t, ROPE_BLOCK, axis=1))
    return t * cos + partner * sin


def _softmax_pv(s, v, bounded):
    p = jnp.exp2(s if bounded else s - jnp.max(s, axis=-1, keepdims=True))
    l = jnp.sum(p, axis=-1, keepdims=True)
    return jnp.dot(p.astype(BF16), v, preferred_element_type=F32) / l


def _when_bounded(flag_ref, body):
    bounded = flag_ref[0] == 1
    pl.when(bounded)(lambda: body(True))
    pl.when(jnp.logical_not(bounded))(lambda: body(False))


def _logits_bounded(gq, gk, extra=0.0):
    bound = math.sqrt(HEAD_DIM) * jnp.max(jnp.abs(gq)) * jnp.max(jnp.abs(gk)) + extra
    return (bound * LOG2E <= EXP2_SAFE).astype(jnp.int32).reshape(1)


def _split_heads(q):
    lo = lax.broadcasted_iota(jnp.int32, q.shape, 1) < HEAD_DIM
    zero = jnp.zeros_like(q)
    return jnp.concatenate([jnp.where(lo, q, zero), jnp.where(lo, zero, q)], axis=0)


def _merge_heads(o, rows):
    lo = lax.broadcasted_iota(jnp.int32, (rows, LANES), 1) < HEAD_DIM
    return jnp.where(lo, o[:rows], o[rows:])


def _nt_dot(a, b):
    return lax.dot_general(a, b, (((1,), (1,)), ((), ())), preferred_element_type=F32)


def _inproj_kernel(*refs, split_input):
    if split_input:
        x_ref, ctx_ref, *refs = refs
    else:
        x_ref, *refs = refs
    (mod_ref, g1_ref, w_ref, cos_ref, sin_ref, gq_ref, gk_ref, nq_ref, nk_ref, qkv_ref, u_ref) = refs
    x = x_ref[0]
    if split_input:
        x = jnp.where(pl.program_id(1) == 0, ctx_ref[0], x)
    sh1, sc1, _ = _mod_slices(mod_ref[0], 0)
    h = (_rms_rows(x) * g1_ref[...]) * (1.0 + sc1) + sh1
    hb = h.astype(BF16)
    p = [jnp.dot(hb, w_ref[:, c:c + GROUP_W], preferred_element_type=F32) for c in range(0, IN_W, GROUP_W)]
    cos = cos_ref[...]
    sin = sin_ref[...]

    def tile(j):
        return p[j // 2][:, (j % 2) * LANES:(j % 2 + 1) * LANES]

    out = [
        _rope(_head_norm(tile(0), gq_ref[...]), cos, sin),
        _rope(_head_norm(tile(1), gq_ref[...]), cos, sin),
        _rope(_head_norm(tile(2), gk_ref[...]), cos, sin),
        tile(3),
        _head_norm(tile(4), nq_ref[...]),
        _head_norm(tile(5), nq_ref[...]),
        _head_norm(tile(6), nk_ref[...]),
        _head_norm(tile(7), nk_ref[...]),
        tile(8), tile(9), tile(10), tile(11),
    ]
    for j, t in enumerate(out):
        qkv_ref[0, :, j * LANES:(j + 1) * LANES] = t.astype(BF16)
    u_ref[0] = p[QKV_W // GROUP_W]


def _inproj(xs, mod, g1, w_in, cos_t, sin_t, gq, gk, nq, nk, *, n_batch, n_tok):
    split_input = len(xs) == 2
    n_tiles = n_tok // TILE
    tok = (1, TILE, D_MODEL)
    if split_input:
        x_specs = [pl.BlockSpec(tok, lambda b, i: (b, jnp.maximum(i - 1, 0), 0)),
                   pl.BlockSpec(tok, lambda b, i: (b, 0, 0))]
    else:
        x_specs = [pl.BlockSpec(tok, lambda b, i: (b, i, 0))]
    vec = lambda w: pl.BlockSpec((1, w), lambda b, i: (0, 0))
    return pl.pallas_call(
        functools.partial(_inproj_kernel, split_input=split_input),
        grid=(n_batch, n_tiles),
        in_specs=x_specs + [
            pl.BlockSpec((1, 1, N_MOD * D_MODEL), lambda b, i: (jnp.where(i == 0, n_batch, b), 0, 0)),
            vec(D_MODEL),
            pl.BlockSpec((D_MODEL, IN_W), lambda b, i: (0, 0)),
            pl.BlockSpec((TILE, LANES), lambda b, i: (i, 0)),
            pl.BlockSpec((TILE, LANES), lambda b, i: (i, 0)),
            vec(LANES), vec(LANES), vec(LANES), vec(LANES),
        ],
        out_specs=[pl.BlockSpec((1, TILE, QKV_W), lambda b, i: (b, i, 0)),
                   pl.BlockSpec((1, TILE, GROUP_W), lambda b, i: (b, i, 0))],
        out_shape=[_sds((n_batch, n_tok, QKV_W), BF16), _sds((n_batch, n_tok, GROUP_W), F32)],
        compiler_params=_params(("arbitrary", "arbitrary")),
        name="inproj",
    )(*xs, mod, g1, w_in, cos_t, sin_t, gq, gk, nq, nk)


def _gqa_kernel(flag_ref, qa_ref, qb_ref, k_ref, v_ref, o_ref, *, n_ctx, tile_off):
    def attend(n_keys, bounded):
        k = k_ref[0, :n_keys, :]
        v = v_ref[0, :n_keys, :]
        outs = []
        for q_ref in (qa_ref, qb_ref):
            q2 = _split_heads(q_ref[0])
            outs.append(_merge_heads(_softmax_pv(_nt_dot(q2, k), v, bounded), TILE))
        o_ref[0] = jnp.concatenate(outs, axis=-1).astype(o_ref.dtype)

    if tile_off == 0:
        is_ctx = pl.program_id(1) == 0
        pl.when(is_ctx)(lambda: _when_bounded(flag_ref, functools.partial(attend, n_ctx)))
        pl.when(jnp.logical_not(is_ctx))(lambda: _when_bounded(flag_ref, functools.partial(attend, k_ref.shape[1])))
    else:
        _when_bounded(flag_ref, functools.partial(attend, k_ref.shape[1]))


_SMEM_SPEC = pl.BlockSpec(memory_space=pltpu.SMEM)


def _gqa(bounded, qkv, *, need_ctx, n_ctx):
    n_batch, n_tok, _ = qkv.shape
    off = 0 if need_ctx else n_ctx // TILE
    q_spec = lambda col: pl.BlockSpec((1, TILE, LANES), lambda b, i: (b, i + off, col))
    kv_spec = lambda col: pl.BlockSpec((1, n_tok, LANES), lambda b, i: (b, 0, col))
    return pl.pallas_call(
        functools.partial(_gqa_kernel, n_ctx=n_ctx, tile_off=off),
        grid=(n_batch, n_tok // TILE - off),
        in_specs=[_SMEM_SPEC, q_spec(0), q_spec(1), kv_spec(2), kv_spec(3)],
        out_specs=pl.BlockSpec((1, TILE, GROUP_W), lambda b, i: (b, i, 0)),
        out_shape=_sds((n_batch, n_tok - off * TILE, GROUP_W), BF16),
        compiler_params=_params(("arbitrary", "arbitrary")),
        name="gqa",
    )(bounded, qkv, qkv, qkv, qkv)


def _na_kernel(flag_ref, q_ref, kp_ref, kc_ref, kn_ref, kx_ref, vp_ref, vc_ref, vn_ref, vx_ref, tbl_ref, o_ref, *, tile_off):
    def attend(local, bounded):
        outs = []
        for pair in range(2):
            cols = slice(pair * LANES, (pair + 1) * LANES)
            q2 = _split_heads(q_ref[0, :, cols])
            if local:
                k = jnp.concatenate([r[0, :, cols] for r in (kp_ref, kc_ref, kn_ref, kx_ref)], axis=0)
                v = jnp.concatenate([r[0, :, cols] for r in (vp_ref, vc_ref, vn_ref, vx_ref)], axis=0)
                bias = jnp.concatenate([tbl_ref[0, 2 * pair], tbl_ref[0, 2 * pair + 1]], axis=0)
                s = _nt_dot(q2, k)
                s = jnp.concatenate([s[:, :3 * TILE] + bias, s[:, 3 * TILE:]], axis=-1)
            else:
                k = kx_ref[0, :, cols]
                v = vx_ref[0, :, cols]
                s = _nt_dot(q2, k)
            outs.append(_merge_heads(_softmax_pv(s, v, bounded), TILE))
        o_ref[0] = jnp.concatenate(outs, axis=-1).astype(o_ref.dtype)

    if tile_off == 0:
        is_ctx = pl.program_id(1) == 0
        pl.when(is_ctx)(lambda: _when_bounded(flag_ref, functools.partial(attend, False)))
        pl.when(jnp.logical_not(is_ctx))(lambda: _when_bounded(flag_ref, functools.partial(attend, True)))
    else:
        _when_bounded(flag_ref, functools.partial(attend, True))


def _na(bounded, qkv, table, *, need_ctx, n_ctx):
    n_batch, n_tok, _ = qkv.shape
    assert n_ctx == TILE
    n_tiles = n_tok // TILE
    off = 0 if need_ctx else 1
    blk = (1, TILE, GROUP_W)
    cur = lambda col: pl.BlockSpec(blk, lambda b, i: (b, i + off, col))
    prev = lambda col: pl.BlockSpec(blk, lambda b, i: (b, jnp.maximum(i + off - 1, 1), col))
    nxt = lambda col: pl.BlockSpec(blk, lambda b, i: (b, jnp.minimum(i + off + 1, n_tiles - 1), col))
    ctx = lambda col: pl.BlockSpec(blk, lambda b, i: (b, 0, col))

    def cls(b, i):
        t = i + off
        return (jnp.where(t == n_tiles - 1, 2, jnp.where(t <= 1, 0, 1)), 0, 0, 0)

    return pl.pallas_call(
        functools.partial(_na_kernel, tile_off=off),
        grid=(n_batch, n_tiles - off),
        in_specs=[_SMEM_SPEC, cur(2), prev(3), cur(3), nxt(3), ctx(3), prev(4), cur(4), nxt(4), ctx(4),
                  pl.BlockSpec((1, NA_HEADS, TILE, 3 * TILE), cls)],
        out_specs=pl.BlockSpec(blk, lambda b, i: (b, i, 0)),
        out_shape=_sds((n_batch, n_tok - off * TILE, GROUP_W), BF16),
        compiler_params=_params(("arbitrary", "arbitrary")),
        name="na",
    )(bounded, *([qkv] * 9), table)


def _na_bias_table(rel_bias, rows):
    n_bands = rows // BAND_ROWS
    assert rows >= WIN_H and rows % BAND_ROWS == 0
    n_dr, n_dc = 2 * WIN_H - 1, 2 * WIN_W - 1
    w = GRID_W
    shift = w - WIN_W
    vp = jnp.pad(rel_bias.astype(F32) * LOG2E, ((0, 0), (0, 0), (shift, 2 * w - shift - n_dc)))
    skew = jnp.broadcast_to(vp[:, :, None, :], (NA_HEADS, n_dr, w, 2 * w)).reshape(NA_HEADS, n_dr, 2 * w * w)
    col = skew[:, :, :w * (2 * w - 1)].reshape(NA_HEADS, n_dr, w, 2 * w - 1)[:, :, :, w - 1:]
    qc = np.arange(w)
    cs = np.clip(qc - WIN_W // 2, 0, w - WIN_W)
    col_ok = (qc[None, :] >= cs[:, None]) & (qc[None, :] < cs[:, None] + WIN_W)
    col = jnp.where(col_ok[None, None], col, NEG_BIG)
    masked = jnp.full((NA_HEADS, w, w), NEG_BIG, F32)
    classes = []
    for band in (0, min(1, n_bands - 1), n_bands - 1):
        q_rows = []
        for i in range(BAND_ROWS):
            r = BAND_ROWS * band + i
            rs = min(max(r - WIN_H // 2, 0), rows - WIN_H)
            blocks = []
            for kk in range(3 * BAND_ROWS):
                kr = BAND_ROWS * (band - 1) + kk
                ok = 0 <= kr < rows and rs <= kr < rs + WIN_H
                blocks.append(col[:, kr - r + WIN_H - 1] if ok else masked)
            q_rows.append(jnp.concatenate(blocks, axis=-1))
        classes.append(jnp.concatenate(q_rows, axis=-2))
    return jnp.stack(classes)


def _fourier_kernel(x_ref, cs_ref, dft_ref, dftc_ref, w_ref, o_ref, x12_ref, y_ref, *, n_ctx, n_lat, tile_off):
    t = pl.program_id(1) + tile_off
    lat_t = t - n_ctx // TILE
    group = y_ref.shape[0] // TILE

    def head_dft(x):
        x12 = jnp.dot(x, cs_ref[...], preferred_element_type=F32).astype(BF16)
        return jnp.concatenate([x12[:, :GROUP_W], x12[:, GROUP_W:]], axis=0)

    def finish(y, n):
        y = y * (1.0 / math.sqrt(n * FNO_HEAD_W))
        return jnp.dot(y.astype(BF16), w_ref[...], preferred_element_type=F32).astype(o_ref.dtype)

    if tile_off == 0:
        @pl.when(t == 0)
        def _():
            x12 = head_dft(x_ref[0, :n_ctx, :])
            o_ref[0] = finish(jnp.dot(dftc_ref[...], x12, preferred_element_type=F32), n_ctx)

    @pl.when(lat_t == 0)
    def _():
        x12_ref[...] = head_dft(x_ref[0, n_ctx:, :])

    @pl.when(jnp.logical_and(lat_t >= 0, lax.rem(lat_t, group) == 0))
    def _():
        row0 = pl.multiple_of(lat_t * TILE, y_ref.shape[0])
        rows = dft_ref[pl.ds(row0, y_ref.shape[0]), :]
        y_ref[...] = finish(jnp.dot(rows, x12_ref[...], preferred_element_type=F32), n_lat)

    @pl.when(lat_t >= 0)
    def _():
        o_ref[0] = y_ref[pl.ds(pl.multiple_of(lax.rem(lat_t, group) * TILE, TILE), TILE), :]


def _fourier(qkv, cs_bd, dft, dft_ctx, w_f, *, need_ctx, n_ctx):
    n_batch, n_tok, _ = qkv.shape
    n_lat = n_tok - n_ctx
    off = 0 if need_ctx else n_ctx // TILE
    const = lambda shape: pl.BlockSpec(shape, lambda b, i: (0,) * len(shape))
    return pl.pallas_call(
        functools.partial(_fourier_kernel, n_ctx=n_ctx, n_lat=n_lat, tile_off=off),
        grid=(n_batch, n_tok // TILE - off),
        in_specs=[pl.BlockSpec((1, n_tok, GROUP_W), lambda b, i: (b, 0, 5)),
                  const(cs_bd.shape), const(dft.shape), const(dft_ctx.shape), const(w_f.shape)],
        out_specs=pl.BlockSpec((1, TILE, GROUP_W), lambda b, i: (b, i, 0)),
        out_shape=_sds((n_batch, n_tok - off * TILE, GROUP_W), BF16),
        scratch_shapes=[pltpu.VMEM((2 * n_lat, GROUP_W), BF16),
                        pltpu.VMEM((FOURIER_ROWS if n_lat % FOURIER_ROWS == 0 else TILE, GROUP_W), BF16)],
        compiler_params=_params(("arbitrary", "arbitrary")),
        name="fourier",
    )(qkv, cs_bd, dft, dft_ctx, w_f)


def _dft_matrix(n):
    f = 32
    assert n % f == 0
    k = lax.broadcasted_iota(jnp.int32, (n, 1), 0)
    m1 = lax.broadcasted_iota(jnp.int32, (1, n // f), 1)
    m0 = lax.broadcasted_iota(jnp.int32, (1, f), 1)
    ang_a = ((k * (m1 * f)) % n).astype(F32) * (2.0 * math.pi / n)
    ang_b = ((k * m0) % n).astype(F32) * (2.0 * math.pi / n)
    ca, sa = jnp.cos(ang_a)[:, :, None], jnp.sin(ang_a)[:, :, None]
    cb, sb = jnp.cos(ang_b)[:, None, :], jnp.sin(ang_b)[:, None, :]
    cos = (ca * cb - sa * sb).reshape(n, n)
    sin = (sa * cb + ca * sb).reshape(n, n)
    return jnp.concatenate([cos, -sin], axis=1).astype(BF16)


def _head_dft_matrix():
    a = np.arange(GROUP_W)
    same = (a[:, None] // FNO_HEAD_W) == (a[None, :] // FNO_HEAD_W)
    ang = 2.0 * np.pi * (((a[:, None] % FNO_HEAD_W) * (a[None, :] % FNO_HEAD_W)) % FNO_HEAD_W) / FNO_HEAD_W
    m = np.concatenate([np.where(same, np.cos(ang), 0.0), np.where(same, np.sin(ang), 0.0)], axis=1)
    return jnp.asarray(m, F32).astype(BF16)


def _s5_kernel(u_ref, a_ref, bd_ref, cd_ref, y_ref, bu_ref, hb_ref, h_ref, *, n_batch, reverse):
    @pl.when(pl.program_id(0) == 0)
    def _():
        h_ref[...] = jnp.zeros_like(h_ref)

    sub_rows = S5_SUB * n_batch
    chunk = 4 * LANES
    def project(q):
        u = pltpu.einshape("btc->tbc", u_ref[:, q * S5_SUB:(q + 1) * S5_SUB, :])
        u = u.reshape(sub_rows, GROUP_W).astype(BF16)
        bu_ref[q * sub_rows:(q + 1) * sub_rows, :] = jnp.dot(u, bd_ref[0], preferred_element_type=F32)

    def scan(q):
        steps = range(S5_SUB - 1, -1, -1) if reverse else range(S5_SUB)
        for sc in range(SSM_W // chunk):
            re = slice(sc * chunk, (sc + 1) * chunk)
            im = slice(SSM_W + sc * chunk, SSM_W + (sc + 1) * chunk)
            a_re = jnp.broadcast_to(a_ref[0, :, re], (n_batch, chunk))
            a_im = jnp.broadcast_to(a_ref[0, :, im], (n_batch, chunk))
            h_re, h_im = h_ref[:, re], h_ref[:, im]
            for t in steps:
                r = slice(q * sub_rows + t * n_batch, q * sub_rows + (t + 1) * n_batch)
                h_re, h_im = (a_re * h_re - a_im * h_im + bu_ref[r, re],
                              a_re * h_im + a_im * h_re + bu_ref[r, im])
                hb_ref[r, re] = h_re.astype(BF16)
                hb_ref[r, im] = h_im.astype(BF16)
            h_ref[:, re] = h_re
            h_ref[:, im] = h_im

    def readout(q):
        y = jnp.dot(hb_ref[q * sub_rows:(q + 1) * sub_rows, :], cd_ref[0], preferred_element_type=F32)
        y = pltpu.einshape("tbc->btc", y.reshape(S5_SUB, n_batch, GROUP_W))
        y_ref[:, q * S5_SUB:(q + 1) * S5_SUB, :] = y.astype(y_ref.dtype)

    order = list(range(S5_STEPS // S5_SUB))
    if reverse:
        order.reverse()
    project(order[0])
    for j, q in enumerate(order):
        if j + 1 < len(order):
            project(order[j + 1])
        scan(q)
        if j > 0:
            readout(order[j - 1])
    readout(order[-1])


def _s5(u, a_tab, b_dense, c_dense, *, n_ctx, reverse):
    n_batch, n_tok, _ = u.shape
    n_blk = n_tok // S5_STEPS
    n_cblk = n_ctx // S5_STEPS
    d = int(reverse)
    if reverse:
        blk = lambda s: jnp.where(s < n_cblk, n_cblk - 1 - s, n_blk - 1 - (s - n_cblk))
    else:
        blk = lambda s: s
    return pl.pallas_call(
        functools.partial(_s5_kernel, n_batch=n_batch, reverse=reverse),
        grid=(n_blk,),
        in_specs=[pl.BlockSpec((n_batch, S5_STEPS, GROUP_W), lambda s: (0, blk(s), 0)),
                  pl.BlockSpec((1, 1, 2 * SSM_W), lambda s: (d, 0, 0)),
                  pl.BlockSpec((1, GROUP_W, 2 * SSM_W), lambda s: (d, 0, 0)),
                  pl.BlockSpec((1, 2 * SSM_W, GROUP_W), lambda s: (d, 0, 0))],
        out_specs=pl.BlockSpec((n_batch, S5_STEPS, GROUP_W), lambda s: (0, blk(s), 0)),
        out_shape=_sds((n_batch, n_tok, GROUP_W), BF16),
        scratch_shapes=[pltpu.VMEM((S5_STEPS * n_batch, 2 * SSM_W), F32),
                        pltpu.VMEM((S5_STEPS * n_batch, 2 * SSM_W), BF16),
                        pltpu.VMEM((n_batch, 2 * SSM_W), F32)],
        compiler_params=_params(("arbitrary",)),
        name="s5",
    )(u, a_tab, b_dense, c_dense)


def _s5_tables(lam_re, lam_im, log_dt, b_re, b_im, c_re, c_im):
    lam = lax.complex(lam_re.astype(F32), lam_im.astype(F32))
    dt = jnp.exp(log_dt.astype(F32))[..., None]
    a_bar = jnp.exp(lam * dt)
    b_bar = ((a_bar - 1.0) / lam)[..., None] * lax.complex(b_re.astype(F32), b_im.astype(F32))
    eye = jnp.eye(SSM_GROUPS, dtype=F32)

    def embed_b(m):
        m = jnp.transpose(m, (0, 1, 3, 2))[:, :, :, None, :] * eye[None, :, None, :, None]
        return m.reshape(2, GROUP_W, SSM_W)

    def embed_c(m):
        m = jnp.transpose(m, (0, 1, 3, 2))[:, :, :, None, :] * eye[None, :, None, :, None]
        return m.reshape(2, SSM_W, GROUP_W)

    a_tab = jnp.concatenate([jnp.real(a_bar).reshape(2, 1, SSM_W), jnp.imag(a_bar).reshape(2, 1, SSM_W)], axis=-1)
    b_dense = jnp.concatenate([embed_b(jnp.real(b_bar)), embed_b(jnp.imag(b_bar))], axis=-1).astype(BF16)
    c_dense = jnp.concatenate([embed_c(c_re.astype(F32)), -embed_c(c_im.astype(F32))], axis=1).astype(BF16)
    return a_tab, b_dense, c_dense


def _gelu_tanh(x):
    return x * (0.5 * (1.0 + jnp.tanh(math.sqrt(2.0 / math.pi) * (x + 0.044715 * (x * x * x)))))


def _mixout_kernel(*refs, split_input, tiles_per_batch, tile_off):
    if split_input:
        x_ref, ctx_ref, *refs = refs
    else:
        x_ref, *refs = refs
    (mod_ref, oa_ref, od_ref, of_ref, u_ref, yf_ref, yb_ref, d_ref, wg_ref, bg_ref, gg_ref, wo_ref,
     g2_ref, w1_ref, w3_ref, w2_ref, o_ref, x1_ref, h_ref, ga2_ref) = refs
    s = pl.program_id(0)
    n_items = pl.num_programs(0) - 1

    def feed_forward():
        h = h_ref[...]
        acc = jnp.zeros((TILE, D_MODEL), F32)
        for c in range(D_FF // FF_CHUNK):
            cols = slice(c * FF_CHUNK, (c + 1) * FF_CHUNK)
            a = jnp.dot(h, w1_ref[:, cols], preferred_element_type=F32)
            b = jnp.dot(h, w3_ref[:, cols], preferred_element_type=F32)
            t = (a * jax.nn.sigmoid(a)) * b
            acc = acc + jnp.dot(t.astype(BF16), w2_ref[cols, :], preferred_element_type=F32)
        o_ref[0] = x1_ref[...] + ga2_ref[...] * acc

    def mix():
        x = x_ref[0]
        if split_input:
            is_ctx = lax.rem(jnp.minimum(s, n_items - 1), tiles_per_batch) + tile_off == 0
            x = jnp.where(is_ctx, ctx_ref[0], x)
        _, _, ga1 = _mod_slices(mod_ref[0], 0)
        sh2, sc2, ga2 = _mod_slices(mod_ref[0], 3)
        y = u_ref[0] * d_ref[...] + yf_ref[0].astype(F32) + yb_ref[0].astype(F32)
        g = _gelu_tanh(y)
        z = jnp.dot(g.astype(BF16), wg_ref[...], preferred_element_type=F32) + bg_ref[...]
        o_s = g * jax.nn.sigmoid(z)
        parts = [oa_ref[0].astype(F32), od_ref[0].astype(F32), of_ref[0].astype(F32), o_s]
        normed = [(_rms_rows(p) * gg_ref[:, k * GROUP_W:(k + 1) * GROUP_W]).astype(BF16)
                  for k, p in enumerate(parts)]
        r = jnp.dot(jnp.concatenate(normed, axis=-1), wo_ref[...], preferred_element_type=F32)
        x = x + ga1 * r
        x1_ref[...] = x
        h_ref[...] = ((_rms_rows(x) * g2_ref[...]) * (1.0 + sc2) + sh2).astype(BF16)
        ga2_ref[...] = ga2

    pl.when(s == 0)(mix)

    @pl.when(jnp.logical_and(s > 0, s < n_items))
    def _():
        feed_forward()
        mix()

    pl.when(s == n_items)(feed_forward)


def _mixout_ffn(xs, mod, o_a, o_d, o_f, u, y_fwd, y_bwd, d_skip, w_glu, b_glu, g_group, w_out, g2, w1, w3, w2, *,
                need_ctx, n_ctx):
    split_input = len(xs) == 2
    n_batch, n_tok, _ = u.shape
    off = 0 if need_ctx else n_ctx // TILE
    n_out = n_tok - off * TILE
    nt = n_out // TILE
    n_items = n_batch * nt
    tok = (1, TILE, D_MODEL)

    def item(fn):
        def index_map(s):
            a = jnp.minimum(s, n_items - 1)
            return fn(a // nt, a % nt)
        return index_map

    def out_map(s):
        a = jnp.maximum(s - 1, 0)
        return (a // nt, a % nt, 0)

    if split_input:
        assert need_ctx
        x_specs = [pl.BlockSpec(tok, item(lambda b, i: (b, jnp.maximum(i - 1, 0), 0))),
                   pl.BlockSpec(tok, item(lambda b, i: (b, 0, 0)))]
    else:
        x_specs = [pl.BlockSpec(tok, item(lambda b, i: (b, i + off, 0)))]
    grp = pl.BlockSpec((1, TILE, GROUP_W), item(lambda b, i: (b, i + off, 0)))
    mix = pl.BlockSpec((1, TILE, GROUP_W), item(lambda b, i: (b, i, 0)))
    const = lambda shape: pl.BlockSpec(shape, lambda s: (0,) * len(shape), pipeline_mode=pl.Buffered(1))
    return pl.pallas_call(
        functools.partial(_mixout_kernel, split_input=split_input, tiles_per_batch=nt, tile_off=off),
        grid=(n_items + 1,),
        in_specs=x_specs + [
            pl.BlockSpec((1, 1, N_MOD * D_MODEL), item(lambda b, i: (jnp.where(i + off == 0, n_batch, b), 0, 0))),
            mix, mix, mix, grp, grp, grp,
            const((1, GROUP_W)), const((GROUP_W, GROUP_W)), const((1, GROUP_W)),
            const((1, D_MODEL)), const((D_MODEL, D_MODEL)),
            const((1, D_MODEL)), const(w1.shape), const(w3.shape), const(w2.shape),
        ],
        out_specs=pl.BlockSpec(tok, out_map),
        out_shape=_sds((n_batch, n_out, D_MODEL), F32),
        scratch_shapes=[pltpu.VMEM((TILE, D_MODEL), F32), pltpu.VMEM((TILE, D_MODEL), BF16),
                        pltpu.VMEM((1, D_MODEL), F32)],
        compiler_params=_params(("arbitrary",)),
        name="mixffn",
    )(*xs, mod, o_a, o_d, o_f, u, y_fwd, y_bwd, d_skip, w_glu, b_glu, g_group, w_out, g2, w1, w3, w2)


def _rope_tables(n_lat, n_ctx):
    t = jnp.arange(n_lat, dtype=jnp.int32)
    rows = (t // GRID_W).astype(F32)
    cols = (t % GRID_W).astype(F32)
    axis_dim = HEAD_DIM // 2
    inv_freq = ROPE_THETA ** (-jnp.arange(0, axis_dim, 2, dtype=F32) / axis_dim)
    ang_r = rows[:, None] * inv_freq[None, :]
    ang_c = cols[:, None] * inv_freq[None, :]
    cos = jnp.concatenate([jnp.cos(ang_r)] * 2 + [jnp.cos(ang_c)] * 2, axis=-1)
    sin = jnp.concatenate([-jnp.sin(ang_r), jnp.sin(ang_r), -jnp.sin(ang_c), jnp.sin(ang_c)], axis=-1)
    cos = jnp.concatenate([jnp.ones((n_ctx, HEAD_DIM), F32), cos], axis=0)
    sin = jnp.concatenate([jnp.zeros((n_ctx, HEAD_DIM), F32), sin], axis=0)
    return jnp.tile(cos, (1, 2)), jnp.tile(sin, (1, 2))


def kernel(x, c, ctx, c_ctx, w_mod, b_mod, g_norm1, w_in, att_q_gain, att_k_gain, na_q_gain, na_k_gain, na_rel_bias, w_fourier, ssm_lam_re, ssm_lam_im, ssm_log_dt, ssm_b_re, ssm_b_im, ssm_c_re, ssm_c_im, ssm_d, w_glu, b_glu, g_group, w_out, g_norm2, w_ff1, w_ff3, w_ff2):
    n_batch, n_lat, d = x.shape
    n_ctx = ctx.shape[1]
    n_tok = n_lat + n_ctx
    assert d == D_MODEL and n_ctx == TILE and n_lat % TILE == 0 and n_lat % GRID_W == 0
    assert n_batch % 8 == 0 and n_tok % S5_STEPS == 0

    mod_rows = -(-(n_batch + 1) // 8) * 8
    cc = jnp.concatenate([c, c_ctx[None, :], jnp.zeros((mod_rows - n_batch - 1, d), F32)], axis=0)
    mod = _modulation(cc, w_mod, b_mod.reshape(DEPTH, 1, N_MOD * d)).reshape(DEPTH, mod_rows, 1, N_MOD * d)

    cos_t, sin_t = _rope_tables(n_lat, n_ctx)
    cs_bd = _head_dft_matrix()
    dft = _dft_matrix(n_lat)
    dft_ctx = _dft_matrix(n_ctx)
    def q_perm(a, axis):
        heads = [lax.slice_in_dim(a, h * HEAD_DIM, (h + 1) * HEAD_DIM, axis=axis) for h in _Q_ORDER]
        return jnp.concatenate(heads + [lax.slice_in_dim(a, GROUP_W, a.shape[axis], axis=axis)], axis=axis)

    scale = HEAD_DIM ** -0.5 * LOG2E
    tile2 = lambda g, s=1.0: jnp.tile(g.astype(F32) * s, 2)[None, :]
    row = lambda v: v.astype(F32)[None, :]

    xs = (x, ctx)
    for l in range(DEPTH):
        need_ctx = l < DEPTH - 1
        qkv, u = _inproj(xs, mod[l], row(g_norm1[l]), q_perm(w_in[l], 1).astype(BF16), cos_t, sin_t,
                         tile2(att_q_gain[l], scale), tile2(att_k_gain[l]),
                         tile2(na_q_gain[l], scale), tile2(na_k_gain[l]),
                         n_batch=n_batch, n_tok=n_tok)
        o_a = _gqa(_logits_bounded(att_q_gain[l], att_k_gain[l]), qkv, need_ctx=need_ctx, n_ctx=n_ctx)
        o_d = _na(_logits_bounded(na_q_gain[l], na_k_gain[l], jnp.max(jnp.abs(na_rel_bias[l]))), qkv,
                  _na_bias_table(na_rel_bias[l], n_lat // GRID_W), need_ctx=need_ctx, n_ctx=n_ctx)
        o_f = _fourier(qkv, cs_bd, dft, dft_ctx, w_fourier[l].astype(BF16), need_ctx=need_ctx, n_ctx=n_ctx)
        a_tab, b_dense, c_dense = _s5_tables(ssm_lam_re[l], ssm_lam_im[l], ssm_log_dt[l], ssm_b_re[l],
                                             ssm_b_im[l], ssm_c_re[l], ssm_c_im[l])
        y_fwd = _s5(u, a_tab, b_dense, c_dense, n_ctx=n_ctx, reverse=False)
        y_bwd = _s5(u, a_tab, b_dense, c_dense, n_ctx=n_ctx, reverse=True)
        x2 = _mixout_ffn(xs, mod[l], o_a, o_d, o_f, u, y_fwd, y_bwd, row(ssm_d[l]), w_glu[l].astype(BF16),
                         row(b_glu[l]), row(q_perm(g_group[l], 0)), q_perm(w_out[l], 0).astype(BF16),
                         row(g_norm2[l]), w_ff1[l].astype(BF16), w_ff3[l].astype(BF16), w_ff2[l].astype(BF16),
                         need_ctx=need_ctx, n_ctx=n_ctx)
        xs = (x2,)
    return xs[0]
```

```python
import functools
import math

import numpy as np
import jax
import jax.numpy as jnp
from jax import lax
from jax.experimental import pallas as pl
from jax.experimental.pallas import tpu as pltpu

F32 = jnp.float32
BF16 = jnp.bfloat16

D_MODEL = 1024
DEPTH = 2
GRID_W = 64
HEAD_DIM = 64
N_GROUPS = 4
GROUP_W = D_MODEL // N_GROUPS
WIN_H = 8
WIN_W = 16
NA_HEADS = GROUP_W // HEAD_DIM
FNO_HEAD_W = 64
SSM_GROUP = 16
SSM_GROUPS = GROUP_W // SSM_GROUP
SSM_STATE = 64
SSM_W = SSM_GROUPS * SSM_STATE
ROPE_THETA = 10000.0
D_FF = ((8 * D_MODEL // 3 + 255) // 256) * 256
N_MOD = 6
EPS = 1e-6
IN_W = 7 * GROUP_W
QKV_W = 6 * GROUP_W

LANES = 128
TILE = 256
BAND_ROWS = TILE // GRID_W
S5_STEPS = 128
S5_SUB = 32
FF_CHUNK = D_FF
ROPE_BLOCK = HEAD_DIM // 4
VMEM_LIMIT_BYTES = 56 * 2**20
NEG_BIG = -1e30
LOG2E = 1.0 / math.log(2.0)
EXP2_SAFE = 60.0

_Q_ORDER = (0, 2, 1, 3)


def _params(sem):
    return pltpu.CompilerParams(dimension_semantics=sem, vmem_limit_bytes=VMEM_LIMIT_BYTES)


def _sds(shape, dtype):
    return jax.ShapeDtypeStruct(shape, dtype)


def _mod_kernel(c_ref, w_ref, b_ref, o_ref):
    c = c_ref[...]
    s = c * jax.nn.sigmoid(c)
    w = w_ref[0]
    s_hi, w_hi = s.astype(BF16), w.astype(BF16)
    s_lo = (s - s_hi.astype(F32)).astype(BF16)
    w_lo = (w - w_hi.astype(F32)).astype(BF16)
    dot = functools.partial(jnp.dot, preferred_element_type=F32)
    o_ref[0] = dot(s_hi, w_hi) + (dot(s_lo, w_hi) + dot(s_hi, w_lo)) + b_ref[0]


def _modulation(cc, w_mod, b_mod):
    rows, d = cc.shape
    n_blk = 4
    bw = N_MOD * d // n_blk
    return pl.pallas_call(
        _mod_kernel,
        grid=(DEPTH, n_blk),
        in_specs=[
            pl.BlockSpec((rows, d), lambda l, j: (0, 0)),
            pl.BlockSpec((1, d, bw), lambda l, j: (l, 0, j)),
            pl.BlockSpec((1, 1, bw), lambda l, j: (l, 0, j)),
        ],
        out_specs=pl.BlockSpec((1, rows, bw), lambda l, j: (l, 0, j)),
        out_shape=_sds((DEPTH, rows, N_MOD * d), F32),
        compiler_params=_params(("arbitrary", "arbitrary")),
        name="modulation",
    )(cc, w_mod, b_mod)


def _rms_rows(x):
    return x * lax.rsqrt(jnp.mean(x * x, axis=-1, keepdims=True) + EPS)


def _mod_slices(m, first):
    return [m[:, (first + k) * D_MODEL:(first + k + 1) * D_MODEL] for k in range(3)]


def _head_norm(t, gain):
    lo = lax.broadcasted_iota(jnp.int32, t.shape, 1) < HEAD_DIM
    sq = t * t
    s_lo = jnp.sum(jnp.where(lo, sq, 0.0), axis=-1, keepdims=True)
    s_hi = jnp.sum(jnp.where(lo, 0.0, sq), axis=-1, keepdims=True)
    ms = jnp.where(lo, s_lo, s_hi) * (1.0 / HEAD_DIM)
    return t * lax.rsqrt(ms + EPS) * gain


def _rope(t, cos, sin):
    first = (lax.broadcasted_iota(jnp.int32, t.shape, 1) & ROPE_BLOCK) == 0
    partner = jnp.where(first, pltpu.roll(t, LANES - ROPE_BLOCK, axis=1), pltpu.roll(t, ROPE_BLOCK, axis=1))
    return t * cos + partner * sin


def _softmax_pv(s, v, bounded):
    p = jnp.exp2(s if bounded else s - jnp.max(s, axis=-1, keepdims=True))
    l = jnp.sum(p, axis=-1, keepdims=True)
    return jnp.dot(p.astype(BF16), v, preferred_element_type=F32) / l


def _when_bounded(flag_ref, body):
    bounded = flag_ref[0] == 1
    pl.when(bounded)(lambda: body(True))
    pl.when(jnp.logical_not(bounded))(lambda: body(False))


def _logits_bounded(gq, gk, extra=0.0):
    bound = math.sqrt(HEAD_DIM) * jnp.max(jnp.abs(gq)) * jnp.max(jnp.abs(gk)) + extra
    return (bound * LOG2E <= EXP2_SAFE).astype(jnp.int32).reshape(1)


def _split_heads(q):
    lo = lax.broadcasted_iota(jnp.int32, q.shape, 1) < HEAD_DIM
    zero = jnp.zeros_like(q)
    return jnp.concatenate([jnp.where(lo, q, zero), jnp.where(lo, zero, q)], axis=0)


def _merge_heads(o, rows):
    lo = lax.broadcasted_iota(jnp.int32, (rows, LANES), 1) < HEAD_DIM
    return jnp.where(lo, o[:rows], o[rows:])


def _nt_dot(a, b):
    return lax.dot_general(a, b, (((1,), (1,)), ((), ())), preferred_element_type=F32)


def _inproj_kernel(*refs, split_input, tiles_per_batch):
    if split_input:
        x_ref, ctx_ref, *refs = refs
    else:
        x_ref, *refs = refs
    (mod_ref, g1_ref, w_ref, cos_ref, sin_ref, gq_ref, gk_ref, nq_ref, nk_ref, qkv_ref, u_ref, h_ref) = refs
    s = pl.program_id(0)
    n_items = pl.num_programs(0) - 1

    def modulate():
        x = x_ref[0]
        if split_input:
            x = jnp.where(lax.rem(jnp.minimum(s, n_items - 1), tiles_per_batch) == 0, ctx_ref[0], x)
        sh1, sc1, _ = _mod_slices(mod_ref[0], 0)
        h_ref[...] = ((_rms_rows(x) * g1_ref[...]) * (1.0 + sc1) + sh1).astype(BF16)

    def project():
        p = jnp.dot(h_ref[...], w_ref[...], preferred_element_type=F32)
        cos = cos_ref[...]
        sin = sin_ref[...]

        def tile(j):
            return p[:, j * LANES:(j + 1) * LANES]

        out = [
            _rope(_head_norm(tile(0), gq_ref[...]), cos, sin),
            _rope(_head_norm(tile(1), gq_ref[...]), cos, sin),
            _rope(_head_norm(tile(2), gk_ref[...]), cos, sin),
            tile(3),
            _head_norm(tile(4), nq_ref[...]),
            _head_norm(tile(5), nq_ref[...]),
            _head_norm(tile(6), nk_ref[...]),
            _head_norm(tile(7), nk_ref[...]),
            tile(8), tile(9), tile(10), tile(11),
        ]
        for j, t in enumerate(out):
            qkv_ref[0, :, j * LANES:(j + 1) * LANES] = t.astype(BF16)
        u_ref[0] = p[:, QKV_W:]

    pl.when(s == 0)(modulate)

    @pl.when(jnp.logical_and(s > 0, s < n_items))
    def _():
        project()
        modulate()

    pl.when(s == n_items)(project)


def _inproj(xs, mod, g1, w_in, cos_t, sin_t, gq, gk, nq, nk, *, n_batch, n_tok):
    split_input = len(xs) == 2
    nt = n_tok // TILE
    n_items = n_batch * nt
    tok = (1, TILE, D_MODEL)

    def item(fn):
        def index_map(s):
            a = jnp.minimum(s, n_items - 1)
            return fn(a // nt, a % nt)
        return index_map

    def done(fn):
        def index_map(s):
            a = jnp.maximum(s - 1, 0)
            return fn(a // nt, a % nt)
        return index_map

    if split_input:
        x_specs = [pl.BlockSpec(tok, item(lambda b, i: (b, jnp.maximum(i - 1, 0), 0))),
                   pl.BlockSpec(tok, item(lambda b, i: (b, 0, 0)))]
    else:
        x_specs = [pl.BlockSpec(tok, item(lambda b, i: (b, i, 0)))]
    vec = lambda w: pl.BlockSpec((1, w), lambda s: (0, 0))
    return pl.pallas_call(
        functools.partial(_inproj_kernel, split_input=split_input, tiles_per_batch=nt),
        grid=(n_items + 1,),
        in_specs=x_specs + [
            pl.BlockSpec((1, 1, N_MOD * D_MODEL), item(lambda b, i: (jnp.where(i == 0, n_batch, b), 0, 0))),
            vec(D_MODEL),
            pl.BlockSpec((D_MODEL, IN_W), lambda s: (0, 0)),
            pl.BlockSpec((TILE, LANES), done(lambda b, i: (i, 0))),
            pl.BlockSpec((TILE, LANES), done(lambda b, i: (i, 0))),
            vec(LANES), vec(LANES), vec(LANES), vec(LANES),
        ],
        out_specs=[pl.BlockSpec((1, TILE, QKV_W), done(lambda b, i: (b, i, 0))),
                   pl.BlockSpec((1, TILE, GROUP_W), done(lambda b, i: (b, i, 0)))],
        out_shape=[_sds((n_batch, n_tok, QKV_W), BF16), _sds((n_batch, n_tok, GROUP_W), F32)],
        scratch_shapes=[pltpu.VMEM((TILE, D_MODEL), BF16)],
        compiler_params=_params(("arbitrary",)),
        name="inproj",
    )(*xs, mod, g1, w_in, cos_t, sin_t, gq, gk, nq, nk)


def _gqa_kernel(flag_ref, qa_ref, qb_ref, k_ref, v_ref, o_ref, *, n_ctx, tile_off):
    def attend(n_keys, bounded):
        k = k_ref[0, :n_keys, :]
        v = v_ref[0, :n_keys, :]
        outs = []
        for q_ref in (qa_ref, qb_ref):
            q2 = _split_heads(q_ref[0])
            outs.append(_merge_heads(_softmax_pv(_nt_dot(q2, k), v, bounded), TILE))
        o_ref[0] = jnp.concatenate(outs, axis=-1).astype(o_ref.dtype)

    if tile_off == 0:
        is_ctx = pl.program_id(1) == 0
        pl.when(is_ctx)(lambda: _when_bounded(flag_ref, functools.partial(attend, n_ctx)))
        pl.when(jnp.logical_not(is_ctx))(lambda: _when_bounded(flag_ref, functools.partial(attend, k_ref.shape[1])))
    else:
        _when_bounded(flag_ref, functools.partial(attend, k_ref.shape[1]))


_SMEM_SPEC = pl.BlockSpec(memory_space=pltpu.SMEM)


def _gqa(bounded, qkv, *, need_ctx, n_ctx):
    n_batch, n_tok, _ = qkv.shape
    off = 0 if need_ctx else n_ctx // TILE
    q_spec = lambda col: pl.BlockSpec((1, TILE, LANES), lambda b, i: (b, i + off, col))
    kv_spec = lambda col: pl.BlockSpec((1, n_tok, LANES), lambda b, i: (b, 0, col))
    return pl.pallas_call(
        functools.partial(_gqa_kernel, n_ctx=n_ctx, tile_off=off),
        grid=(n_batch, n_tok // TILE - off),
        in_specs=[_SMEM_SPEC, q_spec(0), q_spec(1), kv_spec(2), kv_spec(3)],
        out_specs=pl.BlockSpec((1, TILE, GROUP_W), lambda b, i: (b, i, 0)),
        out_shape=_sds((n_batch, n_tok - off * TILE, GROUP_W), BF16),
        compiler_params=_params(("arbitrary", "arbitrary")),
        name="gqa",
    )(bounded, qkv, qkv, qkv, qkv)


def _na_kernel(flag_ref, q_ref, kp_ref, kc_ref, kn_ref, kx_ref, vp_ref, vc_ref, vn_ref, vx_ref, tbl_ref, o_ref, *, tile_off):
    def attend(local, bounded):
        outs = []
        for pair in range(2):
            cols = slice(pair * LANES, (pair + 1) * LANES)
            q2 = _split_heads(q_ref[0, :, cols])
            if local:
                k = jnp.concatenate([r[0, :, cols] for r in (kp_ref, kc_ref, kn_ref, kx_ref)], axis=0)
                v = jnp.concatenate([r[0, :, cols] for r in (vp_ref, vc_ref, vn_ref, vx_ref)], axis=0)
                bias = jnp.concatenate([tbl_ref[0, 2 * pair], tbl_ref[0, 2 * pair + 1]], axis=0)
                s = _nt_dot(q2, k)
                s = jnp.concatenate([s[:, :3 * TILE] + bias, s[:, 3 * TILE:]], axis=-1)
            else:
                k = kx_ref[0, :, cols]
                v = vx_ref[0, :, cols]
                s = _nt_dot(q2, k)
            outs.append(_merge_heads(_softmax_pv(s, v, bounded), TILE))
        o_ref[0] = jnp.concatenate(outs, axis=-1).astype(o_ref.dtype)

    if tile_off == 0:
        is_ctx = pl.program_id(1) == 0
        pl.when(is_ctx)(lambda: _when_bounded(flag_ref, functools.partial(attend, False)))
        pl.when(jnp.logical_not(is_ctx))(lambda: _when_bounded(flag_ref, functools.partial(attend, True)))
    else:
        _when_bounded(flag_ref, functools.partial(attend, True))


def _na(bounded, qkv, table, *, need_ctx, n_ctx):
    n_batch, n_tok, _ = qkv.shape
    assert n_ctx == TILE
    n_tiles = n_tok // TILE
    off = 0 if need_ctx else 1
    blk = (1, TILE, GROUP_W)
    cur = lambda col: pl.BlockSpec(blk, lambda b, i: (b, i + off, col))
    prev = lambda col: pl.BlockSpec(blk, lambda b, i: (b, jnp.maximum(i + off - 1, 1), col))
    nxt = lambda col: pl.BlockSpec(blk, lambda b, i: (b, jnp.minimum(i + off + 1, n_tiles - 1), col))
    ctx = lambda col: pl.BlockSpec(blk, lambda b, i: (b, 0, col))

    def cls(b, i):
        t = i + off
        return (jnp.where(t == n_tiles - 1, 2, jnp.where(t <= 1, 0, 1)), 0, 0, 0)

    return pl.pallas_call(
        functools.partial(_na_kernel, tile_off=off),
        grid=(n_batch, n_tiles - off),
        in_specs=[_SMEM_SPEC, cur(2), prev(3), cur(3), nxt(3), ctx(3), prev(4), cur(4), nxt(4), ctx(4),
                  pl.BlockSpec((1, NA_HEADS, TILE, 3 * TILE), cls)],
        out_specs=pl.BlockSpec(blk, lambda b, i: (b, i, 0)),
        out_shape=_sds((n_batch, n_tok - off * TILE, GROUP_W), BF16),
        compiler_params=_params(("arbitrary", "arbitrary")),
        name="na",
    )(bounded, *([qkv] * 9), table)


def _na_bias_table(rel_bias, rows):
    n_bands = rows // BAND_ROWS
    assert rows >= WIN_H and rows % BAND_ROWS == 0
    n_dr, n_dc = 2 * WIN_H - 1, 2 * WIN_W - 1
    w = GRID_W
    shift = w - WIN_W
    vp = jnp.pad(rel_bias.astype(F32) * LOG2E, ((0, 0), (0, 0), (shift, 2 * w - shift - n_dc)))
    skew = jnp.broadcast_to(vp[:, :, None, :], (NA_HEADS, n_dr, w, 2 * w)).reshape(NA_HEADS, n_dr, 2 * w * w)
    col = skew[:, :, :w * (2 * w - 1)].reshape(NA_HEADS, n_dr, w, 2 * w - 1)[:, :, :, w - 1:]
    qc = np.arange(w)
    cs = np.clip(qc - WIN_W // 2, 0, w - WIN_W)
    col_ok = (qc[None, :] >= cs[:, None]) & (qc[None, :] < cs[:, None] + WIN_W)
    col = jnp.where(col_ok[None, None], col, NEG_BIG)
    masked = jnp.full((NA_HEADS, w, w), NEG_BIG, F32)
    classes = []
    for band in (0, min(1, n_bands - 1), n_bands - 1):
        q_rows = []
        for i in range(BAND_ROWS):
            r = BAND_ROWS * band + i
            rs = min(max(r - WIN_H // 2, 0), rows - WIN_H)
            blocks = []
            for kk in range(3 * BAND_ROWS):
                kr = BAND_ROWS * (band - 1) + kk
                ok = 0 <= kr < rows and rs <= kr < rs + WIN_H
                blocks.append(col[:, kr - r + WIN_H - 1] if ok else masked)
            q_rows.append(jnp.concatenate(blocks, axis=-1))
        classes.append(jnp.concatenate(q_rows, axis=-2))
    return jnp.stack(classes)


def _fourier_kernel(x_ref, cs_ref, dft_ref, dftc_ref, w_ref, o_ref, x12_ref, *, n_ctx, n_lat, tile_off):
    t = pl.program_id(1) + tile_off

    def head_dft(x):
        x12 = jnp.dot(x, cs_ref[...], preferred_element_type=F32).astype(BF16)
        return jnp.concatenate([x12[:, :GROUP_W], x12[:, GROUP_W:]], axis=0)

    def finish(y, n):
        y = y * (1.0 / math.sqrt(n * FNO_HEAD_W))
        o_ref[0] = jnp.dot(y.astype(BF16), w_ref[...], preferred_element_type=F32).astype(o_ref.dtype)

    if tile_off == 0:
        @pl.when(t == 0)
        def _():
            x12 = head_dft(x_ref[0, :n_ctx, :])
            finish(jnp.dot(dftc_ref[...], x12, preferred_element_type=F32), n_ctx)

    @pl.when(t == n_ctx // TILE)
    def _():
        x12_ref[...] = head_dft(x_ref[0, n_ctx:, :])

    @pl.when(t >= n_ctx // TILE)
    def _():
        row0 = pl.multiple_of((t - n_ctx // TILE) * TILE, TILE)
        finish(jnp.dot(dft_ref[pl.ds(row0, TILE), :], x12_ref[...], preferred_element_type=F32), n_lat)


def _fourier(qkv, cs_bd, dft, dft_ctx, w_f, *, need_ctx, n_ctx):
    n_batch, n_tok, _ = qkv.shape
    n_lat = n_tok - n_ctx
    off = 0 if need_ctx else n_ctx // TILE
    const = lambda shape: pl.BlockSpec(shape, lambda b, i: (0,) * len(shape))
    return pl.pallas_call(
        functools.partial(_fourier_kernel, n_ctx=n_ctx, n_lat=n_lat, tile_off=off),
        grid=(n_batch, n_tok // TILE - off),
        in_specs=[pl.BlockSpec((1, n_tok, GROUP_W), lambda b, i: (b, 0, 5)),
                  const(cs_bd.shape), const(dft.shape), const(dft_ctx.shape), const(w_f.shape)],
        out_specs=pl.BlockSpec((1, TILE, GROUP_W), lambda b, i: (b, i, 0)),
        out_shape=_sds((n_batch, n_tok - off * TILE, GROUP_W), BF16),
        scratch_shapes=[pltpu.VMEM((2 * n_lat, GROUP_W), BF16)],
        compiler_params=_params(("arbitrary", "arbitrary")),
        name="fourier",
    )(qkv, cs_bd, dft, dft_ctx, w_f)


def _dft_matrix(n):
    f = 32
    assert n % f == 0
    k = lax.broadcasted_iota(jnp.int32, (n, 1), 0)
    m1 = lax.broadcasted_iota(jnp.int32, (1, n // f), 1)
    m0 = lax.broadcasted_iota(jnp.int32, (1, f), 1)
    ang_a = ((k * (m1 * f)) % n).astype(F32) * (2.0 * math.pi / n)
    ang_b = ((k * m0) % n).astype(F32) * (2.0 * math.pi / n)
    ca, sa = jnp.cos(ang_a)[:, :, None], jnp.sin(ang_a)[:, :, None]
    cb, sb = jnp.cos(ang_b)[:, None, :], jnp.sin(ang_b)[:, None, :]
    cos = (ca * cb - sa * sb).reshape(n, n)
    sin = (sa * cb + ca * sb).reshape(n, n)
    return jnp.concatenate([cos, -sin], axis=1).astype(BF16)


def _head_dft_matrix():
    a = np.arange(GROUP_W)
    same = (a[:, None] // FNO_HEAD_W) == (a[None, :] // FNO_HEAD_W)
    ang = 2.0 * np.pi * (((a[:, None] % FNO_HEAD_W) * (a[None, :] % FNO_HEAD_W)) % FNO_HEAD_W) / FNO_HEAD_W
    m = np.concatenate([np.where(same, np.cos(ang), 0.0), np.where(same, np.sin(ang), 0.0)], axis=1)
    return jnp.asarray(m, F32).astype(BF16)


def _s5_kernel(u_ref, a_ref, bd_ref, cd_ref, y_ref, bu_ref, hb_ref, h_ref, *, n_batch, reverse):
    @pl.when(pl.program_id(0) == 0)
    def _():
        h_ref[...] = jnp.zeros_like(h_ref)

    sub_rows = S5_SUB * n_batch
    chunk = 4 * LANES
    def project(q):
        u = pltpu.einshape("btc->tbc", u_ref[:, q * S5_SUB:(q + 1) * S5_SUB, :])
        u = u.reshape(sub_rows, GROUP_W).astype(BF16)
        bu_ref[q * sub_rows:(q + 1) * sub_rows, :] = jnp.dot(u, bd_ref[0], preferred_element_type=F32)

    def scan(q):
        steps = range(S5_SUB - 1, -1, -1) if reverse else range(S5_SUB)
        for sc in range(SSM_W // chunk):
            re = slice(sc * chunk, (sc + 1) * chunk)
            im = slice(SSM_W + sc * chunk, SSM_W + (sc + 1) * chunk)
            a_re = jnp.broadcast_to(a_ref[0, :, re], (n_batch, chunk))
            a_im = jnp.broadcast_to(a_ref[0, :, im], (n_batch, chunk))
            h_re, h_im = h_ref[:, re], h_ref[:, im]
            for t in steps:
                r = slice(q * sub_rows + t * n_batch, q * sub_rows + (t + 1) * n_batch)
                h_re, h_im = (a_re * h_re - a_im * h_im + bu_ref[r, re],
                              a_re * h_im + a_im * h_re + bu_ref[r, im])
                hb_ref[r, re] = h_re.astype(BF16)
                hb_ref[r, im] = h_im.astype(BF16)
            h_ref[:, re] = h_re
            h_ref[:, im] = h_im

    def readout(q):
        y = jnp.dot(hb_ref[q * sub_rows:(q + 1) * sub_rows, :], cd_ref[0], preferred_element_type=F32)
        y = pltpu.einshape("tbc->btc", y.reshape(S5_SUB, n_batch, GROUP_W))
        y_ref[:, q * S5_SUB:(q + 1) * S5_SUB, :] = y.astype(y_ref.dtype)

    order = list(range(S5_STEPS // S5_SUB))
    if reverse:
        order.reverse()
    project(order[0])
    for j, q in enumerate(order):
        if j + 1 < len(order):
            project(order[j + 1])
        scan(q)
        if j > 0:
            readout(order[j - 1])
    readout(order[-1])


def _s5(u, a_tab, b_dense, c_dense, *, n_ctx, reverse):
    n_batch, n_tok, _ = u.shape
    n_blk = n_tok // S5_STEPS
    n_cblk = n_ctx // S5_STEPS
    d = int(reverse)
    if reverse:
        blk = lambda s: jnp.where(s < n_cblk, n_cblk - 1 - s, n_blk - 1 - (s - n_cblk))
    else:
        blk = lambda s: s
    return pl.pallas_call(
        functools.partial(_s5_kernel, n_batch=n_batch, reverse=reverse),
        grid=(n_blk,),
        in_specs=[pl.BlockSpec((n_batch, S5_STEPS, GROUP_W), lambda s: (0, blk(s), 0)),
                  pl.BlockSpec((1, 1, 2 * SSM_W), lambda s: (d, 0, 0)),
                  pl.BlockSpec((1, GROUP_W, 2 * SSM_W), lambda s: (d, 0, 0)),
                  pl.BlockSpec((1, 2 * SSM_W, GROUP_W), lambda s: (d, 0, 0))],
        out_specs=pl.BlockSpec((n_batch, S5_STEPS, GROUP_W), lambda s: (0, blk(s), 0)),
        out_shape=_sds((n_batch, n_tok, GROUP_W), BF16),
        scratch_shapes=[pltpu.VMEM((S5_STEPS * n_batch, 2 * SSM_W), F32),
                        pltpu.VMEM((S5_STEPS * n_batch, 2 * SSM_W), BF16),
                        pltpu.VMEM((n_batch, 2 * SSM_W), F32)],
        compiler_params=_params(("arbitrary",)),
        name="s5",
    )(u, a_tab, b_dense, c_dense)


def _s5_tables(lam_re, lam_im, log_dt, b_re, b_im, c_re, c_im):
    lam = lax.complex(lam_re.astype(F32), lam_im.astype(F32))
    dt = jnp.exp(log_dt.astype(F32))[..., None]
    a_bar = jnp.exp(lam * dt)
    b_bar = ((a_bar - 1.0) / lam)[..., None] * lax.complex(b_re.astype(F32), b_im.astype(F32))
    eye = jnp.eye(SSM_GROUPS, dtype=F32)

    def embed_b(m):
        m = jnp.transpose(m, (0, 1, 3, 2))[:, :, :, None, :] * eye[None, :, None, :, None]
        return m.reshape(2, GROUP_W, SSM_W)

    def embed_c(m):
        m = jnp.transpose(m, (0, 1, 3, 2))[:, :, :, None, :] * eye[None, :, None, :, None]
        return m.reshape(2, SSM_W, GROUP_W)

    a_tab = jnp.concatenate([jnp.real(a_bar).reshape(2, 1, SSM_W), jnp.imag(a_bar).reshape(2, 1, SSM_W)], axis=-1)
    b_dense = jnp.concatenate([embed_b(jnp.real(b_bar)), embed_b(jnp.imag(b_bar))], axis=-1).astype(BF16)
    c_dense = jnp.concatenate([embed_c(c_re.astype(F32)), -embed_c(c_im.astype(F32))], axis=1).astype(BF16)
    return a_tab, b_dense, c_dense


def _gelu_tanh(x):
    return x * (0.5 * (1.0 + jnp.tanh(math.sqrt(2.0 / math.pi) * (x + 0.044715 * (x * x * x)))))


def _mixout_kernel(*refs, split_input, tiles_per_batch, tile_off):
    if split_input:
        x_ref, ctx_ref, *refs = refs
    else:
        x_ref, *refs = refs
    (mod_ref, oa_ref, od_ref, of_ref, u_ref, yf_ref, yb_ref, d_ref, wg_ref, bg_ref, gg_ref, wo_ref,
     g2_ref, w1_ref, w3_ref, w2_ref, o_ref, x1_ref, h_ref, ga2_ref) = refs
    s = pl.program_id(0)
    n_items = pl.num_programs(0) - 1

    def feed_forward():
        h = h_ref[...]
        acc = jnp.zeros((TILE, D_MODEL), F32)
        for c in range(D_FF // FF_CHUNK):
            cols = slice(c * FF_CHUNK, (c + 1) * FF_CHUNK)
            a = jnp.dot(h, w1_ref[:, cols], preferred_element_type=F32)
            b = jnp.dot(h, w3_ref[:, cols], preferred_element_type=F32)
            t = (a * jax.nn.sigmoid(a)) * b
            acc = acc + jnp.dot(t.astype(BF16), w2_ref[cols, :], preferred_element_type=F32)
        o_ref[0] = x1_ref[...] + ga2_ref[...] * acc

    def mix():
        x = x_ref[0]
        if split_input:
            is_ctx = lax.rem(jnp.minimum(s, n_items - 1), tiles_per_batch) + tile_off == 0
            x = jnp.where(is_ctx, ctx_ref[0], x)
        _, _, ga1 = _mod_slices(mod_ref[0], 0)
        sh2, sc2, ga2 = _mod_slices(mod_ref[0], 3)
        y = u_ref[0] * d_ref[...] + yf_ref[0].astype(F32) + yb_ref[0].astype(F32)
        g = _gelu_tanh(y)
        z = jnp.dot(g.astype(BF16), wg_ref[...], preferred_element_type=F32) + bg_ref[...]
        o_s = g * jax.nn.sigmoid(z)
        parts = [oa_ref[0].astype(F32), od_ref[0].astype(F32), of_ref[0].astype(F32), o_s]
        normed = [(_rms_rows(p) * gg_ref[:, k * GROUP_W:(k + 1) * GROUP_W]).astype(BF16)
                  for k, p in enumerate(parts)]
        r = jnp.dot(jnp.concatenate(normed, axis=-1), wo_ref[...], preferred_element_type=F32)
        x = x + ga1 * r
        x1_ref[...] = x
        h_ref[...] = ((_rms_rows(x) * g2_ref[...]) * (1.0 + sc2) + sh2).astype(BF16)
        ga2_ref[...] = ga2

    pl.when(s == 0)(mix)

    @pl.when(jnp.logical_and(s > 0, s < n_items))
    def _():
        feed_forward()
        mix()

    pl.when(s == n_items)(feed_forward)


def _mixout_ffn(xs, mod, o_a, o_d, o_f, u, y_fwd, y_bwd, d_skip, w_glu, b_glu, g_group, w_out, g2, w1, w3, w2, *,
                need_ctx, n_ctx):
    split_input = len(xs) == 2
    n_batch, n_tok, _ = u.shape
    off = 0 if need_ctx else n_ctx // TILE
    n_out = n_tok - off * TILE
    nt = n_out // TILE
    n_items = n_batch * nt
    tok = (1, TILE, D_MODEL)

    def item(fn):
        def index_map(s):
            a = jnp.minimum(s, n_items - 1)
            return fn(a // nt, a % nt)
        return index_map

    def out_map(s):
        a = jnp.maximum(s - 1, 0)
        return (a // nt, a % nt, 0)

    if split_input:
        assert need_ctx
        x_specs = [pl.BlockSpec(tok, item(lambda b, i: (b, jnp.maximum(i - 1, 0), 0))),
                   pl.BlockSpec(tok, item(lambda b, i: (b, 0, 0)))]
    else:
        x_specs = [pl.BlockSpec(tok, item(lambda b, i: (b, i + off, 0)))]
    grp = pl.BlockSpec((1, TILE, GROUP_W), item(lambda b, i: (b, i + off, 0)))
    mix = pl.BlockSpec((1, TILE, GROUP_W), item(lambda b, i: (b, i, 0)))
    const = lambda shape: pl.BlockSpec(shape, lambda s: (0,) * len(shape), pipeline_mode=pl.Buffered(1))
    return pl.pallas_call(
        functools.partial(_mixout_kernel, split_input=split_input, tiles_per_batch=nt, tile_off=off),
        grid=(n_items + 1,),
        in_specs=x_specs + [
            pl.BlockSpec((1, 1, N_MOD * D_MODEL), item(lambda b, i: (jnp.where(i + off == 0, n_batch, b), 0, 0))),
            mix, mix, mix, grp, grp, grp,
            const((1, GROUP_W)), const((GROUP_W, GROUP_W)), const((1, GROUP_W)),
            const((1, D_MODEL)), const((D_MODEL, D_MODEL)),
            const((1, D_MODEL)), const(w1.shape), const(w3.shape), const(w2.shape),
        ],
        out_specs=pl.BlockSpec(tok, out_map),
        out_shape=_sds((n_batch, n_out, D_MODEL), F32),
        scratch_shapes=[pltpu.VMEM((TILE, D_MODEL), F32), pltpu.VMEM((TILE, D_MODEL), BF16),
                        pltpu.VMEM((1, D_MODEL), F32)],
        compiler_params=_params(("arbitrary",)),
        name="mixffn",
    )(*xs, mod, o_a, o_d, o_f, u, y_fwd, y_bwd, d_skip, w_glu, b_glu, g_group, w_out, g2, w1, w3, w2)


def _rope_tables(n_lat, n_ctx):
    t = jnp.arange(n_lat, dtype=jnp.int32)
    rows = (t // GRID_W).astype(F32)
    cols = (t % GRID_W).astype(F32)
    axis_dim = HEAD_DIM // 2
    inv_freq = ROPE_THETA ** (-jnp.arange(0, axis_dim, 2, dtype=F32) / axis_dim)
    ang_r = rows[:, None] * inv_freq[None, :]
    ang_c = cols[:, None] * inv_freq[None, :]
    cos = jnp.concatenate([jnp.cos(ang_r)] * 2 + [jnp.cos(ang_c)] * 2, axis=-1)
    sin = jnp.concatenate([-jnp.sin(ang_r), jnp.sin(ang_r), -jnp.sin(ang_c), jnp.sin(ang_c)], axis=-1)
    cos = jnp.concatenate([jnp.ones((n_ctx, HEAD_DIM), F32), cos], axis=0)
    sin = jnp.concatenate([jnp.zeros((n_ctx, HEAD_DIM), F32), sin], axis=0)
    return jnp.tile(cos, (1, 2)), jnp.tile(sin, (1, 2))


def kernel(x, c, ctx, c_ctx, w_mod, b_mod, g_norm1, w_in, att_q_gain, att_k_gain, na_q_gain, na_k_gain, na_rel_bias, w_fourier, ssm_lam_re, ssm_lam_im, ssm_log_dt, ssm_b_re, ssm_b_im, ssm_c_re, ssm_c_im, ssm_d, w_glu, b_glu, g_group, w_out, g_norm2, w_ff1, w_ff3, w_ff2):
    n_batch, n_lat, d = x.shape
    n_ctx = ctx.shape[1]
    n_tok = n_lat + n_ctx
    assert d == D_MODEL and n_ctx == TILE and n_lat % TILE == 0 and n_lat % GRID_W == 0
    assert n_batch % 8 == 0 and n_tok % S5_STEPS == 0

    mod_rows = -(-(n_batch + 1) // 8) * 8
    cc = jnp.concatenate([c, c_ctx[None, :], jnp.zeros((mod_rows - n_batch - 1, d), F32)], axis=0)
    mod = _modulation(cc, w_mod, b_mod.reshape(DEPTH, 1, N_MOD * d)).reshape(DEPTH, mod_rows, 1, N_MOD * d)

    cos_t, sin_t = _rope_tables(n_lat, n_ctx)
    cs_bd = _head_dft_matrix()
    dft = _dft_matrix(n_lat)
    dft_ctx = _dft_matrix(n_ctx)
    def q_perm(a, axis):
        heads = [lax.slice_in_dim(a, h * HEAD_DIM, (h + 1) * HEAD_DIM, axis=axis) for h in _Q_ORDER]
        return jnp.concatenate(heads + [lax.slice_in_dim(a, GROUP_W, a.shape[axis], axis=axis)], axis=axis)

    scale = HEAD_DIM ** -0.5 * LOG2E
    tile2 = lambda g, s=1.0: jnp.tile(g.astype(F32) * s, 2)[None, :]
    row = lambda v: v.astype(F32)[None, :]

    xs = (x, ctx)
    for l in range(DEPTH):
        need_ctx = l < DEPTH - 1
        qkv, u = _inproj(xs, mod[l], row(g_norm1[l]), q_perm(w_in[l], 1).astype(BF16), cos_t, sin_t,
                         tile2(att_q_gain[l], scale), tile2(att_k_gain[l]),
                         tile2(na_q_gain[l], scale), tile2(na_k_gain[l]),
                         n_batch=n_batch, n_tok=n_tok)
        o_a = _gqa(_logits_bounded(att_q_gain[l], att_k_gain[l]), qkv, need_ctx=need_ctx, n_ctx=n_ctx)
        o_d = _na(_logits_bounded(na_q_gain[l], na_k_gain[l], jnp.max(jnp.abs(na_rel_bias[l]))), qkv,
                  _na_bias_table(na_rel_bias[l], n_lat // GRID_W), need_ctx=need_ctx, n_ctx=n_ctx)
        o_f = _fourier(qkv, cs_bd, dft, dft_ctx, w_fourier[l].astype(BF16), need_ctx=need_ctx, n_ctx=n_ctx)
        a_tab, b_dense, c_dense = _s5_tables(ssm_lam_re[l], ssm_lam_im[l], ssm_log_dt[l], ssm_b_re[l],
                                             ssm_b_im[l], ssm_c_re[l], ssm_c_im[l])
        y_fwd = _s5(u, a_tab, b_dense, c_dense, n_ctx=n_ctx, reverse=False)
        y_bwd = _s5(u, a_tab, b_dense, c_dense, n_ctx=n_ctx, reverse=True)
        x2 = _mixout_ffn(xs, mod[l], o_a, o_d, o_f, u, y_fwd, y_bwd, row(ssm_d[l]), w_glu[l].astype(BF16),
                         row(b_glu[l]), row(q_perm(g_group[l], 0)), q_perm(w_out[l], 0).astype(BF16),
                         row(g_norm2[l]), w_ff1[l].astype(BF16), w_ff3[l].astype(BF16), w_ff2[l].astype(BF16),
                         need_ctx=need_ctx, n_ctx=n_ctx)
        xs = (x2,)
    return xs[0]
```

```python
import functools
import math

import numpy as np
import jax
import jax.numpy as jnp
from jax import lax
from jax.experimental import pallas as pl
from jax.experimental.pallas import tpu as pltpu

F32 = jnp.float32
BF16 = jnp.bfloat16

D_MODEL = 1024
DEPTH = 2
GRID_W = 64
HEAD_DIM = 64
N_GROUPS = 4
GROUP_W = D_MODEL // N_GROUPS
WIN_H = 8
WIN_W = 16
NA_HEADS = GROUP_W // HEAD_DIM
FNO_HEAD_W = 64
SSM_GROUP = 16
SSM_GROUPS = GROUP_W // SSM_GROUP
SSM_STATE = 64
SSM_W = SSM_GROUPS * SSM_STATE
ROPE_THETA = 10000.0
D_FF = ((8 * D_MODEL // 3 + 255) // 256) * 256
N_MOD = 6
EPS = 1e-6
IN_W = 7 * GROUP_W
QKV_W = 6 * GROUP_W

LANES = 128
TILE = 256
BAND_ROWS = TILE // GRID_W
S5_STEPS = 128
S5_SUB = 32
FF_CHUNK = D_FF
ATT_BATCH = 2
ROPE_BLOCK = HEAD_DIM // 4
VMEM_LIMIT_BYTES = 56 * 2**20
NEG_BIG = -1e30
LOG2E = 1.0 / math.log(2.0)
EXP2_SAFE = 60.0

_Q_ORDER = (0, 2, 1, 3)


def _params(sem):
    return pltpu.CompilerParams(dimension_semantics=sem, vmem_limit_bytes=VMEM_LIMIT_BYTES)


def _sds(shape, dtype):
    return jax.ShapeDtypeStruct(shape, dtype)


def _mod_kernel(c_ref, w_ref, b_ref, o_ref):
    c = c_ref[...]
    s = c * jax.nn.sigmoid(c)
    w = w_ref[0]
    s_hi, w_hi = s.astype(BF16), w.astype(BF16)
    s_lo = (s - s_hi.astype(F32)).astype(BF16)
    w_lo = (w - w_hi.astype(F32)).astype(BF16)
    dot = functools.partial(jnp.dot, preferred_element_type=F32)
    o_ref[0] = dot(s_hi, w_hi) + (dot(s_lo, w_hi) + dot(s_hi, w_lo)) + b_ref[0]


def _modulation(cc, w_mod, b_mod):
    rows, d = cc.shape
    n_blk = 4
    bw = N_MOD * d // n_blk
    return pl.pallas_call(
        _mod_kernel,
        grid=(DEPTH, n_blk),
        in_specs=[
            pl.BlockSpec((rows, d), lambda l, j: (0, 0)),
            pl.BlockSpec((1, d, bw), lambda l, j: (l, 0, j)),
            pl.BlockSpec((1, 1, bw), lambda l, j: (l, 0, j)),
        ],
        out_specs=pl.BlockSpec((1, rows, bw), lambda l, j: (l, 0, j)),
        out_shape=_sds((DEPTH, rows, N_MOD * d), F32),
        compiler_params=_params(("arbitrary", "arbitrary")),
        name="modulation",
    )(cc, w_mod, b_mod)


def _rms_rows(x):
    return x * lax.rsqrt(jnp.mean(x * x, axis=-1, keepdims=True) + EPS)


def _mod_slices(m, first):
    return [m[:, (first + k) * D_MODEL:(first + k + 1) * D_MODEL] for k in range(3)]


def _head_norm(t, gain):
    lo = lax.broadcasted_iota(jnp.int32, t.shape, 1) < HEAD_DIM
    sq = t * t
    s_lo = jnp.sum(jnp.where(lo, sq, 0.0), axis=-1, keepdims=True)
    s_hi = jnp.sum(jnp.where(lo, 0.0, sq), axis=-1, keepdims=True)
    ms = jnp.where(lo, s_lo, s_hi) * (1.0 / HEAD_DIM)
    return t * lax.rsqrt(ms + EPS) * gain


def _rope(t, cos, sin):
    first = (lax.broadcasted_iota(jnp.int32, t.shape, 1) & ROPE_BLOCK) == 0
    partner = jnp.where(first, pltpu.roll(t, LANES - ROPE_BLOCK, axis=1), pltpu.roll(t, ROPE_BLOCK, axis=1))
    return t * cos + partner * sin


def _softmax_pv(s, v, bounded):
    p = jnp.exp2(s if bounded else s - jnp.max(s, axis=-1, keepdims=True))
    l = jnp.sum(p, axis=-1, keepdims=True)
    return jnp.dot(p.astype(BF16), v, preferred_element_type=F32) / l


def _when_bounded(flag_ref, body):
    bounded = flag_ref[0] == 1
    pl.when(bounded)(lambda: body(True))
    pl.when(jnp.logical_not(bounded))(lambda: body(False))


def _logits_bounded(gq, gk, extra=0.0):
    bound = math.sqrt(HEAD_DIM) * jnp.max(jnp.abs(gq)) * jnp.max(jnp.abs(gk)) + extra
    return (bound * LOG2E <= EXP2_SAFE).astype(jnp.int32).reshape(1)


def _split_heads(q):
    lo = lax.broadcasted_iota(jnp.int32, q.shape, 1) < HEAD_DIM
    zero = jnp.zeros_like(q)
    return jnp.concatenate([jnp.where(lo, q, zero), jnp.where(lo, zero, q)], axis=0)


def _merge_heads(o, rows):
    lo = lax.broadcasted_iota(jnp.int32, (rows, LANES), 1) < HEAD_DIM
    return jnp.where(lo, o[:rows], o[rows:])


def _nt_dot(a, b):
    return lax.dot_general(a, b, (((1,), (1,)), ((), ())), preferred_element_type=F32)


def _inproj_kernel(*refs, split_input):
    if split_input:
        x_ref, ctx_ref, *refs = refs
    else:
        x_ref, *refs = refs
    (mod_ref, g1_ref, w_ref, cos_ref, sin_ref, gq_ref, gk_ref, nq_ref, nk_ref, qkv_ref, u_ref) = refs
    x = x_ref[0]
    if split_input:
        x = jnp.where(pl.program_id(1) == 0, ctx_ref[0], x)
    sh1, sc1, _ = _mod_slices(mod_ref[0], 0)
    h = (_rms_rows(x) * g1_ref[...]) * (1.0 + sc1) + sh1
    p = jnp.dot(h.astype(BF16), w_ref[...], preferred_element_type=F32)
    cos = cos_ref[...]
    sin = sin_ref[...]

    def tile(j):
        return p[:, j * LANES:(j + 1) * LANES]

    out = [
        _rope(_head_norm(tile(0), gq_ref[...]), cos, sin),
        _rope(_head_norm(tile(1), gq_ref[...]), cos, sin),
        _rope(_head_norm(tile(2), gk_ref[...]), cos, sin),
        tile(3),
        _head_norm(tile(4), nq_ref[...]),
        _head_norm(tile(5), nq_ref[...]),
        _head_norm(tile(6), nk_ref[...]),
        _head_norm(tile(7), nk_ref[...]),
        tile(8), tile(9), tile(10), tile(11),
    ]
    for j, t in enumerate(out):
        qkv_ref[0, :, j * LANES:(j + 1) * LANES] = t.astype(BF16)
    u_ref[0] = p[:, QKV_W:]


def _inproj(xs, mod, g1, w_in, cos_t, sin_t, gq, gk, nq, nk, *, n_batch, n_tok):
    split_input = len(xs) == 2
    n_tiles = n_tok // TILE
    tok = (1, TILE, D_MODEL)
    if split_input:
        x_specs = [pl.BlockSpec(tok, lambda b, i: (b, jnp.maximum(i - 1, 0), 0)),
                   pl.BlockSpec(tok, lambda b, i: (b, 0, 0))]
    else:
        x_specs = [pl.BlockSpec(tok, lambda b, i: (b, i, 0))]
    vec = lambda w: pl.BlockSpec((1, w), lambda b, i: (0, 0))
    return pl.pallas_call(
        functools.partial(_inproj_kernel, split_input=split_input),
        grid=(n_batch, n_tiles),
        in_specs=x_specs + [
            pl.BlockSpec((1, 1, N_MOD * D_MODEL), lambda b, i: (jnp.where(i == 0, n_batch, b), 0, 0)),
            vec(D_MODEL),
            pl.BlockSpec((D_MODEL, IN_W), lambda b, i: (0, 0)),
            pl.BlockSpec((TILE, LANES), lambda b, i: (i, 0)),
            pl.BlockSpec((TILE, LANES), lambda b, i: (i, 0)),
            vec(LANES), vec(LANES), vec(LANES), vec(LANES),
        ],
        out_specs=[pl.BlockSpec((1, TILE, QKV_W), lambda b, i: (b, i, 0)),
                   pl.BlockSpec((1, TILE, GROUP_W), lambda b, i: (b, i, 0))],
        out_shape=[_sds((n_batch, n_tok, QKV_W), BF16), _sds((n_batch, n_tok, GROUP_W), F32)],
        compiler_params=_params(("arbitrary", "arbitrary")),
        name="inproj",
    )(*xs, mod, g1, w_in, cos_t, sin_t, gq, gk, nq, nk)


def _gqa_kernel(flag_ref, qa_ref, qb_ref, k_ref, v_ref, o_ref, *, n_ctx, tile_off):
    def attend(n_keys, bounded):
        for b in range(ATT_BATCH):
            k = k_ref[b, :n_keys, :]
            v = v_ref[b, :n_keys, :]
            outs = []
            for q_ref in (qa_ref, qb_ref):
                q2 = _split_heads(q_ref[b])
                outs.append(_merge_heads(_softmax_pv(_nt_dot(q2, k), v, bounded), TILE))
            o_ref[b] = jnp.concatenate(outs, axis=-1).astype(o_ref.dtype)

    if tile_off == 0:
        is_ctx = pl.program_id(1) == 0
        pl.when(is_ctx)(lambda: _when_bounded(flag_ref, functools.partial(attend, n_ctx)))
        pl.when(jnp.logical_not(is_ctx))(lambda: _when_bounded(flag_ref, functools.partial(attend, k_ref.shape[1])))
    else:
        _when_bounded(flag_ref, functools.partial(attend, k_ref.shape[1]))


_SMEM_SPEC = pl.BlockSpec(memory_space=pltpu.SMEM)


def _gqa(bounded, qkv, *, need_ctx, n_ctx):
    n_batch, n_tok, _ = qkv.shape
    off = 0 if need_ctx else n_ctx // TILE
    assert n_batch % ATT_BATCH == 0
    q_spec = lambda col: pl.BlockSpec((ATT_BATCH, TILE, LANES), lambda b, i: (b, i + off, col))
    kv_spec = lambda col: pl.BlockSpec((ATT_BATCH, n_tok, LANES), lambda b, i: (b, 0, col))
    return pl.pallas_call(
        functools.partial(_gqa_kernel, n_ctx=n_ctx, tile_off=off),
        grid=(n_batch // ATT_BATCH, n_tok // TILE - off),
        in_specs=[_SMEM_SPEC, q_spec(0), q_spec(1), kv_spec(2), kv_spec(3)],
        out_specs=pl.BlockSpec((ATT_BATCH, TILE, GROUP_W), lambda b, i: (b, i, 0)),
        out_shape=_sds((n_batch, n_tok - off * TILE, GROUP_W), BF16),
        compiler_params=_params(("arbitrary", "arbitrary")),
        name="gqa",
    )(bounded, qkv, qkv, qkv, qkv)


def _na_kernel(flag_ref, q_ref, kp_ref, kc_ref, kn_ref, kx_ref, vp_ref, vc_ref, vn_ref, vx_ref, tbl_ref, o_ref, *, tile_off):
    def attend(local, bounded):
        for b in range(ATT_BATCH):
            outs = []
            for pair in range(2):
                cols = slice(pair * LANES, (pair + 1) * LANES)
                q2 = _split_heads(q_ref[b, :, cols])
                if local:
                    k = jnp.concatenate([r[b, :, cols] for r in (kp_ref, kc_ref, kn_ref, kx_ref)], axis=0)
                    v = jnp.concatenate([r[b, :, cols] for r in (vp_ref, vc_ref, vn_ref, vx_ref)], axis=0)
                    bias = jnp.concatenate([tbl_ref[0, 2 * pair], tbl_ref[0, 2 * pair + 1]], axis=0)
                    s = _nt_dot(q2, k)
                    s = jnp.concatenate([s[:, :3 * TILE] + bias, s[:, 3 * TILE:]], axis=-1)
                else:
                    k = kx_ref[b, :, cols]
                    v = vx_ref[b, :, cols]
                    s = _nt_dot(q2, k)
                outs.append(_merge_heads(_softmax_pv(s, v, bounded), TILE))
            o_ref[b] = jnp.concatenate(outs, axis=-1).astype(o_ref.dtype)

    if tile_off == 0:
        is_ctx = pl.program_id(1) == 0
        pl.when(is_ctx)(lambda: _when_bounded(flag_ref, functools.partial(attend, False)))
        pl.when(jnp.logical_not(is_ctx))(lambda: _when_bounded(flag_ref, functools.partial(attend, True)))
    else:
        _when_bounded(flag_ref, functools.partial(attend, True))


def _na(bounded, qkv, table, *, need_ctx, n_ctx):
    n_batch, n_tok, _ = qkv.shape
    assert n_ctx == TILE
    n_tiles = n_tok // TILE
    off = 0 if need_ctx else 1
    assert n_batch % ATT_BATCH == 0
    blk = (ATT_BATCH, TILE, GROUP_W)
    cur = lambda col: pl.BlockSpec(blk, lambda b, i: (b, i + off, col))
    prev = lambda col: pl.BlockSpec(blk, lambda b, i: (b, jnp.maximum(i + off - 1, 1), col))
    nxt = lambda col: pl.BlockSpec(blk, lambda b, i: (b, jnp.minimum(i + off + 1, n_tiles - 1), col))
    ctx = lambda col: pl.BlockSpec(blk, lambda b, i: (b, 0, col))

    def cls(b, i):
        t = i + off
        return (jnp.where(t == n_tiles - 1, 2, jnp.where(t <= 1, 0, 1)), 0, 0, 0)

    return pl.pallas_call(
        functools.partial(_na_kernel, tile_off=off),
        grid=(n_batch // ATT_BATCH, n_tiles - off),
        in_specs=[_SMEM_SPEC, cur(2), prev(3), cur(3), nxt(3), ctx(3), prev(4), cur(4), nxt(4), ctx(4),
                  pl.BlockSpec((1, NA_HEADS, TILE, 3 * TILE), cls)],
        out_specs=pl.BlockSpec(blk, lambda b, i: (b, i, 0)),
        out_shape=_sds((n_batch, n_tok - off * TILE, GROUP_W), BF16),
        compiler_params=_params(("arbitrary", "arbitrary")),
        name="na",
    )(bounded, *([qkv] * 9), table)


def _na_bias_table(rel_bias, rows):
    n_bands = rows // BAND_ROWS
    assert rows >= WIN_H and rows % BAND_ROWS == 0
    n_dr, n_dc = 2 * WIN_H - 1, 2 * WIN_W - 1
    w = GRID_W
    shift = w - WIN_W
    vp = jnp.pad(rel_bias.astype(F32) * LOG2E, ((0, 0), (0, 0), (shift, 2 * w - shift - n_dc)))
    skew = jnp.broadcast_to(vp[:, :, None, :], (NA_HEADS, n_dr, w, 2 * w)).reshape(NA_HEADS, n_dr, 2 * w * w)
    col = skew[:, :, :w * (2 * w - 1)].reshape(NA_HEADS, n_dr, w, 2 * w - 1)[:, :, :, w - 1:]
    qc = np.arange(w)
    cs = np.clip(qc - WIN_W // 2, 0, w - WIN_W)
    col_ok = (qc[None, :] >= cs[:, None]) & (qc[None, :] < cs[:, None] + WIN_W)
    col = jnp.where(col_ok[None, None], col, NEG_BIG)
    masked = jnp.full((NA_HEADS, w, w), NEG_BIG, F32)
    classes = []
    for band in (0, min(1, n_bands - 1), n_bands - 1):
        q_rows = []
        for i in range(BAND_ROWS):
            r = BAND_ROWS * band + i
            rs = min(max(r - WIN_H // 2, 0), rows - WIN_H)
            blocks = []
            for kk in range(3 * BAND_ROWS):
                kr = BAND_ROWS * (band - 1) + kk
                ok = 0 <= kr < rows and rs <= kr < rs + WIN_H
                blocks.append(col[:, kr - r + WIN_H - 1] if ok else masked)
            q_rows.append(jnp.concatenate(blocks, axis=-1))
        classes.append(jnp.concatenate(q_rows, axis=-2))
    return jnp.stack(classes)


def _fourier_kernel(x_ref, cs_ref, dft_ref, dftc_ref, w_ref, o_ref, x12_ref, *, n_ctx, n_lat, tile_off):
    t = pl.program_id(1) + tile_off

    def head_dft(x):
        x12 = jnp.dot(x, cs_ref[...], preferred_element_type=F32).astype(BF16)
        return jnp.concatenate([x12[:, :GROUP_W], x12[:, GROUP_W:]], axis=0)

    def finish(y, n):
        y = y * (1.0 / math.sqrt(n * FNO_HEAD_W))
        o_ref[0] = jnp.dot(y.astype(BF16), w_ref[...], preferred_element_type=F32).astype(o_ref.dtype)

    if tile_off == 0:
        @pl.when(t == 0)
        def _():
            x12 = head_dft(x_ref[0, :n_ctx, :])
            finish(jnp.dot(dftc_ref[...], x12, preferred_element_type=F32), n_ctx)

    @pl.when(t == n_ctx // TILE)
    def _():
        x12_ref[...] = head_dft(x_ref[0, n_ctx:, :])

    @pl.when(t >= n_ctx // TILE)
    def _():
        row0 = pl.multiple_of((t - n_ctx // TILE) * TILE, TILE)
        finish(jnp.dot(dft_ref[pl.ds(row0, TILE), :], x12_ref[...], preferred_element_type=F32), n_lat)


def _fourier(qkv, cs_bd, dft, dft_ctx, w_f, *, need_ctx, n_ctx):
    n_batch, n_tok, _ = qkv.shape
    n_lat = n_tok - n_ctx
    off = 0 if need_ctx else n_ctx // TILE
    const = lambda shape: pl.BlockSpec(shape, lambda b, i: (0,) * len(shape))
    return pl.pallas_call(
        functools.partial(_fourier_kernel, n_ctx=n_ctx, n_lat=n_lat, tile_off=off),
        grid=(n_batch, n_tok // TILE - off),
        in_specs=[pl.BlockSpec((1, n_tok, GROUP_W), lambda b, i: (b, 0, 5)),
                  const(cs_bd.shape), const(dft.shape), const(dft_ctx.shape), const(w_f.shape)],
        out_specs=pl.BlockSpec((1, TILE, GROUP_W), lambda b, i: (b, i, 0)),
        out_shape=_sds((n_batch, n_tok - off * TILE, GROUP_W), BF16),
        scratch_shapes=[pltpu.VMEM((2 * n_lat, GROUP_W), BF16)],
        compiler_params=_params(("arbitrary", "arbitrary")),
        name="fourier",
    )(qkv, cs_bd, dft, dft_ctx, w_f)


def _dft_matrix(n):
    f = 32
    assert n % f == 0
    k = lax.broadcasted_iota(jnp.int32, (n, 1), 0)
    m1 = lax.broadcasted_iota(jnp.int32, (1, n // f), 1)
    m0 = lax.broadcasted_iota(jnp.int32, (1, f), 1)
    ang_a = ((k * (m1 * f)) % n).astype(F32) * (2.0 * math.pi / n)
    ang_b = ((k * m0) % n).astype(F32) * (2.0 * math.pi / n)
    ca, sa = jnp.cos(ang_a)[:, :, None], jnp.sin(ang_a)[:, :, None]
    cb, sb = jnp.cos(ang_b)[:, None, :], jnp.sin(ang_b)[:, None, :]
    cos = (ca * cb - sa * sb).reshape(n, n)
    sin = (sa * cb + ca * sb).reshape(n, n)
    return jnp.concatenate([cos, -sin], axis=1).astype(BF16)


def _head_dft_matrix():
    a = np.arange(GROUP_W)
    same = (a[:, None] // FNO_HEAD_W) == (a[None, :] // FNO_HEAD_W)
    ang = 2.0 * np.pi * (((a[:, None] % FNO_HEAD_W) * (a[None, :] % FNO_HEAD_W)) % FNO_HEAD_W) / FNO_HEAD_W
    m = np.concatenate([np.where(same, np.cos(ang), 0.0), np.where(same, np.sin(ang), 0.0)], axis=1)
    return jnp.asarray(m, F32).astype(BF16)


def _s5_kernel(u_ref, a_ref, bd_ref, cd_ref, y_ref, bu_ref, hb_ref, h_ref, *, n_batch, reverse):
    @pl.when(pl.program_id(0) == 0)
    def _():
        h_ref[...] = jnp.zeros_like(h_ref)

    sub_rows = S5_SUB * n_batch
    chunk = 4 * LANES
    def project(q):
        u = pltpu.einshape("btc->tbc", u_ref[:, q * S5_SUB:(q + 1) * S5_SUB, :])
        u = u.reshape(sub_rows, GROUP_W).astype(BF16)
        bu_ref[q * sub_rows:(q + 1) * sub_rows, :] = jnp.dot(u, bd_ref[0], preferred_element_type=F32)

    def scan(q):
        steps = range(S5_SUB - 1, -1, -1) if reverse else range(S5_SUB)
        for sc in range(SSM_W // chunk):
            re = slice(sc * chunk, (sc + 1) * chunk)
            im = slice(SSM_W + sc * chunk, SSM_W + (sc + 1) * chunk)
            a_re = jnp.broadcast_to(a_ref[0, :, re], (n_batch, chunk))
            a_im = jnp.broadcast_to(a_ref[0, :, im], (n_batch, chunk))
            h_re, h_im = h_ref[:, re], h_ref[:, im]
            for t in steps:
                r = slice(q * sub_rows + t * n_batch, q * sub_rows + (t + 1) * n_batch)
                h_re, h_im = (a_re * h_re - a_im * h_im + bu_ref[r, re],
                              a_re * h_im + a_im * h_re + bu_ref[r, im])
                hb_ref[r, re] = h_re.astype(BF16)
                hb_ref[r, im] = h_im.astype(BF16)
            h_ref[:, re] = h_re
            h_ref[:, im] = h_im

    def readout(q):
        y = jnp.dot(hb_ref[q * sub_rows:(q + 1) * sub_rows, :], cd_ref[0], preferred_element_type=F32)
        y = pltpu.einshape("tbc->btc", y.reshape(S5_SUB, n_batch, GROUP_W))
        y_ref[:, q * S5_SUB:(q + 1) * S5_SUB, :] = y.astype(y_ref.dtype)

    order = list(range(S5_STEPS // S5_SUB))
    if reverse:
        order.reverse()
    project(order[0])
    for j, q in enumerate(order):
        if j + 1 < len(order):
            project(order[j + 1])
        scan(q)
        if j > 0:
            readout(order[j - 1])
    readout(order[-1])


def _s5(u, a_tab, b_dense, c_dense, *, n_ctx, reverse):
    n_batch, n_tok, _ = u.shape
    n_blk = n_tok // S5_STEPS
    n_cblk = n_ctx // S5_STEPS
    d = int(reverse)
    if reverse:
        blk = lambda s: jnp.where(s < n_cblk, n_cblk - 1 - s, n_blk - 1 - (s - n_cblk))
    else:
        blk = lambda s: s
    return pl.pallas_call(
        functools.partial(_s5_kernel, n_batch=n_batch, reverse=reverse),
        grid=(n_blk,),
        in_specs=[pl.BlockSpec((n_batch, S5_STEPS, GROUP_W), lambda s: (0, blk(s), 0)),
                  pl.BlockSpec((1, 1, 2 * SSM_W), lambda s: (d, 0, 0)),
                  pl.BlockSpec((1, GROUP_W, 2 * SSM_W), lambda s: (d, 0, 0)),
                  pl.BlockSpec((1, 2 * SSM_W, GROUP_W), lambda s: (d, 0, 0))],
        out_specs=pl.BlockSpec((n_batch, S5_STEPS, GROUP_W), lambda s: (0, blk(s), 0)),
        out_shape=_sds((n_batch, n_tok, GROUP_W), BF16),
        scratch_shapes=[pltpu.VMEM((S5_STEPS * n_batch, 2 * SSM_W), F32),
                        pltpu.VMEM((S5_STEPS * n_batch, 2 * SSM_W), BF16),
                        pltpu.VMEM((n_batch, 2 * SSM_W), F32)],
        compiler_params=_params(("arbitrary",)),
        name="s5",
    )(u, a_tab, b_dense, c_dense)


def _s5_tables(lam_re, lam_im, log_dt, b_re, b_im, c_re, c_im):
    lam = lax.complex(lam_re.astype(F32), lam_im.astype(F32))
    dt = jnp.exp(log_dt.astype(F32))[..., None]
    a_bar = jnp.exp(lam * dt)
    b_bar = ((a_bar - 1.0) / lam)[..., None] * lax.complex(b_re.astype(F32), b_im.astype(F32))
    eye = jnp.eye(SSM_GROUPS, dtype=F32)

    def embed_b(m):
        m = jnp.transpose(m, (0, 1, 3, 2))[:, :, :, None, :] * eye[None, :, None, :, None]
        return m.reshape(2, GROUP_W, SSM_W)

    def embed_c(m):
        m = jnp.transpose(m, (0, 1, 3, 2))[:, :, :, None, :] * eye[None, :, None, :, None]
        return m.reshape(2, SSM_W, GROUP_W)

    a_tab = jnp.concatenate([jnp.real(a_bar).reshape(2, 1, SSM_W), jnp.imag(a_bar).reshape(2, 1, SSM_W)], axis=-1)
    b_dense = jnp.concatenate([embed_b(jnp.real(b_bar)), embed_b(jnp.imag(b_bar))], axis=-1).astype(BF16)
    c_dense = jnp.concatenate([embed_c(c_re.astype(F32)), -embed_c(c_im.astype(F32))], axis=1).astype(BF16)
    return a_tab, b_dense, c_dense


def _gelu_tanh(x):
    return x * (0.5 * (1.0 + jnp.tanh(math.sqrt(2.0 / math.pi) * (x + 0.044715 * (x * x * x)))))


def _mixout_kernel(*refs, split_input, tiles_per_batch, tile_off):
    if split_input:
        x_ref, ctx_ref, *refs = refs
    else:
        x_ref, *refs = refs
    (mod_ref, oa_ref, od_ref, of_ref, u_ref, yf_ref, yb_ref, d_ref, wg_ref, bg_ref, gg_ref, wo_ref,
     g2_ref, w1_ref, w3_ref, w2_ref, o_ref, x1_ref, h_ref, ga2_ref) = refs
    s = pl.program_id(0)
    n_items = pl.num_programs(0) - 1

    def feed_forward():
        h = h_ref[...]
        acc = jnp.zeros((TILE, D_MODEL), F32)
        for c in range(D_FF // FF_CHUNK):
            cols = slice(c * FF_CHUNK, (c + 1) * FF_CHUNK)
            a = jnp.dot(h, w1_ref[:, cols], preferred_element_type=F32)
            b = jnp.dot(h, w3_ref[:, cols], preferred_element_type=F32)
            t = (a * jax.nn.sigmoid(a)) * b
            acc = acc + jnp.dot(t.astype(BF16), w2_ref[cols, :], preferred_element_type=F32)
        o_ref[0] = x1_ref[...] + ga2_ref[...] * acc

    def mix():
        x = x_ref[0]
        if split_input:
            is_ctx = lax.rem(jnp.minimum(s, n_items - 1), tiles_per_batch) + tile_off == 0
            x = jnp.where(is_ctx, ctx_ref[0], x)
        _, _, ga1 = _mod_slices(mod_ref[0], 0)
        sh2, sc2, ga2 = _mod_slices(mod_ref[0], 3)
        y = u_ref[0] * d_ref[...] + yf_ref[0].astype(F32) + yb_ref[0].astype(F32)
        g = _gelu_tanh(y)
        z = jnp.dot(g.astype(BF16), wg_ref[...], preferred_element_type=F32) + bg_ref[...]
        o_s = g * jax.nn.sigmoid(z)
        parts = [oa_ref[0].astype(F32), od_ref[0].astype(F32), of_ref[0].astype(F32), o_s]
        normed = [(_rms_rows(p) * gg_ref[:, k * GROUP_W:(k + 1) * GROUP_W]).astype(BF16)
                  for k, p in enumerate(parts)]
        r = jnp.dot(jnp.concatenate(normed, axis=-1), wo_ref[...], preferred_element_type=F32)
        x = x + ga1 * r
        x1_ref[...] = x
        h_ref[...] = ((_rms_rows(x) * g2_ref[...]) * (1.0 + sc2) + sh2).astype(BF16)
        ga2_ref[...] = ga2

    pl.when(s == 0)(mix)

    @pl.when(jnp.logical_and(s > 0, s < n_items))
    def _():
        feed_forward()
        mix()

    pl.when(s == n_items)(feed_forward)


def _mixout_ffn(xs, mod, o_a, o_d, o_f, u, y_fwd, y_bwd, d_skip, w_glu, b_glu, g_group, w_out, g2, w1, w3, w2, *,
                need_ctx, n_ctx):
    split_input = len(xs) == 2
    n_batch, n_tok, _ = u.shape
    off = 0 if need_ctx else n_ctx // TILE
    n_out = n_tok - off * TILE
    nt = n_out // TILE
    n_items = n_batch * nt
    tok = (1, TILE, D_MODEL)

    def item(fn):
        def index_map(s):
            a = jnp.minimum(s, n_items - 1)
            return fn(a // nt, a % nt)
        return index_map

    def out_map(s):
        a = jnp.maximum(s - 1, 0)
        return (a // nt, a % nt, 0)

    if split_input:
        assert need_ctx
        x_specs = [pl.BlockSpec(tok, item(lambda b, i: (b, jnp.maximum(i - 1, 0), 0))),
                   pl.BlockSpec(tok, item(lambda b, i: (b, 0, 0)))]
    else:
        x_specs = [pl.BlockSpec(tok, item(lambda b, i: (b, i + off, 0)))]
    grp = pl.BlockSpec((1, TILE, GROUP_W), item(lambda b, i: (b, i + off, 0)))
    mix = pl.BlockSpec((1, TILE, GROUP_W), item(lambda b, i: (b, i, 0)))
    const = lambda shape: pl.BlockSpec(shape, lambda s: (0,) * len(shape), pipeline_mode=pl.Buffered(1))
    return pl.pallas_call(
        functools.partial(_mixout_kernel, split_input=split_input, tiles_per_batch=nt, tile_off=off),
        grid=(n_items + 1,),
        in_specs=x_specs + [
            pl.BlockSpec((1, 1, N_MOD * D_MODEL), item(lambda b, i: (jnp.where(i + off == 0, n_batch, b), 0, 0))),
            mix, mix, mix, grp, grp, grp,
            const((1, GROUP_W)), const((GROUP_W, GROUP_W)), const((1, GROUP_W)),
            const((1, D_MODEL)), const((D_MODEL, D_MODEL)),
            const((1, D_MODEL)), const(w1.shape), const(w3.shape), const(w2.shape),
        ],
        out_specs=pl.BlockSpec(tok, out_map),
        out_shape=_sds((n_batch, n_out, D_MODEL), F32),
        scratch_shapes=[pltpu.VMEM((TILE, D_MODEL), F32), pltpu.VMEM((TILE, D_MODEL), BF16),
                        pltpu.VMEM((1, D_MODEL), F32)],
        compiler_params=_params(("arbitrary",)),
        name="mixffn",
    )(*xs, mod, o_a, o_d, o_f, u, y_fwd, y_bwd, d_skip, w_glu, b_glu, g_group, w_out, g2, w1, w3, w2)


def _rope_tables(n_lat, n_ctx):
    t = jnp.arange(n_lat, dtype=jnp.int32)
    rows = (t // GRID_W).astype(F32)
    cols = (t % GRID_W).astype(F32)
    axis_dim = HEAD_DIM // 2
    inv_freq = ROPE_THETA ** (-jnp.arange(0, axis_dim, 2, dtype=F32) / axis_dim)
    ang_r = rows[:, None] * inv_freq[None, :]
    ang_c = cols[:, None] * inv_freq[None, :]
    cos = jnp.concatenate([jnp.cos(ang_r)] * 2 + [jnp.cos(ang_c)] * 2, axis=-1)
    sin = jnp.concatenate([-jnp.sin(ang_r), jnp.sin(ang_r), -jnp.sin(ang_c), jnp.sin(ang_c)], axis=-1)
    cos = jnp.concatenate([jnp.ones((n_ctx, HEAD_DIM), F32), cos], axis=0)
    sin = jnp.concatenate([jnp.zeros((n_ctx, HEAD_DIM), F32), sin], axis=0)
    return jnp.tile(cos, (1, 2)), jnp.tile(sin, (1, 2))


def kernel(x, c, ctx, c_ctx, w_mod, b_mod, g_norm1, w_in, att_q_gain, att_k_gain, na_q_gain, na_k_gain, na_rel_bias, w_fourier, ssm_lam_re, ssm_lam_im, ssm_log_dt, ssm_b_re, ssm_b_im, ssm_c_re, ssm_c_im, ssm_d, w_glu, b_glu, g_group, w_out, g_norm2, w_ff1, w_ff3, w_ff2):
    n_batch, n_lat, d = x.shape
    n_ctx = ctx.shape[1]
    n_tok = n_lat + n_ctx
    assert d == D_MODEL and n_ctx == TILE and n_lat % TILE == 0 and n_lat % GRID_W == 0
    assert n_batch % 8 == 0 and n_tok % S5_STEPS == 0

    mod_rows = -(-(n_batch + 1) // 8) * 8
    cc = jnp.concatenate([c, c_ctx[None, :], jnp.zeros((mod_rows - n_batch - 1, d), F32)], axis=0)
    mod = _modulation(cc, w_mod, b_mod.reshape(DEPTH, 1, N_MOD * d)).reshape(DEPTH, mod_rows, 1, N_MOD * d)

    cos_t, sin_t = _rope_tables(n_lat, n_ctx)
    cs_bd = _head_dft_matrix()
    dft = _dft_matrix(n_lat)
    dft_ctx = _dft_matrix(n_ctx)
    def q_perm(a, axis):
        heads = [lax.slice_in_dim(a, h * HEAD_DIM, (h + 1) * HEAD_DIM, axis=axis) for h in _Q_ORDER]
        return jnp.concatenate(heads + [lax.slice_in_dim(a, GROUP_W, a.shape[axis], axis=axis)], axis=axis)

    scale = HEAD_DIM ** -0.5 * LOG2E
    tile2 = lambda g, s=1.0: jnp.tile(g.astype(F32) * s, 2)[None, :]
    row = lambda v: v.astype(F32)[None, :]

    xs = (x, ctx)
    for l in range(DEPTH):
        need_ctx = l < DEPTH - 1
        qkv, u = _inproj(xs, mod[l], row(g_norm1[l]), q_perm(w_in[l], 1).astype(BF16), cos_t, sin_t,
                         tile2(att_q_gain[l], scale), tile2(att_k_gain[l]),
                         tile2(na_q_gain[l], scale), tile2(na_k_gain[l]),
                         n_batch=n_batch, n_tok=n_tok)
        o_a = _gqa(_logits_bounded(att_q_gain[l], att_k_gain[l]), qkv, need_ctx=need_ctx, n_ctx=n_ctx)
        o_d = _na(_logits_bounded(na_q_gain[l], na_k_gain[l], jnp.max(jnp.abs(na_rel_bias[l]))), qkv,
                  _na_bias_table(na_rel_bias[l], n_lat // GRID_W), need_ctx=need_ctx, n_ctx=n_ctx)
        o_f = _fourier(qkv, cs_bd, dft, dft_ctx, w_fourier[l].astype(BF16), need_ctx=need_ctx, n_ctx=n_ctx)
        a_tab, b_dense, c_dense = _s5_tables(ssm_lam_re[l], ssm_lam_im[l], ssm_log_dt[l], ssm_b_re[l],
                                             ssm_b_im[l], ssm_c_re[l], ssm_c_im[l])
        y_fwd = _s5(u, a_tab, b_dense, c_dense, n_ctx=n_ctx, reverse=False)
        y_bwd = _s5(u, a_tab, b_dense, c_dense, n_ctx=n_ctx, reverse=True)
        x2 = _mixout_ffn(xs, mod[l], o_a, o_d, o_f, u, y_fwd, y_bwd, row(ssm_d[l]), w_glu[l].astype(BF16),
                         row(b_glu[l]), row(q_perm(g_group[l], 0)), q_perm(w_out[l], 0).astype(BF16),
                         row(g_norm2[l]), w_ff1[l].astype(BF16), w_ff3[l].astype(BF16), w_ff2[l].astype(BF16),
                         need_ctx=need_ctx, n_ctx=n_ctx)
        xs = (x2,)
    return xs[0]
```

```python
import functools
import math

import numpy as np
import jax
import jax.numpy as jnp
from jax import lax
from jax.experimental import pallas as pl
from jax.experimental.pallas import tpu as pltpu

F32 = jnp.float32
BF16 = jnp.bfloat16

D_MODEL = 1024
DEPTH = 2
GRID_W = 64
HEAD_DIM = 64
N_GROUPS = 4
GROUP_W = D_MODEL // N_GROUPS
WIN_H = 8
WIN_W = 16
NA_HEADS = GROUP_W // HEAD_DIM
FNO_HEAD_W = 64
SSM_GROUP = 16
SSM_GROUPS = GROUP_W // SSM_GROUP
SSM_STATE = 64
SSM_W = SSM_GROUPS * SSM_STATE
ROPE_THETA = 10000.0
D_FF = ((8 * D_MODEL // 3 + 255) // 256) * 256
N_MOD = 6
EPS = 1e-6
IN_W = 7 * GROUP_W
QKV_W = 6 * GROUP_W

LANES = 128
TILE = 256
BAND_ROWS = TILE // GRID_W
S5_STEPS = 128
S5_SUB = 32
FF_CHUNK = D_FF
ATT_BATCH = 4
ROPE_BLOCK = HEAD_DIM // 4
VMEM_LIMIT_BYTES = 56 * 2**20
NEG_BIG = -1e30
LOG2E = 1.0 / math.log(2.0)
EXP2_SAFE = 60.0

_Q_ORDER = (0, 2, 1, 3)


def _params(sem):
    return pltpu.CompilerParams(dimension_semantics=sem, vmem_limit_bytes=VMEM_LIMIT_BYTES)


def _sds(shape, dtype):
    return jax.ShapeDtypeStruct(shape, dtype)


def _mod_kernel(c_ref, w_ref, b_ref, o_ref):
    c = c_ref[...]
    s = c * jax.nn.sigmoid(c)
    w = w_ref[0]
    s_hi, w_hi = s.astype(BF16), w.astype(BF16)
    s_lo = (s - s_hi.astype(F32)).astype(BF16)
    w_lo = (w - w_hi.astype(F32)).astype(BF16)
    dot = functools.partial(jnp.dot, preferred_element_type=F32)
    o_ref[0] = dot(s_hi, w_hi) + (dot(s_lo, w_hi) + dot(s_hi, w_lo)) + b_ref[0]


def _modulation(cc, w_mod, b_mod):
    rows, d = cc.shape
    n_blk = 4
    bw = N_MOD * d // n_blk
    return pl.pallas_call(
        _mod_kernel,
        grid=(DEPTH, n_blk),
        in_specs=[
            pl.BlockSpec((rows, d), lambda l, j: (0, 0)),
            pl.BlockSpec((1, d, bw), lambda l, j: (l, 0, j)),
            pl.BlockSpec((1, 1, bw), lambda l, j: (l, 0, j)),
        ],
        out_specs=pl.BlockSpec((1, rows, bw), lambda l, j: (l, 0, j)),
        out_shape=_sds((DEPTH, rows, N_MOD * d), F32),
        compiler_params=_params(("arbitrary", "arbitrary")),
        name="modulation",
    )(cc, w_mod, b_mod)


def _rms_rows(x):
    return x * lax.rsqrt(jnp.mean(x * x, axis=-1, keepdims=True) + EPS)


def _mod_slices(m, first):
    return [m[:, (first + k) * D_MODEL:(first + k + 1) * D_MODEL] for k in range(3)]


def _head_norm(t, gain):
    lo = lax.broadcasted_iota(jnp.int32, t.shape, 1) < HEAD_DIM
    sq = t * t
    s_lo = jnp.sum(jnp.where(lo, sq, 0.0), axis=-1, keepdims=True)
    s_hi = jnp.sum(jnp.where(lo, 0.0, sq), axis=-1, keepdims=True)
    ms = jnp.where(lo, s_lo, s_hi) * (1.0 / HEAD_DIM)
    return t * lax.rsqrt(ms + EPS) * gain


def _rope(t, cos, sin):
    first = (lax.broadcasted_iota(jnp.int32, t.shape, 1) & ROPE_BLOCK) == 0
    partner = jnp.where(first, pltpu.roll(t, LANES - ROPE_BLOCK, axis=1), pltpu.roll(t, ROPE_BLOCK, axis=1))
    return t * cos + partner * sin


def _softmax_pv(s, v, bounded):
    p = jnp.exp2(s if bounded else s - jnp.max(s, axis=-1, keepdims=True))
    l = jnp.sum(p, axis=-1, keepdims=True)
    return jnp.dot(p.astype(BF16), v, preferred_element_type=F32) / l


def _when_bounded(flag_ref, body):
    bounded = flag_ref[0] == 1
    pl.when(bounded)(lambda: body(True))
    pl.when(jnp.logical_not(bounded))(lambda: body(False))


def _logits_bounded(gq, gk, extra=0.0):
    bound = math.sqrt(HEAD_DIM) * jnp.max(jnp.abs(gq)) * jnp.max(jnp.abs(gk)) + extra
    return (bound * LOG2E <= EXP2_SAFE).astype(jnp.int32).reshape(1)


def _split_heads(q):
    lo = lax.broadcasted_iota(jnp.int32, q.shape, 1) < HEAD_DIM
    zero = jnp.zeros_like(q)
    return jnp.concatenate([jnp.where(lo, q, zero), jnp.where(lo, zero, q)], axis=0)


def _merge_heads(o, rows):
    lo = lax.broadcasted_iota(jnp.int32, (rows, LANES), 1) < HEAD_DIM
    return jnp.where(lo, o[:rows], o[rows:])


def _nt_dot(a, b):
    return lax.dot_general(a, b, (((1,), (1,)), ((), ())), preferred_element_type=F32)


def _inproj_kernel(*refs, split_input):
    if split_input:
        x_ref, ctx_ref, *refs = refs
    else:
        x_ref, *refs = refs
    (mod_ref, modc_ref, g1_ref, w_ref, cos_ref, sin_ref, gq_ref, gk_ref, nq_ref, nk_ref, qkv_ref, u_ref) = refs
    is_ctx = pl.program_id(1) == 0
    cos = cos_ref[...]
    sin = sin_ref[...]
    for b in range(ATT_BATCH):
        x = x_ref[b]
        if split_input:
            x = jnp.where(is_ctx, ctx_ref[b], x)
        sh1, sc1, _ = _mod_slices(jnp.where(is_ctx, modc_ref[0], mod_ref[b]), 0)
        h = (_rms_rows(x) * g1_ref[...]) * (1.0 + sc1) + sh1
        p = jnp.dot(h.astype(BF16), w_ref[...], preferred_element_type=F32)

        def tile(j, p=p):
            return p[:, j * LANES:(j + 1) * LANES]

        out = [
            _rope(_head_norm(tile(0), gq_ref[...]), cos, sin),
            _rope(_head_norm(tile(1), gq_ref[...]), cos, sin),
            _rope(_head_norm(tile(2), gk_ref[...]), cos, sin),
            tile(3),
            _head_norm(tile(4), nq_ref[...]),
            _head_norm(tile(5), nq_ref[...]),
            _head_norm(tile(6), nk_ref[...]),
            _head_norm(tile(7), nk_ref[...]),
            tile(8), tile(9), tile(10), tile(11),
        ]
        for j, t in enumerate(out):
            qkv_ref[b, :, j * LANES:(j + 1) * LANES] = t.astype(BF16)
        u_ref[b] = p[:, QKV_W:]


def _inproj(xs, mod, g1, w_in, cos_t, sin_t, gq, gk, nq, nk, *, n_batch, n_tok):
    split_input = len(xs) == 2
    n_tiles = n_tok // TILE
    assert n_batch % ATT_BATCH == 0
    tok = (ATT_BATCH, TILE, D_MODEL)
    if split_input:
        x_specs = [pl.BlockSpec(tok, lambda b, i: (b, jnp.maximum(i - 1, 0), 0)),
                   pl.BlockSpec(tok, lambda b, i: (b, 0, 0))]
    else:
        x_specs = [pl.BlockSpec(tok, lambda b, i: (b, i, 0))]
    vec = lambda w: pl.BlockSpec((1, w), lambda b, i: (0, 0))
    return pl.pallas_call(
        functools.partial(_inproj_kernel, split_input=split_input),
        grid=(n_batch // ATT_BATCH, n_tiles),
        in_specs=x_specs + [
            pl.BlockSpec((ATT_BATCH, 1, N_MOD * D_MODEL), lambda b, i: (b, 0, 0)),
            pl.BlockSpec((1, 1, N_MOD * D_MODEL), lambda b, i: (n_batch, 0, 0)),
            vec(D_MODEL),
            pl.BlockSpec((D_MODEL, IN_W), lambda b, i: (0, 0)),
            pl.BlockSpec((TILE, LANES), lambda b, i: (i, 0)),
            pl.BlockSpec((TILE, LANES), lambda b, i: (i, 0)),
            vec(LANES), vec(LANES), vec(LANES), vec(LANES),
        ],
        out_specs=[pl.BlockSpec((ATT_BATCH, TILE, QKV_W), lambda b, i: (b, i, 0)),
                   pl.BlockSpec((ATT_BATCH, TILE, GROUP_W), lambda b, i: (b, i, 0))],
        out_shape=[_sds((n_batch, n_tok, QKV_W), BF16), _sds((n_batch, n_tok, GROUP_W), F32)],
        compiler_params=_params(("arbitrary", "arbitrary")),
        name="inproj",
    )(*xs, mod, mod, g1, w_in, cos_t, sin_t, gq, gk, nq, nk)


def _gqa_kernel(flag_ref, qa_ref, qb_ref, k_ref, v_ref, o_ref, *, n_ctx, tile_off):
    def attend(n_keys, bounded):
        for b in range(ATT_BATCH):
            k = k_ref[b, :n_keys, :]
            v = v_ref[b, :n_keys, :]
            outs = []
            for q_ref in (qa_ref, qb_ref):
                q2 = _split_heads(q_ref[b])
                outs.append(_merge_heads(_softmax_pv(_nt_dot(q2, k), v, bounded), TILE))
            o_ref[b] = jnp.concatenate(outs, axis=-1).astype(o_ref.dtype)

    if tile_off == 0:
        is_ctx = pl.program_id(1) == 0
        pl.when(is_ctx)(lambda: _when_bounded(flag_ref, functools.partial(attend, n_ctx)))
        pl.when(jnp.logical_not(is_ctx))(lambda: _when_bounded(flag_ref, functools.partial(attend, k_ref.shape[1])))
    else:
        _when_bounded(flag_ref, functools.partial(attend, k_ref.shape[1]))


_SMEM_SPEC = pl.BlockSpec(memory_space=pltpu.SMEM)


def _gqa(bounded, qkv, *, need_ctx, n_ctx):
    n_batch, n_tok, _ = qkv.shape
    off = 0 if need_ctx else n_ctx // TILE
    assert n_batch % ATT_BATCH == 0
    q_spec = lambda col: pl.BlockSpec((ATT_BATCH, TILE, LANES), lambda b, i: (b, i + off, col))
    kv_spec = lambda col: pl.BlockSpec((ATT_BATCH, n_tok, LANES), lambda b, i: (b, 0, col))
    return pl.pallas_call(
        functools.partial(_gqa_kernel, n_ctx=n_ctx, tile_off=off),
        grid=(n_batch // ATT_BATCH, n_tok // TILE - off),
        in_specs=[_SMEM_SPEC, q_spec(0), q_spec(1), kv_spec(2), kv_spec(3)],
        out_specs=pl.BlockSpec((ATT_BATCH, TILE, GROUP_W), lambda b, i: (b, i, 0)),
        out_shape=_sds((n_batch, n_tok - off * TILE, GROUP_W), BF16),
        compiler_params=_params(("arbitrary", "arbitrary")),
        name="gqa",
    )(bounded, qkv, qkv, qkv, qkv)


def _na_kernel(flag_ref, q_ref, kp_ref, kc_ref, kn_ref, kx_ref, vp_ref, vc_ref, vn_ref, vx_ref, tbl_ref, o_ref, *, tile_off):
    def attend(local, bounded):
        for b in range(ATT_BATCH):
            outs = []
            for pair in range(2):
                cols = slice(pair * LANES, (pair + 1) * LANES)
                q2 = _split_heads(q_ref[b, :, cols])
                if local:
                    k = jnp.concatenate([r[b, :, cols] for r in (kp_ref, kc_ref, kn_ref, kx_ref)], axis=0)
                    v = jnp.concatenate([r[b, :, cols] for r in (vp_ref, vc_ref, vn_ref, vx_ref)], axis=0)
                    bias = jnp.concatenate([tbl_ref[0, 2 * pair], tbl_ref[0, 2 * pair + 1]], axis=0)
                    s = _nt_dot(q2, k)
                    s = jnp.concatenate([s[:, :3 * TILE] + bias, s[:, 3 * TILE:]], axis=-1)
                else:
                    k = kx_ref[b, :, cols]
                    v = vx_ref[b, :, cols]
                    s = _nt_dot(q2, k)
                outs.append(_merge_heads(_softmax_pv(s, v, bounded), TILE))
            o_ref[b] = jnp.concatenate(outs, axis=-1).astype(o_ref.dtype)

    if tile_off == 0:
        is_ctx = pl.program_id(1) == 0
        pl.when(is_ctx)(lambda: _when_bounded(flag_ref, functools.partial(attend, False)))
        pl.when(jnp.logical_not(is_ctx))(lambda: _when_bounded(flag_ref, functools.partial(attend, True)))
    else:
        _when_bounded(flag_ref, functools.partial(attend, True))


def _na(bounded, qkv, table, *, need_ctx, n_ctx):
    n_batch, n_tok, _ = qkv.shape
    assert n_ctx == TILE
    n_tiles = n_tok // TILE
    off = 0 if need_ctx else 1
    assert n_batch % ATT_BATCH == 0
    blk = (ATT_BATCH, TILE, GROUP_W)
    cur = lambda col: pl.BlockSpec(blk, lambda b, i: (b, i + off, col))
    prev = lambda col: pl.BlockSpec(blk, lambda b, i: (b, jnp.maximum(i + off - 1, 1), col))
    nxt = lambda col: pl.BlockSpec(blk, lambda b, i: (b, jnp.minimum(i + off + 1, n_tiles - 1), col))
    ctx = lambda col: pl.BlockSpec(blk, lambda b, i: (b, 0, col))

    def cls(b, i):
        t = i + off
        return (jnp.where(t == n_tiles - 1, 2, jnp.where(t <= 1, 0, 1)), 0, 0, 0)

    return pl.pallas_call(
        functools.partial(_na_kernel, tile_off=off),
        grid=(n_batch // ATT_BATCH, n_tiles - off),
        in_specs=[_SMEM_SPEC, cur(2), prev(3), cur(3), nxt(3), ctx(3), prev(4), cur(4), nxt(4), ctx(4),
                  pl.BlockSpec((1, NA_HEADS, TILE, 3 * TILE), cls)],
        out_specs=pl.BlockSpec(blk, lambda b, i: (b, i, 0)),
        out_shape=_sds((n_batch, n_tok - off * TILE, GROUP_W), BF16),
        compiler_params=_params(("arbitrary", "arbitrary")),
        name="na",
    )(bounded, *([qkv] * 9), table)


def _na_bias_table(rel_bias, rows):
    n_bands = rows // BAND_ROWS
    assert rows >= WIN_H and rows % BAND_ROWS == 0
    n_dr, n_dc = 2 * WIN_H - 1, 2 * WIN_W - 1
    w = GRID_W
    shift = w - WIN_W
    vp = jnp.pad(rel_bias.astype(F32) * LOG2E, ((0, 0), (0, 0), (shift, 2 * w - shift - n_dc)))
    skew = jnp.broadcast_to(vp[:, :, None, :], (NA_HEADS, n_dr, w, 2 * w)).reshape(NA_HEADS, n_dr, 2 * w * w)
    col = skew[:, :, :w * (2 * w - 1)].reshape(NA_HEADS, n_dr, w, 2 * w - 1)[:, :, :, w - 1:]
    qc = np.arange(w)
    cs = np.clip(qc - WIN_W // 2, 0, w - WIN_W)
    col_ok = (qc[None, :] >= cs[:, None]) & (qc[None, :] < cs[:, None] + WIN_W)
    col = jnp.where(col_ok[None, None], col, NEG_BIG)
    masked = jnp.full((NA_HEADS, w, w), NEG_BIG, F32)
    classes = []
    for band in (0, min(1, n_bands - 1), n_bands - 1):
        q_rows = []
        for i in range(BAND_ROWS):
            r = BAND_ROWS * band + i
            rs = min(max(r - WIN_H // 2, 0), rows - WIN_H)
            blocks = []
            for kk in range(3 * BAND_ROWS):
                kr = BAND_ROWS * (band - 1) + kk
                ok = 0 <= kr < rows and rs <= kr < rs + WIN_H
                blocks.append(col[:, kr - r + WIN_H - 1] if ok else masked)
            q_rows.append(jnp.concatenate(blocks, axis=-1))
        classes.append(jnp.concatenate(q_rows, axis=-2))
    return jnp.stack(classes)


def _fourier_kernel(x_ref, cs_ref, dft_ref, dftc_ref, w_ref, o_ref, x12_ref, *, n_ctx, n_lat, tile_off):
    t = pl.program_id(1) + tile_off

    def head_dft(x):
        x12 = jnp.dot(x, cs_ref[...], preferred_element_type=F32).astype(BF16)
        return jnp.concatenate([x12[:, :GROUP_W], x12[:, GROUP_W:]], axis=0)

    def finish(y, n):
        y = y * (1.0 / math.sqrt(n * FNO_HEAD_W))
        o_ref[0] = jnp.dot(y.astype(BF16), w_ref[...], preferred_element_type=F32).astype(o_ref.dtype)

    if tile_off == 0:
        @pl.when(t == 0)
        def _():
            x12 = head_dft(x_ref[0, :n_ctx, :])
            finish(jnp.dot(dftc_ref[...], x12, preferred_element_type=F32), n_ctx)

    @pl.when(t == n_ctx // TILE)
    def _():
        x12_ref[...] = head_dft(x_ref[0, n_ctx:, :])

    @pl.when(t >= n_ctx // TILE)
    def _():
        row0 = pl.multiple_of((t - n_ctx // TILE) * TILE, TILE)
        finish(jnp.dot(dft_ref[pl.ds(row0, TILE), :], x12_ref[...], preferred_element_type=F32), n_lat)


def _fourier(qkv, cs_bd, dft, dft_ctx, w_f, *, need_ctx, n_ctx):
    n_batch, n_tok, _ = qkv.shape
    n_lat = n_tok - n_ctx
    off = 0 if need_ctx else n_ctx // TILE
    const = lambda shape: pl.BlockSpec(shape, lambda b, i: (0,) * len(shape))
    return pl.pallas_call(
        functools.partial(_fourier_kernel, n_ctx=n_ctx, n_lat=n_lat, tile_off=off),
        grid=(n_batch, n_tok // TILE - off),
        in_specs=[pl.BlockSpec((1, n_tok, GROUP_W), lambda b, i: (b, 0, 5)),
                  const(cs_bd.shape), const(dft.shape), const(dft_ctx.shape), const(w_f.shape)],
        out_specs=pl.BlockSpec((1, TILE, GROUP_W), lambda b, i: (b, i, 0)),
        out_shape=_sds((n_batch, n_tok - off * TILE, GROUP_W), BF16),
        scratch_shapes=[pltpu.VMEM((2 * n_lat, GROUP_W), BF16)],
        compiler_params=_params(("arbitrary", "arbitrary")),
        name="fourier",
    )(qkv, cs_bd, dft, dft_ctx, w_f)


def _dft_matrix(n):
    f = 32
    assert n % f == 0
    k = lax.broadcasted_iota(jnp.int32, (n, 1), 0)
    m1 = lax.broadcasted_iota(jnp.int32, (1, n // f), 1)
    m0 = lax.broadcasted_iota(jnp.int32, (1, f), 1)
    ang_a = ((k * (m1 * f)) % n).astype(F32) * (2.0 * math.pi / n)
    ang_b = ((k * m0) % n).astype(F32) * (2.0 * math.pi / n)
    ca, sa = jnp.cos(ang_a)[:, :, None], jnp.sin(ang_a)[:, :, None]
    cb, sb = jnp.cos(ang_b)[:, None, :], jnp.sin(ang_b)[:, None, :]
    cos = (ca * cb - sa * sb).reshape(n, n)
    sin = (sa * cb + ca * sb).reshape(n, n)
    return jnp.concatenate([cos, -sin], axis=1).astype(BF16)


def _head_dft_matrix():
    a = np.arange(GROUP_W)
    same = (a[:, None] // FNO_HEAD_W) == (a[None, :] // FNO_HEAD_W)
    ang = 2.0 * np.pi * (((a[:, None] % FNO_HEAD_W) * (a[None, :] % FNO_HEAD_W)) % FNO_HEAD_W) / FNO_HEAD_W
    m = np.concatenate([np.where(same, np.cos(ang), 0.0), np.where(same, np.sin(ang), 0.0)], axis=1)
    return jnp.asarray(m, F32).astype(BF16)


def _s5_kernel(u_ref, a_ref, bd_ref, cd_ref, y_ref, bu_ref, hb_ref, h_ref, *, n_batch, reverse):
    @pl.when(pl.program_id(0) == 0)
    def _():
        h_ref[...] = jnp.zeros_like(h_ref)

    sub_rows = S5_SUB * n_batch
    chunk = 4 * LANES
    def project(q):
        u = pltpu.einshape("btc->tbc", u_ref[:, q * S5_SUB:(q + 1) * S5_SUB, :])
        u = u.reshape(sub_rows, GROUP_W).astype(BF16)
        bu_ref[q * sub_rows:(q + 1) * sub_rows, :] = jnp.dot(u, bd_ref[0], preferred_element_type=F32)

    def scan(q):
        steps = range(S5_SUB - 1, -1, -1) if reverse else range(S5_SUB)
        for sc in range(SSM_W // chunk):
            re = slice(sc * chunk, (sc + 1) * chunk)
            im = slice(SSM_W + sc * chunk, SSM_W + (sc + 1) * chunk)
            a_re = jnp.broadcast_to(a_ref[0, :, re], (n_batch, chunk))
            a_im = jnp.broadcast_to(a_ref[0, :, im], (n_batch, chunk))
            h_re, h_im = h_ref[:, re], h_ref[:, im]
            for t in steps:
                r = slice(q * sub_rows + t * n_batch, q * sub_rows + (t + 1) * n_batch)
                h_re, h_im = (a_re * h_re - a_im * h_im + bu_ref[r, re],
                              a_re * h_im + a_im * h_re + bu_ref[r, im])
                hb_ref[r, re] = h_re.astype(BF16)
                hb_ref[r, im] = h_im.astype(BF16)
            h_ref[:, re] = h_re
            h_ref[:, im] = h_im

    def readout(q):
        y = jnp.dot(hb_ref[q * sub_rows:(q + 1) * sub_rows, :], cd_ref[0], preferred_element_type=F32)
        y = pltpu.einshape("tbc->btc", y.reshape(S5_SUB, n_batch, GROUP_W))
        y_ref[:, q * S5_SUB:(q + 1) * S5_SUB, :] = y.astype(y_ref.dtype)

    order = list(range(S5_STEPS // S5_SUB))
    if reverse:
        order.reverse()
    project(order[0])
    for j, q in enumerate(order):
        if j + 1 < len(order):
            project(order[j + 1])
        scan(q)
        if j > 0:
            readout(order[j - 1])
    readout(order[-1])


def _s5(u, a_tab, b_dense, c_dense, *, n_ctx, reverse):
    n_batch, n_tok, _ = u.shape
    n_blk = n_tok // S5_STEPS
    n_cblk = n_ctx // S5_STEPS
    d = int(reverse)
    if reverse:
        blk = lambda s: jnp.where(s < n_cblk, n_cblk - 1 - s, n_blk - 1 - (s - n_cblk))
    else:
        blk = lambda s: s
    return pl.pallas_call(
        functools.partial(_s5_kernel, n_batch=n_batch, reverse=reverse),
        grid=(n_blk,),
        in_specs=[pl.BlockSpec((n_batch, S5_STEPS, GROUP_W), lambda s: (0, blk(s), 0)),
                  pl.BlockSpec((1, 1, 2 * SSM_W), lambda s: (d, 0, 0)),
                  pl.BlockSpec((1, GROUP_W, 2 * SSM_W), lambda s: (d, 0, 0)),
                  pl.BlockSpec((1, 2 * SSM_W, GROUP_W), lambda s: (d, 0, 0))],
        out_specs=pl.BlockSpec((n_batch, S5_STEPS, GROUP_W), lambda s: (0, blk(s), 0)),
        out_shape=_sds((n_batch, n_tok, GROUP_W), BF16),
        scratch_shapes=[pltpu.VMEM((S5_STEPS * n_batch, 2 * SSM_W), F32),
                        pltpu.VMEM((S5_STEPS * n_batch, 2 * SSM_W), BF16),
                        pltpu.VMEM((n_batch, 2 * SSM_W), F32)],
        compiler_params=_params(("arbitrary",)),
        name="s5",
    )(u, a_tab, b_dense, c_dense)


def _s5_tables(lam_re, lam_im, log_dt, b_re, b_im, c_re, c_im):
    lam = lax.complex(lam_re.astype(F32), lam_im.astype(F32))
    dt = jnp.exp(log_dt.astype(F32))[..., None]
    a_bar = jnp.exp(lam * dt)
    b_bar = ((a_bar - 1.0) / lam)[..., None] * lax.complex(b_re.astype(F32), b_im.astype(F32))
    eye = jnp.eye(SSM_GROUPS, dtype=F32)

    def embed_b(m):
        m = jnp.transpose(m, (0, 1, 3, 2))[:, :, :, None, :] * eye[None, :, None, :, None]
        return m.reshape(2, GROUP_W, SSM_W)

    def embed_c(m):
        m = jnp.transpose(m, (0, 1, 3, 2))[:, :, :, None, :] * eye[None, :, None, :, None]
        return m.reshape(2, SSM_W, GROUP_W)

    a_tab = jnp.concatenate([jnp.real(a_bar).reshape(2, 1, SSM_W), jnp.imag(a_bar).reshape(2, 1, SSM_W)], axis=-1)
    b_dense = jnp.concatenate([embed_b(jnp.real(b_bar)), embed_b(jnp.imag(b_bar))], axis=-1).astype(BF16)
    c_dense = jnp.concatenate([embed_c(c_re.astype(F32)), -embed_c(c_im.astype(F32))], axis=1).astype(BF16)
    return a_tab, b_dense, c_dense


def _gelu_tanh(x):
    return x * (0.5 * (1.0 + jnp.tanh(math.sqrt(2.0 / math.pi) * (x + 0.044715 * (x * x * x)))))


def _mixout_kernel(*refs, split_input, tiles_per_batch, tile_off):
    if split_input:
        x_ref, ctx_ref, *refs = refs
    else:
        x_ref, *refs = refs
    (mod_ref, oa_ref, od_ref, of_ref, u_ref, yf_ref, yb_ref, d_ref, wg_ref, bg_ref, gg_ref, wo_ref,
     g2_ref, w1_ref, w3_ref, w2_ref, o_ref, x1_ref, h_ref, ga2_ref) = refs
    s = pl.program_id(0)
    n_items = pl.num_programs(0) - 1

    def feed_forward():
        h = h_ref[...]
        acc = jnp.zeros((TILE, D_MODEL), F32)
        for c in range(D_FF // FF_CHUNK):
            cols = slice(c * FF_CHUNK, (c + 1) * FF_CHUNK)
            a = jnp.dot(h, w1_ref[:, cols], preferred_element_type=F32)
            b = jnp.dot(h, w3_ref[:, cols], preferred_element_type=F32)
            t = (a * jax.nn.sigmoid(a)) * b
            acc = acc + jnp.dot(t.astype(BF16), w2_ref[cols, :], preferred_element_type=F32)
        o_ref[0] = x1_ref[...] + ga2_ref[...] * acc

    def mix():
        x = x_ref[0]
        if split_input:
            is_ctx = lax.rem(jnp.minimum(s, n_items - 1), tiles_per_batch) + tile_off == 0
            x = jnp.where(is_ctx, ctx_ref[0], x)
        _, _, ga1 = _mod_slices(mod_ref[0], 0)
        sh2, sc2, ga2 = _mod_slices(mod_ref[0], 3)
        y = u_ref[0] * d_ref[...] + yf_ref[0].astype(F32) + yb_ref[0].astype(F32)
        g = _gelu_tanh(y)
        z = jnp.dot(g.astype(BF16), wg_ref[...], preferred_element_type=F32) + bg_ref[...]
        o_s = g * jax.nn.sigmoid(z)
        parts = [oa_ref[0].astype(F32), od_ref[0].astype(F32), of_ref[0].astype(F32), o_s]
        normed = [(_rms_rows(p) * gg_ref[:, k * GROUP_W:(k + 1) * GROUP_W]).astype(BF16)
                  for k, p in enumerate(parts)]
        r = jnp.dot(jnp.concatenate(normed, axis=-1), wo_ref[...], preferred_element_type=F32)
        x = x + ga1 * r
        x1_ref[...] = x
        h_ref[...] = ((_rms_rows(x) * g2_ref[...]) * (1.0 + sc2) + sh2).astype(BF16)
        ga2_ref[...] = ga2

    pl.when(s == 0)(mix)

    @pl.when(jnp.logical_and(s > 0, s < n_items))
    def _():
        feed_forward()
        mix()

    pl.when(s == n_items)(feed_forward)


def _mixout_ffn(xs, mod, o_a, o_d, o_f, u, y_fwd, y_bwd, d_skip, w_glu, b_glu, g_group, w_out, g2, w1, w3, w2, *,
                need_ctx, n_ctx):
    split_input = len(xs) == 2
    n_batch, n_tok, _ = u.shape
    off = 0 if need_ctx else n_ctx // TILE
    n_out = n_tok - off * TILE
    nt = n_out // TILE
    n_items = n_batch * nt
    tok = (1, TILE, D_MODEL)

    def item(fn):
        def index_map(s):
            a = jnp.minimum(s, n_items - 1)
            return fn(a // nt, a % nt)
        return index_map

    def out_map(s):
        a = jnp.maximum(s - 1, 0)
        return (a // nt, a % nt, 0)

    if split_input:
        assert need_ctx
        x_specs = [pl.BlockSpec(tok, item(lambda b, i: (b, jnp.maximum(i - 1, 0), 0))),
                   pl.BlockSpec(tok, item(lambda b, i: (b, 0, 0)))]
    else:
        x_specs = [pl.BlockSpec(tok, item(lambda b, i: (b, i + off, 0)))]
    grp = pl.BlockSpec((1, TILE, GROUP_W), item(lambda b, i: (b, i + off, 0)))
    mix = pl.BlockSpec((1, TILE, GROUP_W), item(lambda b, i: (b, i, 0)))
    const = lambda shape: pl.BlockSpec(shape, lambda s: (0,) * len(shape), pipeline_mode=pl.Buffered(1))
    return pl.pallas_call(
        functools.partial(_mixout_kernel, split_input=split_input, tiles_per_batch=nt, tile_off=off),
        grid=(n_items + 1,),
        in_specs=x_specs + [
            pl.BlockSpec((1, 1, N_MOD * D_MODEL), item(lambda b, i: (jnp.where(i + off == 0, n_batch, b), 0, 0))),
            mix, mix, mix, grp, grp, grp,
            const((1, GROUP_W)), const((GROUP_W, GROUP_W)), const((1, GROUP_W)),
            const((1, D_MODEL)), const((D_MODEL, D_MODEL)),
            const((1, D_MODEL)), const(w1.shape), const(w3.shape), const(w2.shape),
        ],
        out_specs=pl.BlockSpec(tok, out_map),
        out_shape=_sds((n_batch, n_out, D_MODEL), F32),
        scratch_shapes=[pltpu.VMEM((TILE, D_MODEL), F32), pltpu.VMEM((TILE, D_MODEL), BF16),
                        pltpu.VMEM((1, D_MODEL), F32)],
        compiler_params=_params(("arbitrary",)),
        name="mixffn",
    )(*xs, mod, o_a, o_d, o_f, u, y_fwd, y_bwd, d_skip, w_glu, b_glu, g_group, w_out, g2, w1, w3, w2)


def _rope_tables(n_lat, n_ctx):
    t = jnp.arange(n_lat, dtype=jnp.int32)
    rows = (t // GRID_W).astype(F32)
    cols = (t % GRID_W).astype(F32)
    axis_dim = HEAD_DIM // 2
    inv_freq = ROPE_THETA ** (-jnp.arange(0, axis_dim, 2, dtype=F32) / axis_dim)
    ang_r = rows[:, None] * inv_freq[None, :]
    ang_c = cols[:, None] * inv_freq[None, :]
    cos = jnp.concatenate([jnp.cos(ang_r)] * 2 + [jnp.cos(ang_c)] * 2, axis=-1)
    sin = jnp.concatenate([-jnp.sin(ang_r), jnp.sin(ang_r), -jnp.sin(ang_c), jnp.sin(ang_c)], axis=-1)
    cos = jnp.concatenate([jnp.ones((n_ctx, HEAD_DIM), F32), cos], axis=0)
    sin = jnp.concatenate([jnp.zeros((n_ctx, HEAD_DIM), F32), sin], axis=0)
    return jnp.tile(cos, (1, 2)), jnp.tile(sin, (1, 2))


def kernel(x, c, ctx, c_ctx, w_mod, b_mod, g_norm1, w_in, att_q_gain, att_k_gain, na_q_gain, na_k_gain, na_rel_bias, w_fourier, ssm_lam_re, ssm_lam_im, ssm_log_dt, ssm_b_re, ssm_b_im, ssm_c_re, ssm_c_im, ssm_d, w_glu, b_glu, g_group, w_out, g_norm2, w_ff1, w_ff3, w_ff2):
    n_batch, n_lat, d = x.shape
    n_ctx = ctx.shape[1]
    n_tok = n_lat + n_ctx
    assert d == D_MODEL and n_ctx == TILE and n_lat % TILE == 0 and n_lat % GRID_W == 0
    assert n_batch % 8 == 0 and n_tok % S5_STEPS == 0

    mod_rows = -(-(n_batch + 1) // 8) * 8
    cc = jnp.concatenate([c, c_ctx[None, :], jnp.zeros((mod_rows - n_batch - 1, d), F32)], axis=0)
    mod = _modulation(cc, w_mod, b_mod.reshape(DEPTH, 1, N_MOD * d)).reshape(DEPTH, mod_rows, 1, N_MOD * d)

    cos_t, sin_t = _rope_tables(n_lat, n_ctx)
    cs_bd = _head_dft_matrix()
    dft = _dft_matrix(n_lat)
    dft_ctx = _dft_matrix(n_ctx)
    def q_perm(a, axis):
        heads = [lax.slice_in_dim(a, h * HEAD_DIM, (h + 1) * HEAD_DIM, axis=axis) for h in _Q_ORDER]
        return jnp.concatenate(heads + [lax.slice_in_dim(a, GROUP_W, a.shape[axis], axis=axis)], axis=axis)

    scale = HEAD_DIM ** -0.5 * LOG2E
    tile2 = lambda g, s=1.0: jnp.tile(g.astype(F32) * s, 2)[None, :]
    row = lambda v: v.astype(F32)[None, :]

    xs = (x, ctx)
    for l in range(DEPTH):
        need_ctx = l < DEPTH - 1
        qkv, u = _inproj(xs, mod[l], row(g_norm1[l]), q_perm(w_in[l], 1).astype(BF16), cos_t, sin_t,
                         tile2(att_q_gain[l], scale), tile2(att_k_gain[l]),
                         tile2(na_q_gain[l], scale), tile2(na_k_gain[l]),
                         n_batch=n_batch, n_tok=n_tok)
        o_a = _gqa(_logits_bounded(att_q_gain[l], att_k_gain[l]), qkv, need_ctx=need_ctx, n_ctx=n_ctx)
        o_d = _na(_logits_bounded(na_q_gain[l], na_k_gain[l], jnp.max(jnp.abs(na_rel_bias[l]))), qkv,
                  _na_bias_table(na_rel_bias[l], n_lat // GRID_W), need_ctx=need_ctx, n_ctx=n_ctx)
        o_f = _fourier(qkv, cs_bd, dft, dft_ctx, w_fourier[l].astype(BF16), need_ctx=need_ctx, n_ctx=n_ctx)
        a_tab, b_dense, c_dense = _s5_tables(ssm_lam_re[l], ssm_lam_im[l], ssm_log_dt[l], ssm_b_re[l],
                                             ssm_b_im[l], ssm_c_re[l], ssm_c_im[l])
        y_fwd = _s5(u, a_tab, b_dense, c_dense, n_ctx=n_ctx, reverse=False)
        y_bwd = _s5(u, a_tab, b_dense, c_dense, n_ctx=n_ctx, reverse=True)
        x2 = _mixout_ffn(xs, mod[l], o_a, o_d, o_f, u, y_fwd, y_bwd, row(ssm_d[l]), w_glu[l].astype(BF16),
                         row(b_glu[l]), row(q_perm(g_group[l], 0)), q_perm(w_out[l], 0).astype(BF16),
                         row(g_norm2[l]), w_ff1[l].astype(BF16), w_ff3[l].astype(BF16), w_ff2[l].astype(BF16),
                         need_ctx=need_ctx, n_ctx=n_ctx)
        xs = (x2,)
    return xs[0]
```

```python
import functools
import math

import numpy as np
import jax
import jax.numpy as jnp
from jax import lax
from jax.experimental import pallas as pl
from jax.experimental.pallas import tpu as pltpu

F32 = jnp.float32
BF16 = jnp.bfloat16

D_MODEL = 1024
DEPTH = 2
GRID_W = 64
HEAD_DIM = 64
N_GROUPS = 4
GROUP_W = D_MODEL // N_GROUPS
WIN_H = 8
WIN_W = 16
NA_HEADS = GROUP_W // HEAD_DIM
FNO_HEAD_W = 64
SSM_GROUP = 16
SSM_GROUPS = GROUP_W // SSM_GROUP
SSM_STATE = 64
SSM_W = SSM_GROUPS * SSM_STATE
ROPE_THETA = 10000.0
D_FF = ((8 * D_MODEL // 3 + 255) // 256) * 256
N_MOD = 6
EPS = 1e-6
IN_W = 7 * GROUP_W
QKV_W = 6 * GROUP_W

LANES = 128
TILE = 256
BAND_ROWS = TILE // GRID_W
S5_STEPS = 128
S5_SUB = 32
FF_CHUNK = D_FF
ATT_BATCH = 4
ROPE_BLOCK = HEAD_DIM // 4
VMEM_LIMIT_BYTES = 56 * 2**20
NEG_BIG = -1e30
LOG2E = 1.0 / math.log(2.0)
EXP2_SAFE = 60.0

_Q_ORDER = (0, 2, 1, 3)


def _params(sem):
    return pltpu.CompilerParams(dimension_semantics=sem, vmem_limit_bytes=VMEM_LIMIT_BYTES)


def _sds(shape, dtype):
    return jax.ShapeDtypeStruct(shape, dtype)


def _mod_kernel(c_ref, w_ref, b_ref, o_ref):
    c = c_ref[...]
    s = c * jax.nn.sigmoid(c)
    w = w_ref[0]
    s_hi, w_hi = s.astype(BF16), w.astype(BF16)
    s_lo = (s - s_hi.astype(F32)).astype(BF16)
    w_lo = (w - w_hi.astype(F32)).astype(BF16)
    dot = functools.partial(jnp.dot, preferred_element_type=F32)
    o_ref[0] = dot(s_hi, w_hi) + (dot(s_lo, w_hi) + dot(s_hi, w_lo)) + b_ref[0]


def _modulation(cc, w_mod, b_mod):
    rows, d = cc.shape
    n_blk = 4
    bw = N_MOD * d // n_blk
    return pl.pallas_call(
        _mod_kernel,
        grid=(DEPTH, n_blk),
        in_specs=[
            pl.BlockSpec((rows, d), lambda l, j: (0, 0)),
            pl.BlockSpec((1, d, bw), lambda l, j: (l, 0, j)),
            pl.BlockSpec((1, 1, bw), lambda l, j: (l, 0, j)),
        ],
        out_specs=pl.BlockSpec((1, rows, bw), lambda l, j: (l, 0, j)),
        out_shape=_sds((DEPTH, rows, N_MOD * d), F32),
        compiler_params=_params(("arbitrary", "arbitrary")),
        name="modulation",
    )(cc, w_mod, b_mod)


def _rms_rows(x):
    return x * lax.rsqrt(jnp.mean(x * x, axis=-1, keepdims=True) + EPS)


def _mod_slices(m, first):
    return [m[:, (first + k) * D_MODEL:(first + k + 1) * D_MODEL] for k in range(3)]


def _head_norm(t, gain):
    lo = lax.broadcasted_iota(jnp.int32, t.shape, 1) < HEAD_DIM
    sq = t * t
    s_lo = jnp.sum(jnp.where(lo, sq, 0.0), axis=-1, keepdims=True)
    s_hi = jnp.sum(jnp.where(lo, 0.0, sq), axis=-1, keepdims=True)
    ms = jnp.where(lo, s_lo, s_hi) * (1.0 / HEAD_DIM)
    return t * lax.rsqrt(ms + EPS) * gain


def _rope(t, cos, sin):
    first = (lax.broadcasted_iota(jnp.int32, t.shape, 1) & ROPE_BLOCK) == 0
    partner = jnp.where(first, pltpu.roll(t, LANES - ROPE_BLOCK, axis=1), pltpu.roll(t, ROPE_BLOCK, axis=1))
    return t * cos + partner * sin


def _softmax_pv(s, v, bounded):
    p = jnp.exp2(s if bounded else s - jnp.max(s, axis=-1, keepdims=True))
    l = jnp.sum(p, axis=-1, keepdims=True)
    return jnp.dot(p.astype(BF16), v, preferred_element_type=F32) / l


def _when_bounded(flag_ref, body):
    bounded = flag_ref[0] == 1
    pl.when(bounded)(lambda: body(True))
    pl.when(jnp.logical_not(bounded))(lambda: body(False))


def _logits_bounded(gq, gk, extra=0.0):
    bound = math.sqrt(HEAD_DIM) * jnp.max(jnp.abs(gq)) * jnp.max(jnp.abs(gk)) + extra
    return (bound * LOG2E <= EXP2_SAFE).astype(jnp.int32).reshape(1)


def _split_heads(q):
    lo = lax.broadcasted_iota(jnp.int32, q.shape, 1) < HEAD_DIM
    zero = jnp.zeros_like(q)
    return jnp.concatenate([jnp.where(lo, q, zero), jnp.where(lo, zero, q)], axis=0)


def _merge_heads(o, rows):
    lo = lax.broadcasted_iota(jnp.int32, (rows, LANES), 1) < HEAD_DIM
    return jnp.where(lo, o[:rows], o[rows:])


def _nt_dot(a, b):
    return lax.dot_general(a, b, (((1,), (1,)), ((), ())), preferred_element_type=F32)


def _inproj_kernel(*refs, split_input):
    if split_input:
        x_ref, ctx_ref, *refs = refs
    else:
        x_ref, *refs = refs
    (mod_ref, modc_ref, g1_ref, w_ref, cos_ref, sin_ref, gq_ref, gk_ref, nq_ref, nk_ref, qkv_ref, u_ref) = refs
    is_ctx = pl.program_id(1) == 0
    cos = cos_ref[...]
    sin = sin_ref[...]
    for b in range(ATT_BATCH):
        x = x_ref[b]
        if split_input:
            x = jnp.where(is_ctx, ctx_ref[b], x)
        sh1, sc1, _ = _mod_slices(jnp.where(is_ctx, modc_ref[0], mod_ref[b]), 0)
        h = (_rms_rows(x) * g1_ref[...]) * (1.0 + sc1) + sh1
        p = jnp.dot(h.astype(BF16), w_ref[...], preferred_element_type=F32)

        def tile(j, p=p):
            return p[:, j * LANES:(j + 1) * LANES]

        out = [
            _rope(_head_norm(tile(0), gq_ref[...]), cos, sin),
            _rope(_head_norm(tile(1), gq_ref[...]), cos, sin),
            _rope(_head_norm(tile(2), gk_ref[...]), cos, sin),
            tile(3),
            _head_norm(tile(4), nq_ref[...]),
            _head_norm(tile(5), nq_ref[...]),
            _head_norm(tile(6), nk_ref[...]),
            _head_norm(tile(7), nk_ref[...]),
            tile(8), tile(9), tile(10), tile(11),
        ]
        for j, t in enumerate(out):
            qkv_ref[b, :, j * LANES:(j + 1) * LANES] = t.astype(BF16)
        u_ref[b] = p[:, QKV_W:]


def _inproj(xs, mod, g1, w_in, cos_t, sin_t, gq, gk, nq, nk, *, n_batch, n_tok):
    split_input = len(xs) == 2
    n_tiles = n_tok // TILE
    assert n_batch % ATT_BATCH == 0
    tok = (ATT_BATCH, TILE, D_MODEL)
    if split_input:
        x_specs = [pl.BlockSpec(tok, lambda b, i: (b, jnp.maximum(i - 1, 0), 0)),
                   pl.BlockSpec(tok, lambda b, i: (b, 0, 0))]
    else:
        x_specs = [pl.BlockSpec(tok, lambda b, i: (b, i, 0))]
    vec = lambda w: pl.BlockSpec((1, w), lambda b, i: (0, 0))
    return pl.pallas_call(
        functools.partial(_inproj_kernel, split_input=split_input),
        grid=(n_batch // ATT_BATCH, n_tiles),
        in_specs=x_specs + [
            pl.BlockSpec((ATT_BATCH, 1, N_MOD * D_MODEL), lambda b, i: (b, 0, 0)),
            pl.BlockSpec((1, 1, N_MOD * D_MODEL), lambda b, i: (n_batch, 0, 0)),
            vec(D_MODEL),
            pl.BlockSpec((D_MODEL, IN_W), lambda b, i: (0, 0)),
            pl.BlockSpec((TILE, LANES), lambda b, i: (i, 0)),
            pl.BlockSpec((TILE, LANES), lambda b, i: (i, 0)),
            vec(LANES), vec(LANES), vec(LANES), vec(LANES),
        ],
        out_specs=[pl.BlockSpec((ATT_BATCH, TILE, QKV_W), lambda b, i: (b, i, 0)),
                   pl.BlockSpec((ATT_BATCH, TILE, GROUP_W), lambda b, i: (b, i, 0))],
        out_shape=[_sds((n_batch, n_tok, QKV_W), BF16), _sds((n_batch, n_tok, GROUP_W), F32)],
        compiler_params=_params(("arbitrary", "arbitrary")),
        name="inproj",
    )(*xs, mod, mod, g1, w_in, cos_t, sin_t, gq, gk, nq, nk)


def _gqa_kernel(flag_ref, qa_ref, qb_ref, k_ref, v_ref, o_ref, *, n_ctx, tile_off):
    def attend(n_keys, bounded):
        for b in range(ATT_BATCH):
            k = k_ref[b, :n_keys, :]
            v = v_ref[b, :n_keys, :]
            outs = []
            for q_ref in (qa_ref, qb_ref):
                q2 = _split_heads(q_ref[b])
                outs.append(_merge_heads(_softmax_pv(_nt_dot(q2, k), v, bounded), TILE))
            o_ref[b] = jnp.concatenate(outs, axis=-1).astype(o_ref.dtype)

    if tile_off == 0:
        is_ctx = pl.program_id(1) == 0
        pl.when(is_ctx)(lambda: _when_bounded(flag_ref, functools.partial(attend, n_ctx)))
        pl.when(jnp.logical_not(is_ctx))(lambda: _when_bounded(flag_ref, functools.partial(attend, k_ref.shape[1])))
    else:
        _when_bounded(flag_ref, functools.partial(attend, k_ref.shape[1]))


_SMEM_SPEC = pl.BlockSpec(memory_space=pltpu.SMEM)


def _gqa(bounded, qkv, *, need_ctx, n_ctx):
    n_batch, n_tok, _ = qkv.shape
    off = 0 if need_ctx else n_ctx // TILE
    assert n_batch % ATT_BATCH == 0
    q_spec = lambda col: pl.BlockSpec((ATT_BATCH, TILE, LANES), lambda b, i: (b, i + off, col))
    kv_spec = lambda col: pl.BlockSpec((ATT_BATCH, n_tok, LANES), lambda b, i: (b, 0, col))
    return pl.pallas_call(
        functools.partial(_gqa_kernel, n_ctx=n_ctx, tile_off=off),
        grid=(n_batch // ATT_BATCH, n_tok // TILE - off),
        in_specs=[_SMEM_SPEC, q_spec(0), q_spec(1), kv_spec(2), kv_spec(3)],
        out_specs=pl.BlockSpec((ATT_BATCH, TILE, GROUP_W), lambda b, i: (b, i, 0)),
        out_shape=_sds((n_batch, n_tok - off * TILE, GROUP_W), BF16),
        compiler_params=_params(("arbitrary", "arbitrary")),
        name="gqa",
    )(bounded, qkv, qkv, qkv, qkv)


def _na_kernel(flag_ref, q_ref, kp_ref, kc_ref, kn_ref, kx_ref, vp_ref, vc_ref, vn_ref, vx_ref, tbl_ref, o_ref, *, tile_off):
    def attend(local, bounded):
        for b in range(ATT_BATCH):
            outs = []
            for pair in range(2):
                cols = slice(pair * LANES, (pair + 1) * LANES)
                q2 = _split_heads(q_ref[b, :, cols])
                if local:
                    k = jnp.concatenate([r[b, :, cols] for r in (kp_ref, kc_ref, kn_ref, kx_ref)], axis=0)
                    v = jnp.concatenate([r[b, :, cols] for r in (vp_ref, vc_ref, vn_ref, vx_ref)], axis=0)
                    bias = jnp.concatenate([tbl_ref[0, 2 * pair], tbl_ref[0, 2 * pair + 1]], axis=0)
                    s = _nt_dot(q2, k)
                    s = jnp.concatenate([s[:, :3 * TILE] + bias, s[:, 3 * TILE:]], axis=-1)
                else:
                    k = kx_ref[b, :, cols]
                    v = vx_ref[b, :, cols]
                    s = _nt_dot(q2, k)
                outs.append(_merge_heads(_softmax_pv(s, v, bounded), TILE))
            o_ref[b] = jnp.concatenate(outs, axis=-1).astype(o_ref.dtype)

    if tile_off == 0:
        is_ctx = pl.program_id(1) == 0
        pl.when(is_ctx)(lambda: _when_bounded(flag_ref, functools.partial(attend, False)))
        pl.when(jnp.logical_not(is_ctx))(lambda: _when_bounded(flag_ref, functools.partial(attend, True)))
    else:
        _when_bounded(flag_ref, functools.partial(attend, True))


def _na(bounded, qkv, table, *, need_ctx, n_ctx):
    n_batch, n_tok, _ = qkv.shape
    assert n_ctx == TILE
    n_tiles = n_tok // TILE
    off = 0 if need_ctx else 1
    assert n_batch % ATT_BATCH == 0
    blk = (ATT_BATCH, TILE, GROUP_W)
    cur = lambda col: pl.BlockSpec(blk, lambda b, i: (b, i + off, col))
    prev = lambda col: pl.BlockSpec(blk, lambda b, i: (b, jnp.maximum(i + off - 1, 1), col))
    nxt = lambda col: pl.BlockSpec(blk, lambda b, i: (b, jnp.minimum(i + off + 1, n_tiles - 1), col))
    ctx = lambda col: pl.BlockSpec(blk, lambda b, i: (b, 0, col))

    def cls(b, i):
        t = i + off
        return (jnp.where(t == n_tiles - 1, 2, jnp.where(t <= 1, 0, 1)), 0, 0, 0)

    return pl.pallas_call(
        functools.partial(_na_kernel, tile_off=off),
        grid=(n_batch // ATT_BATCH, n_tiles - off),
        in_specs=[_SMEM_SPEC, cur(2), prev(3), cur(3), nxt(3), ctx(3), prev(4), cur(4), nxt(4), ctx(4),
                  pl.BlockSpec((1, NA_HEADS, TILE, 3 * TILE), cls)],
        out_specs=pl.BlockSpec(blk, lambda b, i: (b, i, 0)),
        out_shape=_sds((n_batch, n_tok - off * TILE, GROUP_W), BF16),
        compiler_params=_params(("arbitrary", "arbitrary")),
        name="na",
    )(bounded, *([qkv] * 9), table)


def _na_bias_table(rel_bias, rows):
    n_bands = rows // BAND_ROWS
    assert rows >= WIN_H and rows % BAND_ROWS == 0
    n_dr, n_dc = 2 * WIN_H - 1, 2 * WIN_W - 1
    w = GRID_W
    shift = w - WIN_W
    vp = jnp.pad(rel_bias.astype(F32) * LOG2E, ((0, 0), (0, 0), (shift, 2 * w - shift - n_dc)))
    skew = jnp.broadcast_to(vp[:, :, None, :], (NA_HEADS, n_dr, w, 2 * w)).reshape(NA_HEADS, n_dr, 2 * w * w)
    col = skew[:, :, :w * (2 * w - 1)].reshape(NA_HEADS, n_dr, w, 2 * w - 1)[:, :, :, w - 1:]
    qc = np.arange(w)
    cs = np.clip(qc - WIN_W // 2, 0, w - WIN_W)
    col_ok = (qc[None, :] >= cs[:, None]) & (qc[None, :] < cs[:, None] + WIN_W)
    col = jnp.where(col_ok[None, None], col, NEG_BIG)
    masked = jnp.full((NA_HEADS, w, w), NEG_BIG, F32)
    classes = []
    for band in (0, min(1, n_bands - 1), n_bands - 1):
        q_rows = []
        for i in range(BAND_ROWS):
            r = BAND_ROWS * band + i
            rs = min(max(r - WIN_H // 2, 0), rows - WIN_H)
            blocks = []
            for kk in range(3 * BAND_ROWS):
                kr = BAND_ROWS * (band - 1) + kk
                ok = 0 <= kr < rows and rs <= kr < rs + WIN_H
                blocks.append(col[:, kr - r + WIN_H - 1] if ok else masked)
            q_rows.append(jnp.concatenate(blocks, axis=-1))
        classes.append(jnp.concatenate(q_rows, axis=-2))
    return jnp.stack(classes)


def _fourier_kernel(x_ref, cs_ref, dft_ref, dftc_ref, w_ref, o_ref, x12_ref, *, n_ctx, n_lat, tile_off):
    t = pl.program_id(1) + tile_off

    def head_dft(x):
        x12 = jnp.dot(x, cs_ref[...], preferred_element_type=F32).astype(BF16)
        return jnp.concatenate([x12[:, :GROUP_W], x12[:, GROUP_W:]], axis=0)

    def finish(b, y, n):
        y = y * (1.0 / math.sqrt(n * FNO_HEAD_W))
        o_ref[b] = jnp.dot(y.astype(BF16), w_ref[...], preferred_element_type=F32).astype(o_ref.dtype)

    if tile_off == 0:
        @pl.when(t == 0)
        def _():
            for b in range(ATT_BATCH):
                x12 = head_dft(x_ref[b, :n_ctx, :])
                finish(b, jnp.dot(dftc_ref[...], x12, preferred_element_type=F32), n_ctx)

    @pl.when(t == n_ctx // TILE)
    def _():
        for b in range(ATT_BATCH):
            x12_ref[b] = head_dft(x_ref[b, n_ctx:, :])

    @pl.when(t >= n_ctx // TILE)
    def _():
        row0 = pl.multiple_of((t - n_ctx // TILE) * TILE, TILE)
        for b in range(ATT_BATCH):
            finish(b, jnp.dot(dft_ref[pl.ds(row0, TILE), :], x12_ref[b], preferred_element_type=F32), n_lat)


def _fourier(qkv, cs_bd, dft, dft_ctx, w_f, *, need_ctx, n_ctx):
    n_batch, n_tok, _ = qkv.shape
    n_lat = n_tok - n_ctx
    off = 0 if need_ctx else n_ctx // TILE
    assert n_batch % ATT_BATCH == 0
    const = lambda shape: pl.BlockSpec(shape, lambda b, i: (0,) * len(shape), pipeline_mode=pl.Buffered(1))
    return pl.pallas_call(
        functools.partial(_fourier_kernel, n_ctx=n_ctx, n_lat=n_lat, tile_off=off),
        grid=(n_batch // ATT_BATCH, n_tok // TILE - off),
        in_specs=[pl.BlockSpec((ATT_BATCH, n_tok, GROUP_W), lambda b, i: (b, 0, 5)),
                  const(cs_bd.shape), const(dft.shape), const(dft_ctx.shape), const(w_f.shape)],
        out_specs=pl.BlockSpec((ATT_BATCH, TILE, GROUP_W), lambda b, i: (b, i, 0)),
        out_shape=_sds((n_batch, n_tok - off * TILE, GROUP_W), BF16),
        scratch_shapes=[pltpu.VMEM((ATT_BATCH, 2 * n_lat, GROUP_W), BF16)],
        compiler_params=_params(("arbitrary", "arbitrary")),
        name="fourier",
    )(qkv, cs_bd, dft, dft_ctx, w_f)


def _dft_matrix(n):
    f = 32
    assert n % f == 0
    k = lax.broadcasted_iota(jnp.int32, (n, 1), 0)
    m1 = lax.broadcasted_iota(jnp.int32, (1, n // f), 1)
    m0 = lax.broadcasted_iota(jnp.int32, (1, f), 1)
    ang_a = ((k * (m1 * f)) % n).astype(F32) * (2.0 * math.pi / n)
    ang_b = ((k * m0) % n).astype(F32) * (2.0 * math.pi / n)
    ca, sa = jnp.cos(ang_a)[:, :, None], jnp.sin(ang_a)[:, :, None]
    cb, sb = jnp.cos(ang_b)[:, None, :], jnp.sin(ang_b)[:, None, :]
    cos = (ca * cb - sa * sb).reshape(n, n)
    sin = (sa * cb + ca * sb).reshape(n, n)
    return jnp.concatenate([cos, -sin], axis=1).astype(BF16)


def _head_dft_matrix():
    a = np.arange(GROUP_W)
    same = (a[:, None] // FNO_HEAD_W) == (a[None, :] // FNO_HEAD_W)
    ang = 2.0 * np.pi * (((a[:, None] % FNO_HEAD_W) * (a[None, :] % FNO_HEAD_W)) % FNO_HEAD_W) / FNO_HEAD_W
    m = np.concatenate([np.where(same, np.cos(ang), 0.0), np.where(same, np.sin(ang), 0.0)], axis=1)
    return jnp.asarray(m, F32).astype(BF16)


def _s5_kernel(u_ref, a_ref, bd_ref, cd_ref, y_ref, bu_ref, hb_ref, h_ref, *, n_batch, reverse):
    @pl.when(pl.program_id(0) == 0)
    def _():
        h_ref[...] = jnp.zeros_like(h_ref)

    sub_rows = S5_SUB * n_batch
    chunk = 4 * LANES
    def project(q):
        u = pltpu.einshape("btc->tbc", u_ref[:, q * S5_SUB:(q + 1) * S5_SUB, :])
        u = u.reshape(sub_rows, GROUP_W).astype(BF16)
        bu_ref[q * sub_rows:(q + 1) * sub_rows, :] = jnp.dot(u, bd_ref[0], preferred_element_type=F32)

    def scan(q):
        steps = range(S5_SUB - 1, -1, -1) if reverse else range(S5_SUB)
        for sc in range(SSM_W // chunk):
            re = slice(sc * chunk, (sc + 1) * chunk)
            im = slice(SSM_W + sc * chunk, SSM_W + (sc + 1) * chunk)
            a_re = jnp.broadcast_to(a_ref[0, :, re], (n_batch, chunk))
            a_im = jnp.broadcast_to(a_ref[0, :, im], (n_batch, chunk))
            h_re, h_im = h_ref[:, re], h_ref[:, im]
            for t in steps:
                r = slice(q * sub_rows + t * n_batch, q * sub_rows + (t + 1) * n_batch)
                h_re, h_im = (a_re * h_re - a_im * h_im + bu_ref[r, re],
                              a_re * h_im + a_im * h_re + bu_ref[r, im])
                hb_ref[r, re] = h_re.astype(BF16)
                hb_ref[r, im] = h_im.astype(BF16)
            h_ref[:, re] = h_re
            h_ref[:, im] = h_im

    def readout(q):
        y = jnp.dot(hb_ref[q * sub_rows:(q + 1) * sub_rows, :], cd_ref[0], preferred_element_type=F32)
        y = pltpu.einshape("tbc->btc", y.reshape(S5_SUB, n_batch, GROUP_W))
        y_ref[:, q * S5_SUB:(q + 1) * S5_SUB, :] = y.astype(y_ref.dtype)

    order = list(range(S5_STEPS // S5_SUB))
    if reverse:
        order.reverse()
    project(order[0])
    for j, q in enumerate(order):
        if j + 1 < len(order):
            project(order[j + 1])
        scan(q)
        if j > 0:
            readout(order[j - 1])
    readout(order[-1])


def _s5(u, a_tab, b_dense, c_dense, *, n_ctx, reverse):
    n_batch, n_tok, _ = u.shape
    n_blk = n_tok // S5_STEPS
    n_cblk = n_ctx // S5_STEPS
    d = int(reverse)
    if reverse:
        blk = lambda s: jnp.where(s < n_cblk, n_cblk - 1 - s, n_blk - 1 - (s - n_cblk))
    else:
        blk = lambda s: s
    return pl.pallas_call(
        functools.partial(_s5_kernel, n_batch=n_batch, reverse=reverse),
        grid=(n_blk,),
        in_specs=[pl.BlockSpec((n_batch, S5_STEPS, GROUP_W), lambda s: (0, blk(s), 0)),
                  pl.BlockSpec((1, 1, 2 * SSM_W), lambda s: (d, 0, 0)),
                  pl.BlockSpec((1, GROUP_W, 2 * SSM_W), lambda s: (d, 0, 0)),
                  pl.BlockSpec((1, 2 * SSM_W, GROUP_W), lambda s: (d, 0, 0))],
        out_specs=pl.BlockSpec((n_batch, S5_STEPS, GROUP_W), lambda s: (0, blk(s), 0)),
        out_shape=_sds((n_batch, n_tok, GROUP_W), BF16),
        scratch_shapes=[pltpu.VMEM((S5_STEPS * n_batch, 2 * SSM_W), F32),
                        pltpu.VMEM((S5_STEPS * n_batch, 2 * SSM_W), BF16),
                        pltpu.VMEM((n_batch, 2 * SSM_W), F32)],
        compiler_params=_params(("arbitrary",)),
        name="s5",
    )(u, a_tab, b_dense, c_dense)


def _s5_tables(lam_re, lam_im, log_dt, b_re, b_im, c_re, c_im):
    lam = lax.complex(lam_re.astype(F32), lam_im.astype(F32))
    dt = jnp.exp(log_dt.astype(F32))[..., None]
    a_bar = jnp.exp(lam * dt)
    b_bar = ((a_bar - 1.0) / lam)[..., None] * lax.complex(b_re.astype(F32), b_im.astype(F32))
    eye = jnp.eye(SSM_GROUPS, dtype=F32)

    def embed_b(m):
        m = jnp.transpose(m, (0, 1, 3, 2))[:, :, :, None, :] * eye[None, :, None, :, None]
        return m.reshape(2, GROUP_W, SSM_W)

    def embed_c(m):
        m = jnp.transpose(m, (0, 1, 3, 2))[:, :, :, None, :] * eye[None, :, None, :, None]
        return m.reshape(2, SSM_W, GROUP_W)

    a_tab = jnp.concatenate([jnp.real(a_bar).reshape(2, 1, SSM_W), jnp.imag(a_bar).reshape(2, 1, SSM_W)], axis=-1)
    b_dense = jnp.concatenate([embed_b(jnp.real(b_bar)), embed_b(jnp.imag(b_bar))], axis=-1).astype(BF16)
    c_dense = jnp.concatenate([embed_c(c_re.astype(F32)), -embed_c(c_im.astype(F32))], axis=1).astype(BF16)
    return a_tab, b_dense, c_dense


def _gelu_tanh(x):
    return x * (0.5 * (1.0 + jnp.tanh(math.sqrt(2.0 / math.pi) * (x + 0.044715 * (x * x * x)))))


def _mixout_kernel(*refs, split_input, tiles_per_batch, tile_off):
    if split_input:
        x_ref, ctx_ref, *refs = refs
    else:
        x_ref, *refs = refs
    (mod_ref, oa_ref, od_ref, of_ref, u_ref, yf_ref, yb_ref, d_ref, wg_ref, bg_ref, gg_ref, wo_ref,
     g2_ref, w1_ref, w3_ref, w2_ref, o_ref, x1_ref, h_ref, ga2_ref) = refs
    s = pl.program_id(0)
    n_items = pl.num_programs(0) - 1

    def feed_forward():
        h = h_ref[...]
        acc = jnp.zeros((TILE, D_MODEL), F32)
        for c in range(D_FF // FF_CHUNK):
            cols = slice(c * FF_CHUNK, (c + 1) * FF_CHUNK)
            a = jnp.dot(h, w1_ref[:, cols], preferred_element_type=F32)
            b = jnp.dot(h, w3_ref[:, cols], preferred_element_type=F32)
            t = (a * jax.nn.sigmoid(a)) * b
            acc = acc + jnp.dot(t.astype(BF16), w2_ref[cols, :], preferred_element_type=F32)
        o_ref[0] = x1_ref[...] + ga2_ref[...] * acc

    def mix():
        x = x_ref[0]
        if split_input:
            is_ctx = lax.rem(jnp.minimum(s, n_items - 1), tiles_per_batch) + tile_off == 0
            x = jnp.where(is_ctx, ctx_ref[0], x)
        _, _, ga1 = _mod_slices(mod_ref[0], 0)
        sh2, sc2, ga2 = _mod_slices(mod_ref[0], 3)
        y = u_ref[0] * d_ref[...] + yf_ref[0].astype(F32) + yb_ref[0].astype(F32)
        g = _gelu_tanh(y)
        z = jnp.dot(g.astype(BF16), wg_ref[...], preferred_element_type=F32) + bg_ref[...]
        o_s = g * jax.nn.sigmoid(z)
        parts = [oa_ref[0].astype(F32), od_ref[0].astype(F32), of_ref[0].astype(F32), o_s]
        normed = [(_rms_rows(p) * gg_ref[:, k * GROUP_W:(k + 1) * GROUP_W]).astype(BF16)
                  for k, p in enumerate(parts)]
        r = jnp.dot(jnp.concatenate(normed, axis=-1), wo_ref[...], preferred_element_type=F32)
        x = x + ga1 * r
        x1_ref[...] = x
        h_ref[...] = ((_rms_rows(x) * g2_ref[...]) * (1.0 + sc2) + sh2).astype(BF16)
        ga2_ref[...] = ga2

    pl.when(s == 0)(mix)

    @pl.when(jnp.logical_and(s > 0, s < n_items))
    def _():
        feed_forward()
        mix()

    pl.when(s == n_items)(feed_forward)


def _mixout_ffn(xs, mod, o_a, o_d, o_f, u, y_fwd, y_bwd, d_skip, w_glu, b_glu, g_group, w_out, g2, w1, w3, w2, *,
                need_ctx, n_ctx):
    split_input = len(xs) == 2
    n_batch, n_tok, _ = u.shape
    off = 0 if need_ctx else n_ctx // TILE
    n_out = n_tok - off * TILE
    nt = n_out // TILE
    n_items = n_batch * nt
    tok = (1, TILE, D_MODEL)

    def item(fn):
        def index_map(s):
            a = jnp.minimum(s, n_items - 1)
            return fn(a // nt, a % nt)
        return index_map

    def out_map(s):
        a = jnp.maximum(s - 1, 0)
        return (a // nt, a % nt, 0)

    if split_input:
        assert need_ctx
        x_specs = [pl.BlockSpec(tok, item(lambda b, i: (b, jnp.maximum(i - 1, 0), 0))),
                   pl.BlockSpec(tok, item(lambda b, i: (b, 0, 0)))]
    else:
        x_specs = [pl.BlockSpec(tok, item(lambda b, i: (b, i + off, 0)))]
    grp = pl.BlockSpec((1, TILE, GROUP_W), item(lambda b, i: (b, i + off, 0)))
    mix = pl.BlockSpec((1, TILE, GROUP_W), item(lambda b, i: (b, i, 0)))
    const = lambda shape: pl.BlockSpec(shape, lambda s: (0,) * len(shape), pipeline_mode=pl.Buffered(1))
    return pl.pallas_call(
        functools.partial(_mixout_kernel, split_input=split_input, tiles_per_batch=nt, tile_off=off),
        grid=(n_items + 1,),
        in_specs=x_specs + [
            pl.BlockSpec((1, 1, N_MOD * D_MODEL), item(lambda b, i: (jnp.where(i + off == 0, n_batch, b), 0, 0))),
            mix, mix, mix, grp, grp, grp,
            const((1, GROUP_W)), const((GROUP_W, GROUP_W)), const((1, GROUP_W)),
            const((1, D_MODEL)), const((D_MODEL, D_MODEL)),
            const((1, D_MODEL)), const(w1.shape), const(w3.shape), const(w2.shape),
        ],
        out_specs=pl.BlockSpec(tok, out_map),
        out_shape=_sds((n_batch, n_out, D_MODEL), F32),
        scratch_shapes=[pltpu.VMEM((TILE, D_MODEL), F32), pltpu.VMEM((TILE, D_MODEL), BF16),
                        pltpu.VMEM((1, D_MODEL), F32)],
        compiler_params=_params(("arbitrary",)),
        name="mixffn",
    )(*xs, mod, o_a, o_d, o_f, u, y_fwd, y_bwd, d_skip, w_glu, b_glu, g_group, w_out, g2, w1, w3, w2)


def _rope_tables(n_lat, n_ctx):
    t = jnp.arange(n_lat, dtype=jnp.int32)
    rows = (t // GRID_W).astype(F32)
    cols = (t % GRID_W).astype(F32)
    axis_dim = HEAD_DIM // 2
    inv_freq = ROPE_THETA ** (-jnp.arange(0, axis_dim, 2, dtype=F32) / axis_dim)
    ang_r = rows[:, None] * inv_freq[None, :]
    ang_c = cols[:, None] * inv_freq[None, :]
    cos = jnp.concatenate([jnp.cos(ang_r)] * 2 + [jnp.cos(ang_c)] * 2, axis=-1)
    sin = jnp.concatenate([-jnp.sin(ang_r), jnp.sin(ang_r), -jnp.sin(ang_c), jnp.sin(ang_c)], axis=-1)
    cos = jnp.concatenate([jnp.ones((n_ctx, HEAD_DIM), F32), cos], axis=0)
    sin = jnp.concatenate([jnp.zeros((n_ctx, HEAD_DIM), F32), sin], axis=0)
    return jnp.tile(cos, (1, 2)), jnp.tile(sin, (1, 2))


def kernel(x, c, ctx, c_ctx, w_mod, b_mod, g_norm1, w_in, att_q_gain, att_k_gain, na_q_gain, na_k_gain, na_rel_bias, w_fourier, ssm_lam_re, ssm_lam_im, ssm_log_dt, ssm_b_re, ssm_b_im, ssm_c_re, ssm_c_im, ssm_d, w_glu, b_glu, g_group, w_out, g_norm2, w_ff1, w_ff3, w_ff2):
    n_batch, n_lat, d = x.shape
    n_ctx = ctx.shape[1]
    n_tok = n_lat + n_ctx
    assert d == D_MODEL and n_ctx == TILE and n_lat % TILE == 0 and n_lat % GRID_W == 0
    assert n_batch % 8 == 0 and n_tok % S5_STEPS == 0

    mod_rows = -(-(n_batch + 1) // 8) * 8
    cc = jnp.concatenate([c, c_ctx[None, :], jnp.zeros((mod_rows - n_batch - 1, d), F32)], axis=0)
    mod = _modulation(cc, w_mod, b_mod.reshape(DEPTH, 1, N_MOD * d)).reshape(DEPTH, mod_rows, 1, N_MOD * d)

    cos_t, sin_t = _rope_tables(n_lat, n_ctx)
    cs_bd = _head_dft_matrix()
    dft = _dft_matrix(n_lat)
    dft_ctx = _dft_matrix(n_ctx)
    def q_perm(a, axis):
        heads = [lax.slice_in_dim(a, h * HEAD_DIM, (h + 1) * HEAD_DIM, axis=axis) for h in _Q_ORDER]
        return jnp.concatenate(heads + [lax.slice_in_dim(a, GROUP_W, a.shape[axis], axis=axis)], axis=axis)

    scale = HEAD_DIM ** -0.5 * LOG2E
    tile2 = lambda g, s=1.0: jnp.tile(g.astype(F32) * s, 2)[None, :]
    row = lambda v: v.astype(F32)[None, :]

    xs = (x, ctx)
    for l in range(DEPTH):
        need_ctx = l < DEPTH - 1
        qkv, u = _inproj(xs, mod[l], row(g_norm1[l]), q_perm(w_in[l], 1).astype(BF16), cos_t, sin_t,
                         tile2(att_q_gain[l], scale), tile2(att_k_gain[l]),
                         tile2(na_q_gain[l], scale), tile2(na_k_gain[l]),
                         n_batch=n_batch, n_tok=n_tok)
        o_a = _gqa(_logits_bounded(att_q_gain[l], att_k_gain[l]), qkv, need_ctx=need_ctx, n_ctx=n_ctx)
        o_d = _na(_logits_bounded(na_q_gain[l], na_k_gain[l], jnp.max(jnp.abs(na_rel_bias[l]))), qkv,
                  _na_bias_table(na_rel_bias[l], n_lat // GRID_W), need_ctx=need_ctx, n_ctx=n_ctx)
        o_f = _fourier(qkv, cs_bd, dft, dft_ctx, w_fourier[l].astype(BF16), need_ctx=need_ctx, n_ctx=n_ctx)
        a_tab, b_dense, c_dense = _s5_tables(ssm_lam_re[l], ssm_lam_im[l], ssm_log_dt[l], ssm_b_re[l],
                                             ssm_b_im[l], ssm_c_re[l], ssm_c_im[l])
        y_fwd = _s5(u, a_tab, b_dense, c_dense, n_ctx=n_ctx, reverse=False)
        y_bwd = _s5(u, a_tab, b_dense, c_dense, n_ctx=n_ctx, reverse=True)
        x2 = _mixout_ffn(xs, mod[l], o_a, o_d, o_f, u, y_fwd, y_bwd, row(ssm_d[l]), w_glu[l].astype(BF16),
                         row(b_glu[l]), row(q_perm(g_group[l], 0)), q_perm(w_out[l], 0).astype(BF16),
                         row(g_norm2[l]), w_ff1[l].astype(BF16), w_ff3[l].astype(BF16), w_ff2[l].astype(BF16),
                         need_ctx=need_ctx, n_ctx=n_ctx)
        xs = (x2,)
    return xs[0]
```
